```python
import math
import jax, jax.numpy as jnp
from jax import lax
import numpy as np

D_MODEL = 1024
BATCH = 1
SEQ = 16384
DEPTH = 1
DEC_BATCH = 32
DEC_SEQ = 8
PAST_LEN = 16384
PAGE_SIZE = 128

HEAD_DIM = 64
FOX_HEADS = D_MODEL // 128
NSA_HEADS = D_MODEL // 128
NSA_KV_GROUPS = 2
NSA_HPG = NSA_HEADS // NSA_KV_GROUPS
CMP_BLOCK = 32
CMP_STRIDE = 16
CMP_HIDDEN = 2 * HEAD_DIM
SEL_BLOCK = 64
N_SEL = 16
WINDOW = 512
REL_BUCKETS = 32
REL_MAX_DIST = 128
D_FF = 4 * D_MODEL
Q_BLOCK = 128
EPS = 1e-6
FORGET_BIAS_INIT = 3.0
FOX_W = FOX_HEADS * HEAD_DIM
NSA_W = NSA_HEADS * HEAD_DIM
KV_W = NSA_KV_GROUPS * HEAD_DIM
IN_SIZES = (FOX_W, FOX_W, FOX_W, FOX_HEADS, NSA_W, 6 * KV_W, 3 * NSA_HEADS, 2 * D_MODEL)
IN_W = sum(IN_SIZES)

kernel_name = 'fox_nsa_parallel_hybrid_step'


def rmsnorm(x, g):
    xf = x.astype(jnp.float32)
    y = xf * lax.rsqrt(jnp.mean(xf * xf, axis=-1, keepdims=True) + EPS)
    return y.astype(x.dtype) * g


def masked_softmax(s, mask, axis):
    s = jnp.where(mask, s, -jnp.inf)
    m = jnp.max(s, axis=axis, keepdims=True)
    m = jnp.where(jnp.isfinite(m), m, 0.0)
    p = jnp.where(mask, jnp.exp(s - m), 0.0)
    return p / jnp.maximum(jnp.sum(p, axis=axis, keepdims=True), 1e-30)


def t5_bucket(dist):
    d = jnp.maximum(dist, 0)
    exact = REL_BUCKETS // 2
    far = exact + (jnp.log(jnp.maximum(d, 1).astype(jnp.float32) / exact)
                   / math.log(REL_MAX_DIST / exact) * (REL_BUCKETS - exact)).astype(jnp.int32)
    return jnp.where(d < exact, d, jnp.minimum(far, REL_BUCKETS - 1))


def group_bias(rel_bias, dist):
    b = rel_bias[t5_bucket(dist)].reshape(dist.shape + (NSA_KV_GROUPS, NSA_HPG))
    return jnp.moveaxis(b, 1, 3).astype(jnp.float32)


def ada_mod(c, w_ada, b_ada):
    m = jax.nn.silu(c) @ w_ada + b_ada
    return jnp.moveaxis(m.reshape(c.shape[0], 6, 1, D_MODEL), 1, 0)


def compress(rows, pe, w1, w2):
    n_chunk = rows.shape[0] // CMP_STRIDE
    r = CMP_BLOCK // CMP_STRIDE
    n_blk = n_chunk - r + 1
    chunks = rows.reshape(n_chunk, CMP_STRIDE, NSA_KV_GROUPS, HEAD_DIM)
    blocks = jnp.concatenate([chunks[i:i + n_blk] for i in range(r)], axis=1)
    blocks = blocks + pe[None, :, None, :]
    flat = jnp.swapaxes(blocks, 1, 2).reshape(n_blk, NSA_KV_GROUPS, CMP_BLOCK * HEAD_DIM)
    return jax.nn.gelu(flat @ w1) @ w2


def compressed_kv(rows, pe_cmp, w_cmp1, w_cmp2, g_kc):
    kc = rmsnorm(compress(rows[:, 0], pe_cmp[0], w_cmp1[0], w_cmp2[0]), g_kc)
    vc = compress(rows[:, 1], pe_cmp[1], w_cmp1[1], w_cmp2[1])
    cmp_end = jnp.arange(kc.shape[0]) * CMP_STRIDE + CMP_BLOCK - 1
    return kc, vc, cmp_end


def cmp_to_sel_weights(n_cmp, n_sel):
    c0 = jnp.arange(n_cmp)[:, None] * CMP_STRIDE
    s0 = jnp.arange(n_sel)[None, :] * SEL_BLOCK
    shared = jnp.minimum(c0 + CMP_BLOCK, s0 + SEL_BLOCK) - jnp.maximum(c0, s0)
    return jnp.maximum(shared, 0).astype(jnp.float32) / CMP_BLOCK


def fox_attend(q, cq, pos_q, k, v, ck, pos_k):
    s = jnp.einsum('qhd,khd->hqk', q, k).astype(jnp.float32) * (1.0 / math.sqrt(HEAD_DIM))
    s = s + (cq.T[:, :, None] - ck.T[:, None, :])
    mask = (pos_k[None, :] <= pos_q[:, None])[None]
    p = masked_softmax(s, mask, -1)
    return jnp.einsum('hqk,khd->qhd', p.astype(v.dtype), v)


def nsa_attend(q, gates, pos_q, kc, vc, cmp_end, ks_blk, vs_blk, kw, vw, pos_w, rel_bias):
    n_q = q.shape[0]
    scale = 1.0 / math.sqrt(HEAD_DIM)
    qg = q.reshape(n_q, NSA_KV_GROUPS, NSA_HPG, HEAD_DIM)
    dist_c = pos_q[:, None] - cmp_end[None, :]
    s_c = jnp.einsum('qgjd,ngd->qgjn', qg, kc).astype(jnp.float32) * scale + group_bias(rel_bias, dist_c)
    p_c = masked_softmax(s_c, (dist_c >= 0)[:, None, None, :], -1)
    o_c = jnp.einsum('qgjn,ngd->qgjd', p_c.astype(vc.dtype), vc)
    n_sel = ks_blk.shape[0]
    imp = jnp.einsum('qgn,nm->qgm', p_c.sum(axis=2), cmp_to_sel_weights(kc.shape[0], n_sel))
    blk = jnp.arange(n_sel)[None, :]
    cur = (pos_q // SEL_BLOCK)[:, None]
    forced = (blk == 0) | (blk == cur) | (blk == cur - 1)
    causal_blk = blk * SEL_BLOCK <= pos_q[:, None]
    score = jnp.where(forced[:, None, :], jnp.inf, imp)
    score = jnp.where(causal_blk[:, None, :], score, -jnp.inf)
    top_score, idx = lax.top_k(score, min(N_SEL, n_sel))
    sel_ok = top_score > -jnp.inf
    idx_g = jnp.moveaxis(idx, 1, 0)
    ks = jax.vmap(lambda b, i: b[i])(jnp.moveaxis(ks_blk, 2, 0), idx_g)
    vs = jax.vmap(lambda b, i: b[i])(jnp.moveaxis(vs_blk, 2, 0), idx_g)
    pos_s = idx[..., None] * SEL_BLOCK + jnp.arange(SEL_BLOCK)
    dist_s = pos_q[:, None, None, None] - pos_s
    table = rel_bias.reshape(REL_BUCKETS, NSA_KV_GROUPS, NSA_HPG)
    b_s = table[t5_bucket(dist_s), jnp.arange(NSA_KV_GROUPS)[None, :, None, None]]
    s_s = (jnp.einsum('qgjd,gqnkd->qgjnk', qg, ks).astype(jnp.float32) * scale
           + jnp.moveaxis(b_s, -1, 2).astype(jnp.float32))
    mask_s = (sel_ok[..., None] & (dist_s >= 0))[:, :, None]
    p_s = masked_softmax(s_s, mask_s, (-2, -1))
    o_s = jnp.einsum('qgjnk,gqnkd->qgjd', p_s.astype(vs.dtype), vs)
    dist_w = pos_q[:, None] - pos_w[None, :]
    s_w = jnp.einsum('qgjd,kgd->qgjk', qg, kw).astype(jnp.float32) * scale + group_bias(rel_bias, dist_w)
    mask_w = ((dist_w >= 0) & (dist_w <= WINDOW) & (pos_w >= 0)[None, :])[:, None, None, :]
    p_w = masked_softmax(s_w, mask_w, -1)
    o_w = jnp.einsum('qgjk,kgd->qgjd', p_w.astype(vw.dtype), vw)
    g = gates.reshape(n_q, NSA_KV_GROUPS, NSA_HPG, 1, 3)
    o = g[..., 0] * o_c + g[..., 1] * o_s + g[..., 2] * o_w
    return o.reshape(n_q, NSA_HEADS, HEAD_DIM)


def mixer_inputs(h, w_in, b_forget, g_qk_fox, g_qk_nsa):
    b, t, _ = h.shape
    z = h @ w_in
    qa, ka, va, zf, qb, zkv, zg, zm = jnp.split(z, np.cumsum(IN_SIZES)[:-1].tolist(), axis=-1)
    qa = rmsnorm(qa.reshape(b, t, FOX_HEADS, HEAD_DIM), g_qk_fox[0])
    ka = rmsnorm(ka.reshape(b, t, FOX_HEADS, HEAD_DIM), g_qk_fox[1])
    va = va.reshape(b, t, FOX_HEADS, HEAD_DIM)
    lfa = jax.nn.log_sigmoid((zf + b_forget).astype(jnp.float32))
    qb = rmsnorm(qb.reshape(b, t, NSA_HEADS, HEAD_DIM), g_qk_nsa[0])
    kv = zkv.reshape(b, t, 6, NSA_KV_GROUPS, HEAD_DIM)
    kv_nsa = jnp.stack([kv[:, :, 0], kv[:, :, 1], rmsnorm(kv[:, :, 2], g_qk_nsa[2]), kv[:, :, 3]], axis=2)
    kv_win = jnp.stack([rmsnorm(kv[:, :, 4], g_qk_nsa[3]), kv[:, :, 5]], axis=2)
    gb = jax.nn.sigmoid(zg.reshape(b, t, NSA_HEADS, 3))
    gm = jax.nn.sigmoid(zm.reshape(b, t, 2, D_MODEL))
    return qa, ka, va, lfa, qb, gb, kv_nsa, kv_win, gm


def merge_branches(oa, ob, gm, w_out_fox, w_out_nsa, w_out):
    b, t = oa.shape[:2]
    ya = oa.reshape(b, t, FOX_W) @ w_out_fox
    yb = ob.reshape(b, t, NSA_W) @ w_out_nsa
    return (gm[:, :, 0] * ya + gm[:, :, 1] * yb) @ w_out


def channel_mixer(x, mod, g, w_up, w_down):
    h = rmsnorm(x, g) * (1 + mod[4]) + mod[3]
    return x + mod[5] * (jnp.square(jax.nn.relu(h @ w_up)) @ w_down)


def mix_prompt_seq(qa, ka, va, lfa, qb, gb, kv_nsa, kv_win, pe_cmp, w_cmp1, w_cmp2, g_kc, rel_bias):
    s_len = qa.shape[0]
    pos = jnp.arange(s_len)
    cum = jnp.cumsum(lfa.astype(jnp.float32), axis=0)
    kc, vc, cmp_end = compressed_kv(kv_nsa, pe_cmp, w_cmp1, w_cmp2, g_kc)
    ks_blk = kv_nsa[:, 2].reshape(s_len // SEL_BLOCK, SEL_BLOCK, NSA_KV_GROUPS, HEAD_DIM)
    vs_blk = kv_nsa[:, 3].reshape(s_len // SEL_BLOCK, SEL_BLOCK, NSA_KV_GROUPS, HEAD_DIM)
    win_pad = jnp.pad(kv_win, ((WINDOW, 0), (0, 0), (0, 0), (0, 0)))

    def block(i):
        q0 = i * Q_BLOCK
        pq = q0 + jnp.arange(Q_BLOCK)
        take = lambda a: lax.dynamic_slice_in_dim(a, q0, Q_BLOCK, axis=0)
        oa = fox_attend(take(qa), take(cum), pq, ka, va, cum, pos)
        band = lax.dynamic_slice_in_dim(win_pad, q0, WINDOW + Q_BLOCK, axis=0)
        pw = q0 - WINDOW + jnp.arange(WINDOW + Q_BLOCK)
        ob = nsa_attend(take(qb), take(gb), pq, kc, vc, cmp_end, ks_blk, vs_blk,
                        band[:, 0], band[:, 1], pw, rel_bias)
        return oa, ob

    oa, ob = lax.map(block, jnp.arange(s_len // Q_BLOCK))
    return oa.reshape(s_len, FOX_HEADS, HEAD_DIM), ob.reshape(s_len, NSA_HEADS, HEAD_DIM)


def mix_sample_seq(pages, qa, ka, va, lfa, qb, gb, kv_nsa, kv_win, win_buf,
                   pool_fox_kv, pool_fox_logf, pool_nsa_kv, layer,
                   pe_cmp, w_cmp1, w_cmp2, g_kc, rel_bias):
    n_new = qa.shape[0]
    past_fox = pool_fox_kv[layer, pages]
    past_len = past_fox.shape[0] * past_fox.shape[1]
    past_fox = past_fox.reshape(past_len, 2, FOX_HEADS, HEAD_DIM)
    k = jnp.concatenate([past_fox[:, 0], ka], axis=0)
    v = jnp.concatenate([past_fox[:, 1], va], axis=0)
    lf = jnp.concatenate([pool_fox_logf[layer, pages].reshape(past_len, FOX_HEADS).astype(jnp.float32),
                          lfa.astype(jnp.float32)], axis=0)
    cum = jnp.cumsum(lf, axis=0)
    total = past_len + n_new
    pos = jnp.arange(total)
    pq = past_len + jnp.arange(n_new)
    oa = fox_attend(qa, cum[past_len:], pq, k, v, cum, pos)
    rows = jnp.concatenate([pool_nsa_kv[layer, pages].reshape(past_len, 4, NSA_KV_GROUPS, HEAD_DIM), kv_nsa], axis=0)
    padded = -(-total // SEL_BLOCK) * SEL_BLOCK
    rows = jnp.pad(rows, ((0, padded - total), (0, 0), (0, 0), (0, 0)))
    kc, vc, cmp_end = compressed_kv(rows, pe_cmp, w_cmp1, w_cmp2, g_kc)
    ks_blk = rows[:, 2].reshape(padded // SEL_BLOCK, SEL_BLOCK, NSA_KV_GROUPS, HEAD_DIM)
    vs_blk = rows[:, 3].reshape(padded // SEL_BLOCK, SEL_BLOCK, NSA_KV_GROUPS, HEAD_DIM)
    n_buf = win_buf.shape[0]
    band = jnp.concatenate([win_buf, kv_win], axis=0)
    pw = past_len - n_buf + jnp.arange(n_buf + n_new)
    ob = nsa_attend(qb, gb, pq, kc, vc, cmp_end, ks_blk, vs_blk, band[:, 0], band[:, 1], pw, rel_bias)
    return oa, ob, band[n_new:]


def setup_inputs(seed: int = 0) -> dict:
    key = jax.random.key(seed)
    ks = jax.random.split(key, 25)
    n_pages = PAST_LEN // PAGE_SIZE
    n_pool = (5 * DEC_BATCH * n_pages) // 4
    win_buf = min(WINDOW, PAST_LEN)
    nrm = lambda k, shape, scale: jax.random.normal(k, shape, jnp.float32) * scale
    perm = jax.random.permutation(ks[6], n_pool)
    return {
        'x_prompt': nrm(ks[0], (BATCH, SEQ, D_MODEL), 1.0),
        'x_sample': nrm(ks[1], (DEC_BATCH, DEC_SEQ, D_MODEL), 1.0),
        'cache_fox_kv': nrm(ks[2], (DEPTH, n_pool, PAGE_SIZE, 2, FOX_HEADS, HEAD_DIM), 1.0),
        'cache_fox_logf': jax.nn.log_sigmoid(nrm(ks[3], (DEPTH, n_pool, PAGE_SIZE, FOX_HEADS), 1.0) + FORGET_BIAS_INIT),
        'cache_nsa_kv': nrm(ks[4], (DEPTH, n_pool, PAGE_SIZE, 4, NSA_KV_GROUPS, HEAD_DIM), 1.0),
        'state_nsa_win': nrm(ks[5], (DEPTH, DEC_BATCH, win_buf, 2, NSA_KV_GROUPS, HEAD_DIM), 1.0),
        'page_table': perm[:DEC_BATCH * n_pages].reshape(DEC_BATCH, n_pages).astype(jnp.int32),
        'c_prompt': nrm(ks[7], (BATCH, D_MODEL), 1.0),
        'c_sample': nrm(ks[8], (DEC_BATCH, D_MODEL), 1.0),
        'w_ada': nrm(ks[9], (DEPTH, D_MODEL, 6 * D_MODEL), 0.5 * D_MODEL ** -0.5),
        'b_ada': nrm(ks[10], (DEPTH, 6 * D_MODEL), 0.02),
        'g_norm': 1.0 + nrm(ks[11], (DEPTH, 2, D_MODEL), 0.02),
        'w_in': nrm(ks[12], (DEPTH, D_MODEL, IN_W), D_MODEL ** -0.5),
        'b_forget': FORGET_BIAS_INIT + nrm(ks[13], (DEPTH, FOX_HEADS), 0.5),
        'g_qk_fox': 1.0 + nrm(ks[14], (DEPTH, 2, HEAD_DIM), 0.02),
        'g_qk_nsa': 1.0 + nrm(ks[15], (DEPTH, 4, HEAD_DIM), 0.02),
        'pe_cmp': nrm(ks[16], (DEPTH, 2, CMP_BLOCK, HEAD_DIM), 0.5),
        'w_cmp1': nrm(ks[17], (DEPTH, 2, CMP_BLOCK * HEAD_DIM, CMP_HIDDEN), (CMP_BLOCK * HEAD_DIM) ** -0.5),
        'w_cmp2': nrm(ks[18], (DEPTH, 2, CMP_HIDDEN, HEAD_DIM), CMP_HIDDEN ** -0.5),
        'rel_bias': nrm(ks[19], (REL_BUCKETS, NSA_HEADS), 0.5),
        'w_out_fox': nrm(ks[20], (DEPTH, FOX_W, D_MODEL), FOX_W ** -0.5),
        'w_out_nsa': nrm(ks[21], (DEPTH, NSA_W, D_MODEL), NSA_W ** -0.5),
        'w_out': nrm(ks[22], (DEPTH, D_MODEL, D_MODEL), D_MODEL ** -0.5),
        'w_up': nrm(ks[23], (DEPTH, D_MODEL, D_FF), D_MODEL ** -0.5),
        'w_down': nrm(ks[24], (DEPTH, D_FF, D_MODEL), D_FF ** -0.5),
    }


def reference(x_prompt, x_sample, cache_fox_kv, cache_fox_logf, cache_nsa_kv, state_nsa_win, page_table,
              c_prompt, c_sample, w_ada, b_ada, g_norm, w_in, b_forget, g_qk_fox, g_qk_nsa,
              pe_cmp, w_cmp1, w_cmp2, rel_bias, w_out_fox, w_out_nsa, w_out, w_up, w_down):
    y_p, y_s = x_prompt, x_sample
    fox_kv_p, fox_kv_s, fox_lf_p, fox_lf_s = [], [], [], []
    nsa_kv_p, nsa_kv_s, win_p, win_s = [], [], [], []
    for l in range(DEPTH):
        mod_p = ada_mod(c_prompt, w_ada[l], b_ada[l])
        h = rmsnorm(y_p, g_norm[l, 0]) * (1 + mod_p[1]) + mod_p[0]
        qa, ka, va, lfa, qb, gb, kv_nsa, kv_win, gm = mixer_inputs(h, w_in[l], b_forget[l], g_qk_fox[l], g_qk_nsa[l])
        prompt_fn = lambda *a: mix_prompt_seq(*a, pe_cmp[l], w_cmp1[l], w_cmp2[l], g_qk_nsa[l, 1], rel_bias)
        oa, ob = jax.vmap(prompt_fn)(qa, ka, va, lfa, qb, gb, kv_nsa, kv_win)
        y_p = y_p + mod_p[2] * merge_branches(oa, ob, gm, w_out_fox[l], w_out_nsa[l], w_out[l])
        y_p = channel_mixer(y_p, mod_p, g_norm[l, 1], w_up[l], w_down[l])
        fox_kv_p.append(jnp.stack([ka, va], axis=2))
        fox_lf_p.append(lfa)
        nsa_kv_p.append(kv_nsa)
        win_p.append(kv_win[:, -min(WINDOW, kv_win.shape[1]):])
        mod_s = ada_mod(c_sample, w_ada[l], b_ada[l])
        h = rmsnorm(y_s, g_norm[l, 0]) * (1 + mod_s[1]) + mod_s[0]
        qa, ka, va, lfa, qb, gb, kv_nsa, kv_win, gm = mixer_inputs(h, w_in[l], b_forget[l], g_qk_fox[l], g_qk_nsa[l])
        sample_fn = lambda a: mix_sample_seq(*a, cache_fox_kv, cache_fox_logf, cache_nsa_kv, l,
                                             pe_cmp[l], w_cmp1[l], w_cmp2[l], g_qk_nsa[l, 1], rel_bias)
        oa, ob, new_buf = lax.map(sample_fn, (page_table, qa, ka, va, lfa, qb, gb, kv_nsa, kv_win, state_nsa_win[l]))
        y_s = y_s + mod_s[2] * merge_branches(oa, ob, gm, w_out_fox[l], w_out_nsa[l], w_out[l])
        y_s = channel_mixer(y_s, mod_s, g_norm[l, 1], w_up[l], w_down[l])
        fox_kv_s.append(jnp.stack([ka, va], axis=2))
        fox_lf_s.append(lfa)
        nsa_kv_s.append(kv_nsa)
        win_s.append(new_buf)
    y_prompt = y_p
    y_sample = y_s
    new_fox_kv_prompt = jnp.stack(fox_kv_p)
    new_fox_kv_sample = jnp.stack(fox_kv_s)
    new_fox_logf_prompt = jnp.stack(fox_lf_p)
    new_fox_logf_sample = jnp.stack(fox_lf_s)
    new_nsa_kv_prompt = jnp.stack(nsa_kv_p)
    new_nsa_kv_sample = jnp.stack(nsa_kv_s)
    new_win_prompt = jnp.stack(win_p)
    new_win_sample = jnp.stack(win_s)
    return (y_prompt, y_sample, new_fox_kv_prompt, new_fox_kv_sample, new_fox_logf_prompt, new_fox_logf_sample,
            new_nsa_kv_prompt, new_nsa_kv_sample, new_win_prompt, new_win_sample)
```

```python
import functools
import math

import numpy as np
import jax
import jax.numpy as jnp
from jax import lax
from jax.experimental import pallas as pl
from jax.experimental.pallas import tpu as pltpu

F32 = jnp.float32
BF16 = jnp.bfloat16

D_MODEL = 1024
HEAD_DIM = 64
FOX_HEADS = 8
NSA_HEADS = 8
NSA_G = 2
NSA_HPG = 4
CMP_BLOCK = 32
CMP_STRIDE = 16
CMP_HIDDEN = 128
SEL_BLOCK = 64
N_SEL = 16
WINDOW = 512
REL_BUCKETS = 32
REL_MAX_DIST = 128
PAGE = 128
EPS = 1e-6
D_FF = 4 * D_MODEL

LANE = 128
LOG2E = 1.4426950408889634
QSCALE = LOG2E / math.sqrt(HEAD_DIM)
NEG = -float(2 ** 30)
M_INIT = -1e30
VMEM_LIMIT = 56 * 1024 * 1024

NSA_TQ = 256
NSA_TK = 256
WIN_BAND = WINDOW + NSA_TQ
SEL_NEAR = 2 * NSA_TK
CMP_NEAR = 32
FOX_TQ = 512
FOX_TK = 512
PREP_ROWS = 2048
FOX_DEC_PAGES = 8
NSA_DEC_PAGES = PREP_ROWS // PAGE


def _cparams(sem):
    return pltpu.CompilerParams(dimension_semantics=sem, vmem_limit_bytes=VMEM_LIMIT)


def _split2(x):
    hi = x.astype(BF16)
    lo = (x - hi.astype(F32)).astype(BF16)
    return hi, lo


def _split3(x):
    hi = x.astype(BF16)
    r = x - hi.astype(F32)
    mid = r.astype(BF16)
    lo = (r - mid.astype(F32)).astype(BF16)
    return hi, mid, lo


def _dot(a, b):
    return jnp.dot(a, b, preferred_element_type=F32)


def _dot_nt(a, b):
    return lax.dot_general(a, b, (((1,), (1,)), ((), ())), preferred_element_type=F32)


def _ada_kernel(c_ref, w_ref, b_ref, o_ref):
    c = c_ref[...]
    s = c * jax.nn.sigmoid(c)
    o_ref[...] = jnp.dot(s, w_ref[...], precision=lax.Precision.HIGHEST,
                         preferred_element_type=F32) + b_ref[...]


def _ada(c_all, w_ada, b_ada):
    rows = c_all.shape[0]
    n = w_ada.shape[1]
    tn = 1536
    return pl.pallas_call(
        _ada_kernel,
        grid=(n // tn,),
        in_specs=[pl.BlockSpec((rows, D_MODEL), lambda j: (0, 0)),
                  pl.BlockSpec((D_MODEL, tn), lambda j: (0, j)),
                  pl.BlockSpec((1, tn), lambda j: (0, j))],
        out_specs=pl.BlockSpec((rows, tn), lambda j: (0, j)),
        out_shape=jax.ShapeDtypeStruct((rows, n), F32),
        compiler_params=_cparams(("arbitrary",)),
        name="ada",
    )(c_all, w_ada, b_ada.reshape(1, n))


_C_QA, _C_KA, _C_VA, _C_QB = 0, 512, 1024, 1536
_C_KV = 2048
_C_F = 2816
_C_G = 2944
_C_M = 3072
_C_END = 5120


def _inproj_kernel(x_ref, sh_ref, sc_ref, g0_ref, w_ref, bd_ref, gq_ref, bf_ref,
                   foxkv_ref, nsakv_ref, kvwin_ref, lft_ref, qa_ref, kat_ref, va_ref,
                   qb_ref, kwt_ref, vw_ref, gb_ref, gm_ref):
    x = x_ref[...]
    ms = jnp.mean(x * x, axis=-1, keepdims=True)
    y = x * lax.rsqrt(ms + EPS) * g0_ref[...]
    h = y * (1.0 + sc_ref[...]) + sh_ref[...]
    hb = h.astype(BF16)
    bd = bd_ref[...]

    def head_norm(z, gain):
        zz = (z * z).astype(BF16)
        w = z.shape[1]
        if w == LANE:
            msq = _dot(zz, bd[:LANE, :LANE])
        else:
            msq = jnp.concatenate([_dot(zz[:, k:k + 256], bd) for k in range(0, w, 256)], axis=1)
        return z * lax.rsqrt(msq + EPS) * gain

    za = _dot(hb, w_ref[:, _C_QA:_C_KV])
    qa = head_norm(za[:, _C_QA:_C_KA], gq_ref[:, 0:512])
    ka = head_norm(za[:, _C_KA:_C_VA], gq_ref[:, 512:1024])
    va = za[:, _C_VA:_C_QB]
    qb = head_norm(za[:, _C_QB:_C_KV], gq_ref[:, 1024:1536])
    foxkv_ref[:, 0:512] = ka
    foxkv_ref[:, 512:1024] = va
    qa_ref[...] = (qa * QSCALE).astype(BF16)
    kat_ref[...] = ka.T.astype(BF16)
    va_ref[...] = va.astype(BF16)
    qb_ref[...] = (qb * QSCALE).astype(BF16)

    zb = _dot(hb, w_ref[:, _C_KV:_C_M])
    ksel = head_norm(zb[:, 256:384], gq_ref[:, 1536:1664])
    kwin = head_norm(zb[:, 512:640], gq_ref[:, 1664:1792])
    vwin = zb[:, 640:768]
    nsakv_ref[:, 0:256] = zb[:, 0:256]
    nsakv_ref[:, 256:384] = ksel
    nsakv_ref[:, 384:512] = zb[:, 384:512]
    kvwin_ref[:, 0:128] = kwin
    kvwin_ref[:, 128:256] = vwin
    kwt_ref[...] = kwin.T.astype(BF16)
    vw_ref[0] = vwin[:, 0:64].astype(BF16)
    vw_ref[1] = vwin[:, 64:128].astype(BF16)
    zf = zb[:, 768:896] + bf_ref[...]
    lf = jnp.minimum(zf, 0.0) - jnp.log(1.0 + jnp.exp(-jnp.abs(zf)))
    lft_ref[...] = lf.T[0:8, :]
    gb_ref[...] = jax.nn.sigmoid(zb[:, 896:1024])

    zm = _dot(hb, w_ref[:, _C_M:_C_END])
    gm_ref[...] = jax.nn.sigmoid(zm)


def _inproj(x, shift, scale, g0, w_cat, bd, gq, bf_pad):
    rows = x.shape[0]
    tr = 256
    mrows = shift.shape[0]
    mod_spec = (pl.BlockSpec((1, D_MODEL), lambda i: (0, 0)) if mrows == 1
                else pl.BlockSpec((tr, D_MODEL), lambda i: (i, 0)))
    const = lambda shape: pl.BlockSpec(shape, lambda i: tuple(0 for _ in shape))
    row_spec = lambda w: pl.BlockSpec((tr, w), lambda i: (i, 0))
    col_spec = lambda h: pl.BlockSpec((h, tr), lambda i: (0, i))
    out_shape = (
        jax.ShapeDtypeStruct((rows, 1024), F32),
        jax.ShapeDtypeStruct((rows, 512), F32),
        jax.ShapeDtypeStruct((rows, 256), F32),
        jax.ShapeDtypeStruct((8, rows), F32),
        jax.ShapeDtypeStruct((rows, 512), BF16),
        jax.ShapeDtypeStruct((512, rows), BF16),
        jax.ShapeDtypeStruct((rows, 512), BF16),
        jax.ShapeDtypeStruct((rows, 512), BF16),
        jax.ShapeDtypeStruct((128, rows), BF16),
        jax.ShapeDtypeStruct((NSA_G, rows, 64), BF16),
        jax.ShapeDtypeStruct((rows, 128), F32),
        jax.ShapeDtypeStruct((rows, 2048), F32),
    )
    out_specs = (row_spec(1024), row_spec(512), row_spec(256), col_spec(8), row_spec(512),
                 col_spec(512), row_spec(512), row_spec(512), col_spec(128),
                 pl.BlockSpec((NSA_G, tr, 64), lambda i: (0, i, 0)), row_spec(128), row_spec(2048))
    return pl.pallas_call(
        _inproj_kernel,
        grid=(rows // tr,),
        in_specs=[row_spec(D_MODEL), mod_spec, mod_spec, const((1, D_MODEL)),
                  const((D_MODEL, _C_END)), const((256, 256)), const((1, 1792)), const((1, 128))],
        out_specs=out_specs,
        out_shape=out_shape,
        compiler_params=_cparams(("arbitrary",)),
        name="inproj",
    )(x, shift, scale, g0, w_cat, bd, gq, bf_pad)


def _cumsum_kernel(x_ref, u_ref, pick_ref, bt_ref, o_ref):
    u = u_ref[...]
    c2 = sum(_dot(p, u) for p in _split3(x_ref[...]))
    pick = pick_ref[...]
    totb = sum(_dot(p, pick) for p in _split3(c2))
    bt = bt_ref[...]
    offs = sum(_dot(bt, p) for p in _split3(totb))
    o_ref[...] = (c2 + offs) * LOG2E


def _cumsum(lft):
    heads, s = lft.shape
    nblk = s // LANE
    rows = heads * nblk
    x = lft.reshape(rows, LANE)
    u = jnp.asarray(np.triu(np.ones((LANE, LANE), np.float32)), BF16)
    pick = np.zeros((LANE, LANE), np.float32)
    pick[LANE - 1, :] = 1.0
    r = np.arange(rows)
    bt = ((r[:, None] // nblk == r[None, :] // nblk) & (r[None, :] < r[:, None])).astype(np.float32)
    full = lambda shape: pl.BlockSpec(shape, lambda i: (0, 0))
    out = pl.pallas_call(
        _cumsum_kernel,
        grid=(1,),
        in_specs=[full((rows, LANE)), full((LANE, LANE)), full((LANE, LANE)), full((rows, rows))],
        out_specs=full((rows, LANE)),
        out_shape=jax.ShapeDtypeStruct((rows, LANE), F32),
        compiler_params=_cparams(("arbitrary",)),
        name="cumsum",
    )(x, u, jnp.asarray(pick, BF16), jnp.asarray(bt, BF16))
    return out.reshape(heads, nblk, LANE)


def _online_update(t, v, m_ref, l_ref, acc_ref):
    tk = t.shape[1]
    m_prev = m_ref[...]
    m_new = jnp.maximum(m_prev, jnp.max(t, axis=1, keepdims=True))
    alpha = jnp.exp2(m_prev - m_new)
    p = jnp.exp2(t - jnp.tile(m_new, (1, tk // LANE)))
    l_ref[...] = alpha * l_ref[...] + jnp.sum(p, axis=1, keepdims=True)
    dv = acc_ref.shape[1]
    a = alpha if dv == LANE else alpha[:, 0:dv]
    acc_ref[...] = acc_ref[...] * a + _dot(p.astype(BF16), v)
    m_ref[...] = m_new


def _fox_kernel(q_ref, kt_ref, v_ref, ck_ref, o_ref, m_ref, l_ref, acc_ref):
    i = pl.program_id(1)
    tq, tk = FOX_TQ, FOX_TK
    nsub = tk // LANE
    for hh in range(2):
        q = q_ref[:, 64 * hh:64 * hh + 64]
        m_ref[...] = jnp.full(m_ref.shape, M_INIT, F32)
        l_ref[...] = jnp.zeros(l_ref.shape, F32)
        acc_ref[...] = jnp.zeros(acc_ref.shape, F32)

        def tile(j, masked, hh=hh, q=q):
            koff = pl.multiple_of(j * tk, tk)
            kt = kt_ref[64 * hh:64 * hh + 64, pl.ds(koff, tk)]
            s = _dot(q, kt)
            ck = jnp.concatenate([ck_ref[hh, nsub * j + c] for c in range(nsub)], axis=1)
            t = (s.reshape(tq // 8, 8, tk) - ck[None]).reshape(tq, tk)
            if masked:
                row = lax.broadcasted_iota(jnp.int32, (tq, tk), 0)
                col = lax.broadcasted_iota(jnp.int32, (tq, tk), 1)
                t = jnp.where(row >= col, t, NEG)
            _online_update(t, v_ref[pl.ds(koff, tk), :], m_ref, l_ref, acc_ref)

        def body(j, carry):
            tile(j, False)
            return carry

        lax.fori_loop(0, i, body, 0)
        tile(i, True)
        o = acc_ref[...] / l_ref[...]
        o_ref[:, 64 * hh:64 * hh + 64] = o[:, 64 * hh:64 * hh + 64]


def _fox_prompt(qa, kat, va, ck_rep):
    s = qa.shape[0]
    nblk = s // LANE
    return pl.pallas_call(
        _fox_kernel,
        grid=(FOX_HEADS // 2, s // FOX_TQ),
        in_specs=[pl.BlockSpec((FOX_TQ, LANE), lambda hp, i: (i, hp)),
                  pl.BlockSpec((LANE, s), lambda hp, i: (hp, 0)),
                  pl.BlockSpec((s, LANE), lambda hp, i: (0, hp)),
                  pl.BlockSpec((2, nblk, 8, LANE), lambda hp, i: (hp, 0, 0, 0))],
        out_specs=pl.BlockSpec((FOX_TQ, LANE), lambda hp, i: (i, hp)),
        out_shape=jax.ShapeDtypeStruct((s, 512), F32),
        scratch_shapes=[pltpu.VMEM((FOX_TQ, LANE), F32), pltpu.VMEM((FOX_TQ, LANE), F32),
                        pltpu.VMEM((FOX_TQ, LANE), F32)],
        compiler_params=_cparams(("arbitrary", "arbitrary")),
        name="fox",
    )(qa, kat, va, ck_rep)


def _gelu_tanh(x):
    return 0.5 * x * (1.0 + jnp.tanh(math.sqrt(2.0 / math.pi) * (x + 0.044715 * (x * x * x))))


def _nsaprep_kernel(*refs, n_parts, n_prefetch):
    refs = refs[n_prefetch:]
    part_refs = refs[:n_parts]
    (look_ref, pe_ref, w1_ref, w2_ref, bd_ref, gkc_ref,
     kct_ref, vc_ref, kst_ref, vs_ref, xk_ref, xv_ref) = refs[n_parts:]
    xs_refs = (xk_ref, xv_ref)
    prow = PREP_ROWS // n_parts
    for k in range(n_parts):
        blk = part_refs[k][...].reshape(prow, 512)
        xk_ref[k * prow:(k + 1) * prow, :] = blk[:, 0:128]
        xv_ref[k * prow:(k + 1) * prow, :] = blk[:, 128:256]
        kst_ref[0, :, k * prow:(k + 1) * prow] = blk[:, 256:384].T.astype(BF16)
        vs_ref[0, 0, k * prow:(k + 1) * prow, :] = blk[:, 384:448].astype(BF16)
        vs_ref[0, 1, k * prow:(k + 1) * prow, :] = blk[:, 448:512].astype(BF16)
    look = look_ref[...].reshape(CMP_STRIDE, 512)
    xk_ref[PREP_ROWS:PREP_ROWS + CMP_STRIDE, :] = look[:, 0:128]
    xv_ref[PREP_ROWS:PREP_ROWS + CMP_STRIDE, :] = look[:, 128:256]

    ntok = PREP_ROWS // CMP_STRIDE
    outs = []
    for kv in range(2):
        hid = jnp.zeros((ntok, NSA_G * CMP_HIDDEN), F32)
        for half in range(2):
            xcat = jnp.concatenate(
                [xs_refs[kv][pl.ds(half * CMP_STRIDE + r, ntok, stride=CMP_STRIDE), :]
                 for r in range(CMP_STRIDE)], axis=1)
            xcat = (xcat + pe_ref[kv, half]).astype(BF16)
            hid = hid + _dot(xcat, w1_ref[kv, half])
        act = _gelu_tanh(hid).astype(BF16)
        outs.append(_dot(act, w2_ref[kv]))
    kc, vc = outs
    msq = _dot((kc * kc).astype(BF16), bd_ref[...])
    kc = kc * lax.rsqrt(msq + EPS) * gkc_ref[...]
    kct_ref[0] = kc.T.astype(BF16)
    vc_ref[0, 0] = vc[:, 0:64].astype(BF16)
    vc_ref[0, 1] = vc[:, 64:128].astype(BF16)


def _nsaprep(parts_arrays, part_specs, look_array, look_spec, grid, batch, length, prefetch,
             pe, w1, w2, bd128, gkc):
    n_parts = len(part_specs)
    ntok = PREP_ROWS // CMP_STRIDE
    ncp = length // CMP_STRIDE
    npf = len(prefetch)

    def cm(shape):
        return pl.BlockSpec(shape, lambda b, t, *pf: tuple(0 for _ in shape))

    in_specs = list(part_specs) + [look_spec, cm(pe.shape), cm(w1.shape), cm(w2.shape),
                                   cm(bd128.shape), cm(gkc.shape)]
    out_specs = (
        pl.BlockSpec((1, 128, ntok), lambda b, t, *pf: (b, 0, t)),
        pl.BlockSpec((1, NSA_G, ntok, 64), lambda b, t, *pf: (b, 0, t, 0)),
        pl.BlockSpec((1, 128, PREP_ROWS), lambda b, t, *pf: (b, 0, t)),
        pl.BlockSpec((1, NSA_G, PREP_ROWS, 64), lambda b, t, *pf: (b, 0, t, 0)),
    )
    out_shape = (
        jax.ShapeDtypeStruct((batch, 128, ncp), BF16),
        jax.ShapeDtypeStruct((batch, NSA_G, ncp, 64), BF16),
        jax.ShapeDtypeStruct((batch, 128, length), BF16),
        jax.ShapeDtypeStruct((batch, NSA_G, length, 64), BF16),
    )
    grid_spec = pltpu.PrefetchScalarGridSpec(
        num_scalar_prefetch=npf, grid=grid, in_specs=in_specs, out_specs=out_specs,
        scratch_shapes=[pltpu.VMEM((PREP_ROWS + CMP_STRIDE, LANE), F32),
                        pltpu.VMEM((PREP_ROWS + CMP_STRIDE, LANE), F32)])
    return pl.pallas_call(
        functools.partial(_nsaprep_kernel, n_parts=n_parts, n_prefetch=npf),
        grid_spec=grid_spec,
        out_shape=out_shape,
        compiler_params=_cparams(("arbitrary", "arbitrary")),
        name="nsaprep",
    )(*prefetch, *parts_arrays, look_array, pe, w1, w2, bd128, gkc)


def _nsa_kernel(*refs, tq, tile_base, has_tail, ncp, nbp):
    if has_tail:
        (q_ref, kct_ref, vc_ref, kst_ref, vs_ref, kstt_ref, vst_ref, kwt_ref, vw_ref, g_ref,
         tw_ref, ts_ref, tc_ref, etab_ref, wcs_ref, eg_ref,
         o_ref, sel_ref, m_ref, l_ref, acc_ref) = refs
    else:
        (q_ref, kct_ref, vc_ref, kst_ref, vs_ref, kwt_ref, vw_ref, g_ref,
         tw_ref, ts_ref, tc_ref, etab_ref, wcs_ref, eg_ref,
         o_ref, sel_ref, m_ref, l_ref, acc_ref) = refs
        kstt_ref = vst_ref = None
    rows = NSA_HPG * tq
    i_abs = tile_base + pl.program_id(2)
    q0 = i_abs * NSA_TQ
    q4 = q_ref[0]
    qhs = [q4[:, 64 * hh:64 * hh + 64] for hh in range(NSA_HPG)]
    qs = jnp.concatenate(qhs, axis=0)

    n_iota = lax.broadcasted_iota(jnp.int32, (1, ncp), 1)
    base = jnp.where(n_iota >= 16 * i_abs + 16, NEG, 0.0)
    pr = lax.broadcasted_iota(jnp.int32, (2 * CMP_NEAR, ncp), 0) % CMP_NEAR
    pn = lax.broadcasted_iota(jnp.int32, (2 * CMP_NEAR, ncp), 1)
    place = jnp.where(pn == 16 * i_abs - 16 + pr, 1.0, 0.0).astype(BF16)
    kct = kct_ref[0]
    vc = vc_ref[0, 0]
    psum = jnp.zeros((tq, ncp), F32)
    o_cmp = []
    for hh in range(NSA_HPG):
        s = _dot(qhs[hh], kct) + _dot(tc_ref[0, hh * tq:(hh + 1) * tq, :], place) + base
        m = jnp.max(s, axis=1, keepdims=True)
        p = jnp.exp2(s - m)
        l = jnp.sum(p, axis=1, keepdims=True)
        inv = jnp.where(m > 0.5 * NEG, 1.0 / l, 0.0)
        pn_ = p * inv
        psum = psum + pn_
        o_cmp.append(_dot(pn_.astype(BF16), vc))
    wcs = wcs_ref[...]
    imp = sum(_dot(part, wcs) for part in _split2(psum))

    posq = q0 + lax.broadcasted_iota(jnp.int32, (tq, 1), 0)
    blk = lax.broadcasted_iota(jnp.int32, (tq, nbp), 1)
    blk_f = blk.astype(F32)
    cur = posq // SEL_BLOCK
    forced = (blk == 0) | (blk == cur) | (blk == cur - 1)
    score = jnp.where(forced, jnp.inf, imp)
    score = jnp.where(blk * SEL_BLOCK <= posq, score, -jnp.inf)
    sel = jnp.full((tq, nbp), NEG, F32)
    for _ in range(N_SEL):
        mx = jnp.max(score, axis=1, keepdims=True)
        cand = jnp.where(score == mx, blk_f, float(4 * nbp))
        amin = jnp.min(cand, axis=1, keepdims=True)
        hit = blk_f == amin
        sel = jnp.where(hit & (mx > -jnp.inf), 0.0, sel)
        score = jnp.where(hit, -jnp.inf, score)
    sel_ref[...] = sel

    m_ref[...] = jnp.full(m_ref.shape, M_INIT, F32)
    l_ref[...] = jnp.zeros(l_ref.shape, F32)
    acc_ref[...] = jnp.zeros(acc_ref.shape, F32)

    def sel_tile(j, near_tbl, kt, v):
        s = _dot(qs, kt)
        c = j // 32
        r = j % 32
        sel128 = sel_ref[:, pl.ds(pl.multiple_of(c * LANE, LANE), LANE)]
        mb = _dot(sel128.astype(BF16), etab_ref[r])
        t = (s.reshape(NSA_HPG, tq, NSA_TK) + mb[None]).reshape(rows, NSA_TK)
        if near_tbl is not None:
            t = t + near_tbl
        _online_update(t, v, m_ref, l_ref, acc_ref)

    def main_kv(j):
        koff = pl.multiple_of(j * NSA_TK, NSA_TK)
        return kst_ref[0, :, pl.ds(koff, NSA_TK)], vs_ref[0, 0, pl.ds(koff, NSA_TK), :]

    def far_body(j, carry):
        kt, v = main_kv(j)
        sel_tile(j, None, kt, v)
        return carry

    lax.fori_loop(0, jnp.maximum(i_abs - 1, 0), far_body, 0)

    @pl.when(i_abs >= 1)
    def _():
        kt, v = main_kv(i_abs - 1)
        sel_tile(i_abs - 1, ts_ref[0, :, 0:NSA_TK], kt, v)

    if has_tail:
        sel_tile(i_abs, ts_ref[0, :, NSA_TK:SEL_NEAR], kstt_ref[0], vst_ref[0, 0])
    else:
        kt, v = main_kv(i_abs)
        sel_tile(i_abs, ts_ref[0, :, NSA_TK:SEL_NEAR], kt, v)
    o_sel = acc_ref[...] / l_ref[:, 0:64]

    band0 = pl.multiple_of((i_abs - tile_base if has_tail else i_abs) * NSA_TQ, NSA_TQ)
    ktw = kwt_ref[0, :, pl.ds(band0, WIN_BAND)]
    s = _dot(qs, ktw) + tw_ref[0]
    b_iota = lax.broadcasted_iota(jnp.int32, (1, WIN_BAND), 1)
    s = jnp.where(b_iota >= WINDOW - q0, s, NEG)
    m = jnp.max(s, axis=1, keepdims=True)
    p = jnp.exp2(s - m)
    l = jnp.sum(p, axis=1, keepdims=True)
    o_win = _dot(p.astype(BF16), vw_ref[0, 0, pl.ds(band0, WIN_BAND), :]) / l

    def heads_to_lanes(o):
        return jnp.concatenate([o[hh * tq:(hh + 1) * tq] for hh in range(NSA_HPG)], axis=1)

    branches = (jnp.concatenate(o_cmp, axis=1), heads_to_lanes(o_sel), heads_to_lanes(o_win))
    gparts = _split2(g_ref[0])
    out = jnp.zeros((tq, NSA_HPG * HEAD_DIM), F32)
    for br in range(3):
        gexp = sum(_dot(part, eg_ref[0, br]) for part in gparts)
        out = out + gexp * branches[br]
    o_ref[0] = out


def _nsa_attention(q, kct, vc, kst, vs, tail, kwt, vw, gates, tables, tq, tile_base):
    tw, ts, tc, etab, wcs, eg = tables
    batch, sq, _ = q.shape
    n_qt = sq // tq
    ncp = kct.shape[2]
    nbp = wcs.shape[1]
    skv = kst.shape[2]
    lw = kwt.shape[2]
    rows = NSA_HPG * tq
    has_tail = tail is not None
    in_specs = [
        pl.BlockSpec((1, tq, 256), lambda b, g, i: (b, i, g)),
        pl.BlockSpec((1, 64, ncp), lambda b, g, i: (b, g, 0)),
        pl.BlockSpec((1, 1, ncp, 64), lambda b, g, i: (b, g, 0, 0)),
        pl.BlockSpec((1, 64, skv), lambda b, g, i: (b, g, 0)),
        pl.BlockSpec((1, 1, skv, 64), lambda b, g, i: (b, g, 0, 0)),
    ]
    args = [q, kct, vc, kst, vs]
    if has_tail:
        in_specs += [pl.BlockSpec((1, 64, NSA_TK), lambda b, g, i: (b, g, 0)),
                     pl.BlockSpec((1, 1, NSA_TK, 64), lambda b, g, i: (b, g, 0, 0))]
        args += list(tail)
    in_specs += [
        pl.BlockSpec((1, 64, lw), lambda b, g, i: (b, g, 0)),
        pl.BlockSpec((1, 1, lw, 64), lambda b, g, i: (b, g, 0, 0)),
        pl.BlockSpec((1, tq, 128), lambda b, g, i: (b, i, 0)),
        pl.BlockSpec((1, rows, WIN_BAND), lambda b, g, i: (g, 0, 0)),
        pl.BlockSpec((1, rows, SEL_NEAR), lambda b, g, i: (g, 0, 0)),
        pl.BlockSpec((1, rows, 2 * CMP_NEAR), lambda b, g, i: (g, 0, 0)),
        pl.BlockSpec(etab.shape, lambda b, g, i: (0, 0, 0)),
        pl.BlockSpec(wcs.shape, lambda b, g, i: (0, 0)),
        pl.BlockSpec((1, 3, 128, 256), lambda b, g, i: (g, 0, 0, 0)),
    ]
    args += [kwt, vw, gates, tw, ts, tc, etab, wcs, eg]
    return pl.pallas_call(
        functools.partial(_nsa_kernel, tq=tq, tile_base=tile_base, has_tail=has_tail, ncp=ncp, nbp=nbp),
        grid=(batch, NSA_G, n_qt),
        in_specs=in_specs,
        out_specs=pl.BlockSpec((1, tq, 256), lambda b, g, i: (b, i, g)),
        out_shape=jax.ShapeDtypeStruct((batch, sq, 512), F32),
        scratch_shapes=[pltpu.VMEM((tq, nbp), F32), pltpu.VMEM((rows, LANE), F32),
                        pltpu.VMEM((rows, LANE), F32), pltpu.VMEM((rows, HEAD_DIM), F32)],
        compiler_params=_cparams(("arbitrary", "arbitrary", "arbitrary")),
        name="nsa",
    )(*args)


def _t5_bucket(dist):
    d = jnp.maximum(dist, 0)
    exact = REL_BUCKETS // 2
    far = exact + (jnp.log(jnp.maximum(d, 1).astype(F32) / exact)
                   / math.log(REL_MAX_DIST / exact) * (REL_BUCKETS - exact)).astype(jnp.int32)
    return jnp.where(d < exact, d, jnp.minimum(far, REL_BUCKETS - 1))


def _nsa_tables(rel_bias, tq, ncp, nbp):
    dmax = WINDOW + NSA_TQ + 1
    d = jnp.arange(dmax)
    by_dist = (rel_bias[_t5_bucket(d)] - rel_bias[REL_BUCKETS - 1][None, :]) * LOG2E
    a = np.arange(tq)

    def table(dist, valid):
        dist_c = np.clip(dist, 0, dmax - 1)
        t = by_dist[dist_c]
        t = jnp.where(jnp.asarray(valid)[:, :, None], t, NEG)
        t = jnp.transpose(t, (2, 0, 1)).reshape(NSA_G, NSA_HPG * tq, dist.shape[1])
        return t

    dw = a[:, None] + WINDOW - np.arange(WIN_BAND)[None, :]
    tw = table(dw, (dw >= 0) & (dw <= WINDOW))
    ds_ = a[:, None] + NSA_TK - np.arange(SEL_NEAR)[None, :]
    ts = table(ds_, ds_ >= 0)
    dc = a[:, None] - CMP_STRIDE * np.arange(CMP_NEAR)[None, :] + (NSA_TQ - CMP_BLOCK + 1)
    tc = table(dc, dc >= 0)
    tc_hi = tc.astype(BF16)
    tc_lo = (tc - tc_hi.astype(F32)).astype(BF16)
    tc2 = jnp.concatenate([tc_hi, tc_lo], axis=2)

    m = np.arange(LANE)[None, :, None]
    key = np.arange(NSA_TK)[None, None, :]
    r = np.arange(32)[:, None, None]
    etab = jnp.asarray((m == 4 * r + key // SEL_BLOCK).astype(np.float32), BF16)

    c0 = np.arange(ncp)[:, None] * CMP_STRIDE
    s0 = np.arange(nbp)[None, :] * SEL_BLOCK
    shared = np.minimum(c0 + CMP_BLOCK, s0 + SEL_BLOCK) - np.maximum(c0, s0)
    wcs = jnp.asarray(np.maximum(shared, 0).astype(np.float32) / CMP_BLOCK, BF16)

    eg = np.zeros((NSA_G, 3, 128, 256), np.float32)
    for g in range(NSA_G):
        for br in range(3):
            for hh in range(NSA_HPG):
                eg[g, br, 3 * (NSA_HPG * g + hh) + br, 64 * hh:64 * hh + 64] = 1.0
    return tw, ts, tc2, etab, wcs, jnp.asarray(eg, BF16)


def _foxdec_kernel(pt_ref, qbd_ref, knew_ref, vnew_ref, lfnew_ref, u_ref, *refs):
    npg = FOX_DEC_PAGES
    kv_refs = refs[:npg]
    lf_refs = refs[npg:2 * npg]
    o_ref, m_ref, l_ref, acc_ref, carry_ref = refs[2 * npg:]
    c = pl.program_id(1)
    nrow = FOX_HEADS * 8

    @pl.when(c == 0)
    def _():
        m_ref[...] = jnp.full(m_ref.shape, M_INIT, F32)
        l_ref[...] = jnp.zeros(l_ref.shape, F32)
        acc_ref[...] = jnp.zeros(acc_ref.shape, F32)
        carry_ref[...] = jnp.zeros(carry_ref.shape, F32)

    qbd = qbd_ref[0]
    u = u_ref[...]

    def cum_block(lf_blk):
        cb = sum(_dot(part, u) for part in _split3(lf_blk)) + carry_ref[...]
        carry_ref[...] = jnp.broadcast_to(cb[:, LANE - 1:LANE], cb.shape)
        return cb * LOG2E

    def step(s, ck, vs_list):
        t = (s.reshape(FOX_HEADS, 8, s.shape[1]) - ck[:, None, :]).reshape(nrow, s.shape[1])
        return t

    s_parts, ck_parts = [], []
    for k in range(npg):
        kpage = kv_refs[k][0, :, 0:512].astype(BF16)
        s_parts.append(_dot_nt(qbd, kpage))
        ck_parts.append(cum_block(lf_refs[k][0]))
    s = jnp.concatenate(s_parts, axis=1)
    ck = jnp.concatenate(ck_parts, axis=1)
    t = step(s, ck, None)
    tk = t.shape[1]
    m_prev = m_ref[...]
    m_new = jnp.maximum(m_prev, jnp.max(t, axis=1, keepdims=True))
    alpha = jnp.exp2(m_prev - m_new)
    p = jnp.exp2(t - jnp.tile(m_new, (1, tk // LANE)))
    l_ref[...] = alpha * l_ref[...] + jnp.sum(p, axis=1, keepdims=True)
    pv = sum(_dot(p[:, LANE * k:LANE * (k + 1)].astype(BF16), kv_refs[k][0, :, 512:1024].astype(BF16))
             for k in range(npg))
    acc_ref[...] = acc_ref[...] * jnp.tile(alpha, (1, 4)) + pv
    m_ref[...] = m_new

    @pl.when(c == pl.num_programs(1) - 1)
    def _():
        s2 = _dot_nt(qbd, knew_ref[0])
        ck2 = cum_block(lfnew_ref[0])
        t2 = step(s2, ck2, None)
        row = lax.broadcasted_iota(jnp.int32, (nrow, LANE), 0) % 8
        col = lax.broadcasted_iota(jnp.int32, (nrow, LANE), 1)
        t2 = jnp.where(col <= row, t2, NEG)
        m_prev2 = m_ref[...]
        m_new2 = jnp.maximum(m_prev2, jnp.max(t2, axis=1, keepdims=True))
        alpha2 = jnp.exp2(m_prev2 - m_new2)
        p2 = jnp.exp2(t2 - m_new2)
        l2 = alpha2 * l_ref[...] + jnp.sum(p2, axis=1, keepdims=True)
        acc2 = acc_ref[...] * jnp.tile(alpha2, (1, 4)) + _dot(p2.astype(BF16), vnew_ref[0])
        o = acc2 / jnp.tile(l2, (1, 4))
        for h in range(FOX_HEADS):
            o_ref[0, :, 64 * h:64 * h + 64] = o[8 * h:8 * h + 8, 64 * h:64 * h + 64]


def _fox_decode(page_table, qbd, knew, vnew, lfnew, kv_pool, lft_pool):
    batch, n_pages = page_table.shape
    npg = FOX_DEC_PAGES
    u = jnp.asarray(np.triu(np.ones((LANE, LANE), np.float32)), BF16)

    def seq(shape):
        return pl.BlockSpec(shape, lambda b, c, pt: (b,) + tuple(0 for _ in shape[1:]))

    def page(shape, k):
        return pl.BlockSpec(shape, lambda b, c, pt, k=k: (pt[b, c * npg + k],) + tuple(0 for _ in shape[1:]))

    in_specs = ([seq((1, 64, 512)), seq((1, LANE, 512)), seq((1, LANE, 512)), seq((1, 8, LANE)),
                 pl.BlockSpec((LANE, LANE), lambda b, c, pt: (0, 0))]
                + [page((1, PAGE, 1024), k) for k in range(npg)]
                + [page((1, 8, PAGE), k) for k in range(npg)])
    grid_spec = pltpu.PrefetchScalarGridSpec(
        num_scalar_prefetch=1, grid=(batch, n_pages // npg), in_specs=in_specs,
        out_specs=seq((1, 8, 512)),
        scratch_shapes=[pltpu.VMEM((64, LANE), F32), pltpu.VMEM((64, LANE), F32),
                        pltpu.VMEM((64, 512), F32), pltpu.VMEM((8, LANE), F32)])
    return pl.pallas_call(
        _foxdec_kernel,
        grid_spec=grid_spec,
        out_shape=jax.ShapeDtypeStruct((batch, 8, 512), F32),
        compiler_params=_cparams(("arbitrary", "arbitrary")),
        name="foxdec",
    )(page_table, qbd, knew, vnew, lfnew, u, *([kv_pool] * npg), *([lft_pool] * npg))


def _outmlp_kernel(x_ref, oa_ref, ob_ref, gm_ref, m2_ref, m3_ref, m4_ref, m5_ref, g1_ref,
                   wf_ref, wn_ref, wo_ref, wu_ref, wd_ref, y_ref, y1_ref, h2_ref, acc_ref):
    c = pl.program_id(1)

    @pl.when(c == 0)
    def _():
        ya = _dot(oa_ref[...].astype(BF16), wf_ref[...])
        yb = _dot(ob_ref[...].astype(BF16), wn_ref[...])
        u = gm_ref[:, 0:D_MODEL] * ya + gm_ref[:, D_MODEL:2 * D_MODEL] * yb
        att = _dot(u.astype(BF16), wo_ref[...])
        y1 = x_ref[...] + m2_ref[...] * att
        y1_ref[...] = y1
        ms = jnp.mean(y1 * y1, axis=-1, keepdims=True)
        hn = y1 * lax.rsqrt(ms + EPS) * g1_ref[...]
        h2_ref[...] = (hn * (1.0 + m4_ref[...]) + m3_ref[...]).astype(BF16)
        acc_ref[...] = jnp.zeros(acc_ref.shape, F32)

    up = _dot(h2_ref[...], wu_ref[...])
    act = jnp.square(jnp.maximum(up, 0.0)).astype(BF16)
    acc_ref[...] += _dot(act, wd_ref[...])

    @pl.when(c == pl.num_programs(1) - 1)
    def _():
        y_ref[...] = y1_ref[...] + m5_ref[...] * acc_ref[...]


def _out_mlp(x, oa, ob, gm, mods, g1, wf, wn, wo, wu, wd, tr):
    rows = x.shape[0]
    ffc = 1024
    mrows = mods[0].shape[0]
    mod_spec = (pl.BlockSpec((1, D_MODEL), lambda i, c: (0, 0)) if mrows == 1
                else pl.BlockSpec((tr, D_MODEL), lambda i, c: (i, 0)))
    row = lambda w: pl.BlockSpec((tr, w), lambda i, c: (i, 0))
    const = lambda shape: pl.BlockSpec(shape, lambda i, c: (0, 0))
    return pl.pallas_call(
        _outmlp_kernel,
        grid=(rows // tr, D_FF // ffc),
        in_specs=[row(D_MODEL), row(512), row(512), row(2048), mod_spec, mod_spec, mod_spec, mod_spec,
                  const((1, D_MODEL)), const((512, D_MODEL)), const((512, D_MODEL)),
                  const((D_MODEL, D_MODEL)),
                  pl.BlockSpec((D_MODEL, ffc), lambda i, c: (0, c)),
                  pl.BlockSpec((ffc, D_MODEL), lambda i, c: (c, 0))],
        out_specs=row(D_MODEL),
        out_shape=jax.ShapeDtypeStruct((rows, D_MODEL), F32),
        scratch_shapes=[pltpu.VMEM((tr, D_MODEL), F32), pltpu.VMEM((tr, D_MODEL), BF16),
                        pltpu.VMEM((tr, D_MODEL), F32)],
        compiler_params=_cparams(("arbitrary", "arbitrary")),
        name="outmlp",
    )(x, oa, ob, gm, *mods, g1, wf, wn, wo, wu, wd)


def _prep_w_in(w_in):
    qa, ka, va, zf, qb, zkv, zg, zm = jnp.split(
        w_in, np.cumsum([512, 512, 512, 8, 512, 768, 24, 2048])[:-1].tolist(), axis=1)
    pad = lambda w: jnp.pad(w, ((0, 0), (0, 128 - w.shape[1])))
    return jnp.concatenate([qa, ka, va, qb, zkv, pad(zf), pad(zg), zm], axis=1).astype(BF16)


def _prep_cmp_weights(pe_cmp, w_cmp1, w_cmp2):
    eye = jnp.eye(NSA_G, dtype=F32)
    w1 = w_cmp1.reshape(2, 2, CMP_STRIDE, HEAD_DIM, CMP_HIDDEN)
    w1big = jnp.einsum('khrdj,gq->khrgdqj', w1, eye).reshape(
        2, 2, CMP_STRIDE * NSA_G * HEAD_DIM, NSA_G * CMP_HIDDEN)
    w2big = jnp.einsum('kjd,gq->kgjqd', w_cmp2, eye).reshape(2, NSA_G * CMP_HIDDEN, NSA_G * HEAD_DIM)
    pe = pe_cmp.reshape(2, 2, CMP_STRIDE, 1, HEAD_DIM)
    pe = jnp.broadcast_to(pe, (2, 2, CMP_STRIDE, NSA_G, HEAD_DIM)).reshape(2, 2, 1, CMP_STRIDE * NSA_G * HEAD_DIM)
    return pe, w1big.astype(BF16), w2big.astype(BF16)


def _block_diag_mean(n):
    r = np.arange(n)
    return jnp.asarray((r[:, None] // HEAD_DIM == r[None, :] // HEAD_DIM).astype(np.float32) / HEAD_DIM, BF16)


def kernel(x_prompt, x_sample, cache_fox_kv, cache_fox_logf, cache_nsa_kv, state_nsa_win, page_table,
           c_prompt, c_sample, w_ada, b_ada, g_norm, w_in, b_forget, g_qk_fox, g_qk_nsa,
           pe_cmp, w_cmp1, w_cmp2, rel_bias, w_out_fox, w_out_nsa, w_out, w_up, w_down):
    n_batch, seq, _ = x_prompt.shape
    dec_batch, dec_seq, _ = x_sample.shape
    n_pool = cache_fox_kv.shape[1]
    n_pages = page_table.shape[1]
    past = n_pages * PAGE
    assert n_batch == 1 and dec_seq == 8 and w_ada.shape[0] == 1
    assert seq % PREP_ROWS == 0 and past % PREP_ROWS == 0 and state_nsa_win.shape[2] == WINDOW

    n_c = 1 + dec_batch
    c_rows = -(-n_c // 8) * 8
    c_all = jnp.pad(jnp.concatenate([c_prompt, c_sample], axis=0), ((0, c_rows - n_c), (0, 0)))
    mods = _ada(c_all, w_ada[0], b_ada[0])
    mod_p = [mods[0:1, k * D_MODEL:(k + 1) * D_MODEL] for k in range(6)]
    mod_s = [jnp.repeat(mods[1:n_c, k * D_MODEL:(k + 1) * D_MODEL], dec_seq, axis=0) for k in range(6)]

    w_cat = _prep_w_in(w_in[0])
    bd256 = _block_diag_mean(256)
    tile8 = lambda g: jnp.tile(g, 8)
    gq = jnp.concatenate([tile8(g_qk_fox[0, 0]), tile8(g_qk_fox[0, 1]), tile8(g_qk_nsa[0, 0]),
                          jnp.tile(g_qk_nsa[0, 2], 2), jnp.tile(g_qk_nsa[0, 3], 2)]).reshape(1, 1792)
    bf_pad = jnp.pad(b_forget[0], (0, 120)).reshape(1, 128)
    g0 = g_norm[0, 0].reshape(1, D_MODEL)
    xp = x_prompt.reshape(seq, D_MODEL)
    xs = x_sample.reshape(dec_batch * dec_seq, D_MODEL)
    (p_foxkv, p_nsakv, p_kvwin, p_lft, p_qa, p_kat, p_va, p_qb, p_kwt, p_vw, p_gb, p_gm) = _inproj(
        xp, mod_p[0], mod_p[1], g0, w_cat, bd256, gq, bf_pad)
    (s_foxkv, s_nsakv, s_kvwin, s_lft, s_qa, _, _, s_qb, _, _, s_gb, s_gm) = _inproj(
        xs, mod_s[0], mod_s[1], g0, w_cat, bd256, gq, bf_pad)

    ck = _cumsum(p_lft)
    ck_rep = jnp.broadcast_to(ck[:, :, None, :], (FOX_HEADS, seq // LANE, 8, LANE))
    oa_p = _fox_prompt(p_qa, p_kat, p_va, ck_rep)

    pe, w1big, w2big = _prep_cmp_weights(pe_cmp[0], w_cmp1[0], w_cmp2[0])
    bd128 = _block_diag_mean(128)
    gkc = jnp.tile(g_qk_nsa[0, 1], 2).reshape(1, 128)
    look_blocks = PREP_ROWS // CMP_STRIDE
    n_steps_p = seq // PREP_ROWS
    last_look_p = seq // CMP_STRIDE - 1
    kct_p, vc_p, kst_p, vs_p = _nsaprep(
        [p_nsakv], [pl.BlockSpec((PREP_ROWS, 512), lambda b, t: (t, 0))],
        p_nsakv, pl.BlockSpec((CMP_STRIDE, 512), lambda b, t: (jnp.minimum((t + 1) * look_blocks, last_look_p), 0)),
        (1, n_steps_p), 1, seq, (), pe, w1big, w2big, bd128, gkc)
    tables_p = _nsa_tables(rel_bias, NSA_TQ, seq // CMP_STRIDE, max(seq // SEL_BLOCK, LANE))
    kwt_pad = jnp.pad(p_kwt, ((0, 0), (WINDOW, 0)))[None]
    vw_pad = jnp.pad(p_vw, ((0, 0), (WINDOW, 0), (0, 0)))[None]
    ob_p = _nsa_attention(p_qb[None], kct_p, vc_p, kst_p, vs_p, None, kwt_pad, vw_pad, p_gb[None],
                          tables_p, NSA_TQ, 0)[0]

    kv_pool = cache_fox_kv.reshape(n_pool, PAGE, 1024)
    lft_pool = jnp.transpose(cache_fox_logf[0], (0, 2, 1))
    eye_h = jnp.eye(FOX_HEADS, dtype=BF16)
    qa_s = s_qa.reshape(dec_batch, dec_seq, FOX_HEADS, HEAD_DIM)
    qbd = jnp.einsum('bjhd,hk->bhjkd', qa_s, eye_h).reshape(dec_batch, 64, 512)
    pad_rows = lambda a: jnp.pad(a, ((0, 0), (0, LANE - dec_seq), (0, 0)))
    knew = pad_rows(s_foxkv[:, 0:512].reshape(dec_batch, dec_seq, 512)).astype(BF16)
    vnew = pad_rows(s_foxkv[:, 512:1024].reshape(dec_batch, dec_seq, 512)).astype(BF16)
    lfnew = jnp.pad(jnp.transpose(s_lft.reshape(8, dec_batch, dec_seq), (1, 0, 2)),
                    ((0, 0), (0, 0), (0, LANE - dec_seq)))
    oa_s = _fox_decode(page_table, qbd, knew, vnew, lfnew, kv_pool, lft_pool)

    nsa_pool = cache_nsa_kv.reshape(n_pool, PAGE, 512)
    npg = NSA_DEC_PAGES
    part_specs = [pl.BlockSpec((1, PAGE, 512), lambda b, t, pt, k=k: (pt[b, t * npg + k], 0, 0))
                  for k in range(npg)]
    look_spec = pl.BlockSpec(
        (1, CMP_STRIDE, 512), lambda b, t, pt: (pt[b, jnp.minimum((t + 1) * npg, n_pages - 1)], 0, 0))
    kct_s, vc_s, kst_s, vs_s = _nsaprep(
        [nsa_pool] * npg, part_specs, nsa_pool, look_spec,
        (dec_batch, past // PREP_ROWS), dec_batch, past, (page_table,), pe, w1big, w2big, bd128, gkc)
    tile_base = past // NSA_TQ
    nb_dec = past // SEL_BLOCK + NSA_TK // SEL_BLOCK
    tables_s = _nsa_tables(rel_bias, dec_seq, past // CMP_STRIDE, -(-nb_dec // LANE) * LANE)
    new_nsa = s_nsakv.reshape(dec_batch, dec_seq, 512)
    pad_tail = lambda a: jnp.pad(a, ((0, 0), (0, NSA_TK - dec_seq), (0, 0)))
    kst_tail = jnp.transpose(pad_tail(new_nsa[:, :, 256:384]), (0, 2, 1)).astype(BF16)
    vs_tail = jnp.transpose(pad_tail(new_nsa[:, :, 384:512]).reshape(dec_batch, NSA_TK, NSA_G, 64),
                            (0, 2, 1, 3)).astype(BF16)
    band = jnp.concatenate([state_nsa_win[0].reshape(dec_batch, WINDOW, 256),
                            s_kvwin.reshape(dec_batch, dec_seq, 256)], axis=1)
    band_pad = jnp.pad(band, ((0, 0), (0, WIN_BAND - band.shape[1]), (0, 0)))
    kwt_s = jnp.transpose(band_pad[:, :, 0:128], (0, 2, 1)).astype(BF16)
    vw_s = jnp.transpose(band_pad[:, :, 128:256].reshape(dec_batch, WIN_BAND, NSA_G, 64), (0, 2, 1, 3)).astype(BF16)
    ob_s = _nsa_attention(s_qb.reshape(dec_batch, dec_seq, 512), kct_s, vc_s, kst_s, vs_s,
                          (kst_tail, vs_tail), kwt_s, vw_s, s_gb.reshape(dec_batch, dec_seq, 128),
                          tables_s, dec_seq, tile_base)

    g1 = g_norm[0, 1].reshape(1, D_MODEL)
    wf, wn, wo = w_out_fox[0].astype(BF16), w_out_nsa[0].astype(BF16), w_out[0].astype(BF16)
    wu, wd = w_up[0].astype(BF16), w_down[0].astype(BF16)
    y_p = _out_mlp(xp, oa_p, ob_p, p_gm, mod_p[2:6], g1, wf, wn, wo, wu, wd, 512)
    y_s = _out_mlp(xs, oa_s.reshape(dec_batch * dec_seq, 512), ob_s.reshape(dec_batch * dec_seq, 512),
                   s_gm, mod_s[2:6], g1, wf, wn, wo, wu, wd, dec_batch * dec_seq)

    win_keep = min(WINDOW, seq)
    new_win_sample = band[:, dec_seq:].reshape(1, dec_batch, WINDOW, 2, NSA_G, HEAD_DIM)
    return (
        y_p.reshape(1, seq, D_MODEL),
        y_s.reshape(dec_batch, dec_seq, D_MODEL),
        p_foxkv.reshape(1, 1, seq, 2, FOX_HEADS, HEAD_DIM),
        s_foxkv.reshape(1, dec_batch, dec_seq, 2, FOX_HEADS, HEAD_DIM),
        jnp.transpose(p_lft).reshape(1, 1, seq, FOX_HEADS),
        jnp.transpose(s_lft).reshape(1, dec_batch, dec_seq, FOX_HEADS),
        p_nsakv.reshape(1, 1, seq, 4, NSA_G, HEAD_DIM),
        s_nsakv.reshape(1, dec_batch, dec_seq, 4, NSA_G, HEAD_DIM),
        p_kvwin[seq - win_keep:].reshape(1, 1, win_keep, 2, NSA_G, HEAD_DIM),
        new_win_sample,
    )
```

```python
import functools
import math

import numpy as np
import jax
import jax.numpy as jnp
from jax import lax
from jax.experimental import pallas as pl
from jax.experimental.pallas import tpu as pltpu

F32 = jnp.float32
BF16 = jnp.bfloat16

D_MODEL = 1024
HEAD_DIM = 64
FOX_HEADS = 8
NSA_HEADS = 8
NSA_G = 2
NSA_HPG = 4
CMP_BLOCK = 32
CMP_STRIDE = 16
CMP_HIDDEN = 128
SEL_BLOCK = 64
N_SEL = 16
N_FORCED = 3
WINDOW = 512
REL_BUCKETS = 32
REL_MAX_DIST = 128
PAGE = 128
EPS = 1e-6
D_FF = 4 * D_MODEL

LANE = 128
LOG2E = 1.4426950408889634
QSCALE = LOG2E / math.sqrt(HEAD_DIM)
NEG = -float(2 ** 30)
M_INIT = -1e30
VMEM_LIMIT = 58 * 1024 * 1024

NSA_TQ = 256
NSA_TK = 256
NSA_FAR = 1024
WIN_BAND = WINDOW + NSA_TQ
SEL_NEAR = 2 * NSA_TK
CMP_NEAR = 32
SEL_SLAB = LANE * SEL_BLOCK
FOX_TQ = 512
FOX_TK = 512
FOX_FAR = 1024
PREP_ROWS = 2048
FOX_DEC_PAGES = 8
NSA_DEC_PAGES = PREP_ROWS // PAGE


def _cparams(sem):
    return pltpu.CompilerParams(dimension_semantics=sem, vmem_limit_bytes=VMEM_LIMIT)


def _split2(x):
    hi = x.astype(BF16)
    lo = (x - hi.astype(F32)).astype(BF16)
    return hi, lo


def _split3(x):
    hi = x.astype(BF16)
    r = x - hi.astype(F32)
    mid = r.astype(BF16)
    lo = (r - mid.astype(F32)).astype(BF16)
    return hi, mid, lo


def _dot(a, b):
    return jnp.dot(a, b, preferred_element_type=F32)


def _dot_nt(a, b):
    return lax.dot_general(a, b, (((1,), (1,)), ((), ())), preferred_element_type=F32)


def _ada_kernel(c_ref, w_ref, b_ref, o_ref):
    c = c_ref[...]
    s = c * jax.nn.sigmoid(c)
    o_ref[...] = jnp.dot(s, w_ref[...], precision=lax.Precision.HIGHEST,
                         preferred_element_type=F32) + b_ref[...]


def _ada(c_all, w_ada, b_ada):
    rows = c_all.shape[0]
    n = w_ada.shape[1]
    tn = 1536
    return pl.pallas_call(
        _ada_kernel,
        grid=(n // tn,),
        in_specs=[pl.BlockSpec((rows, D_MODEL), lambda j: (0, 0)),
                  pl.BlockSpec((D_MODEL, tn), lambda j: (0, j)),
                  pl.BlockSpec((1, tn), lambda j: (0, j))],
        out_specs=pl.BlockSpec((rows, tn), lambda j: (0, j)),
        out_shape=jax.ShapeDtypeStruct((rows, n), F32),
        compiler_params=_cparams(("arbitrary",)),
        name="ada",
    )(c_all, w_ada, b_ada.reshape(1, n))


_C_QA, _C_KA, _C_VA, _C_QB = 0, 512, 1024, 1536
_C_KV = 2048
_C_F = 2816
_C_G = 2944
_C_M = 3072
_C_END = 5120


def _inproj_kernel(x_ref, sh_ref, sc_ref, g0_ref, w_ref, bd_ref, gq_ref, bf_ref,
                   foxkv_ref, nsakv_ref, kvwin_ref, lft_ref, qa_ref, kat_ref, va_ref,
                   qb_ref, kwt_ref, vwt_ref, gb_ref, gm_ref):
    x = x_ref[...]
    ms = jnp.mean(x * x, axis=-1, keepdims=True)
    y = x * lax.rsqrt(ms + EPS) * g0_ref[...]
    h = y * (1.0 + sc_ref[...]) + sh_ref[...]
    hb = h.astype(BF16)
    bd = bd_ref[...]

    def head_norm(z, gain):
        zz = (z * z).astype(BF16)
        w = z.shape[1]
        if w == LANE:
            msq = _dot(zz, bd[:LANE, :LANE])
        else:
            msq = jnp.concatenate([_dot(zz[:, k:k + 256], bd) for k in range(0, w, 256)], axis=1)
        return z * lax.rsqrt(msq + EPS) * gain

    za = _dot(hb, w_ref[:, _C_QA:_C_KV])
    qa = head_norm(za[:, _C_QA:_C_KA], gq_ref[:, 0:512])
    ka = head_norm(za[:, _C_KA:_C_VA], gq_ref[:, 512:1024])
    va = za[:, _C_VA:_C_QB]
    qb = head_norm(za[:, _C_QB:_C_KV], gq_ref[:, 1024:1536])
    foxkv_ref[:, 0:512] = ka
    foxkv_ref[:, 512:1024] = va
    qa_ref[...] = (qa * QSCALE).astype(BF16)
    kat_ref[...] = ka.T.astype(BF16)
    va_ref[...] = va.astype(BF16)
    qb_ref[...] = (qb * QSCALE).astype(BF16)

    zb = _dot(hb, w_ref[:, _C_KV:_C_M])
    ksel = head_norm(zb[:, 256:384], gq_ref[:, 1536:1664])
    kwin = head_norm(zb[:, 512:640], gq_ref[:, 1664:1792])
    vwin = zb[:, 640:768]
    nsakv_ref[:, 0:256] = zb[:, 0:256]
    nsakv_ref[:, 256:384] = ksel
    nsakv_ref[:, 384:512] = zb[:, 384:512]
    kvwin_ref[:, 0:128] = kwin
    kvwin_ref[:, 128:256] = vwin
    kwt_ref[...] = kwin.T.astype(BF16)
    vwt_ref[...] = vwin.T.astype(BF16)
    zf = zb[:, 768:896] + bf_ref[...]
    lf = jnp.minimum(zf, 0.0) - jnp.log(1.0 + jnp.exp(-jnp.abs(zf)))
    lft_ref[...] = lf.T[0:8, :]
    gb_ref[...] = jax.nn.sigmoid(zb[:, 896:1024])

    zm = _dot(hb, w_ref[:, _C_M:_C_END])
    gm_ref[...] = jax.nn.sigmoid(zm)


def _inproj(x, shift, scale, g0, w_cat, bd, gq, bf_pad):
    rows = x.shape[0]
    tr = 256
    mrows = shift.shape[0]
    mod_spec = (pl.BlockSpec((1, D_MODEL), lambda i: (0, 0)) if mrows == 1
                else pl.BlockSpec((tr, D_MODEL), lambda i: (i, 0)))
    const = lambda shape: pl.BlockSpec(shape, lambda i: tuple(0 for _ in shape))
    row_spec = lambda w: pl.BlockSpec((tr, w), lambda i: (i, 0))
    col_spec = lambda h: pl.BlockSpec((h, tr), lambda i: (0, i))
    out_shape = (
        jax.ShapeDtypeStruct((rows, 1024), F32),
        jax.ShapeDtypeStruct((rows, 512), F32),
        jax.ShapeDtypeStruct((rows, 256), F32),
        jax.ShapeDtypeStruct((8, rows), F32),
        jax.ShapeDtypeStruct((rows, 512), BF16),
        jax.ShapeDtypeStruct((512, rows), BF16),
        jax.ShapeDtypeStruct((rows, 512), BF16),
        jax.ShapeDtypeStruct((rows, 512), BF16),
        jax.ShapeDtypeStruct((128, rows), BF16),
        jax.ShapeDtypeStruct((128, rows), BF16),
        jax.ShapeDtypeStruct((rows, 128), F32),
        jax.ShapeDtypeStruct((rows, 2048), F32),
    )
    out_specs = (row_spec(1024), row_spec(512), row_spec(256), col_spec(8), row_spec(512),
                 col_spec(512), row_spec(512), row_spec(512), col_spec(128), col_spec(128),
                 row_spec(128), row_spec(2048))
    return pl.pallas_call(
        _inproj_kernel,
        grid=(rows // tr,),
        in_specs=[row_spec(D_MODEL), mod_spec, mod_spec, const((1, D_MODEL)),
                  const((D_MODEL, _C_END)), const((256, 256)), const((1, 1792)), const((1, 128))],
        out_specs=out_specs,
        out_shape=out_shape,
        compiler_params=_cparams(("arbitrary",)),
        name="inproj",
    )(x, shift, scale, g0, w_cat, bd, gq, bf_pad)


def _cumsum_kernel(x_ref, u_ref, pick_ref, bt_ref, o_ref):
    u = u_ref[...]
    c2 = sum(_dot(p, u) for p in _split3(x_ref[...]))
    pick = pick_ref[...]
    totb = sum(_dot(p, pick) for p in _split3(c2))
    bt = bt_ref[...]
    offs = sum(_dot(bt, p) for p in _split3(totb))
    o_ref[...] = (c2 + offs) * LOG2E


def _cumsum(lft):
    heads, s = lft.shape
    nblk = s // LANE
    rows = heads * nblk
    x = lft.reshape(rows, LANE)
    u = jnp.asarray(np.triu(np.ones((LANE, LANE), np.float32)), BF16)
    pick = np.zeros((LANE, LANE), np.float32)
    pick[LANE - 1, :] = 1.0
    r = np.arange(rows)
    bt = ((r[:, None] // nblk == r[None, :] // nblk) & (r[None, :] < r[:, None])).astype(np.float32)
    full = lambda shape: pl.BlockSpec(shape, lambda i: (0, 0))
    out = pl.pallas_call(
        _cumsum_kernel,
        grid=(1,),
        in_specs=[full((rows, LANE)), full((LANE, LANE)), full((LANE, LANE)), full((rows, rows))],
        out_specs=full((rows, LANE)),
        out_shape=jax.ShapeDtypeStruct((rows, LANE), F32),
        compiler_params=_cparams(("arbitrary",)),
        name="cumsum",
    )(x, u, jnp.asarray(pick, BF16), jnp.asarray(bt, BF16))
    return out.reshape(heads, nblk, LANE)


def _online_update(t, v, m_ref, l_ref, acc_ref, v_transposed=False):
    tk = t.shape[1]
    m_prev = m_ref[...]
    m_new = jnp.maximum(m_prev, jnp.max(t, axis=1, keepdims=True))
    alpha = jnp.exp2(m_prev - m_new)
    p = jnp.exp2(t - jnp.tile(m_new, (1, tk // LANE)))
    l_ref[...] = alpha * l_ref[...] + jnp.sum(p, axis=1, keepdims=True)
    dv = acc_ref.shape[1]
    a = alpha if dv == LANE else alpha[:, 0:dv]
    pv = _dot_nt(p.astype(BF16), v) if v_transposed else _dot(p.astype(BF16), v)
    acc_ref[...] = acc_ref[...] * a + pv
    m_ref[...] = m_new


def _fox_kernel(q_ref, kt_ref, v_ref, ck_ref, o_ref, m_ref, l_ref, acc_ref):
    i = pl.program_id(1)
    tq = FOX_TQ
    for hh in range(2):
        q = q_ref[:, 64 * hh:64 * hh + 64]
        m_ref[...] = jnp.full(m_ref.shape, M_INIT, F32)
        l_ref[...] = jnp.zeros(l_ref.shape, F32)
        acc_ref[...] = jnp.zeros(acc_ref.shape, F32)

        def tile(koff, width, masked, hh=hh, q=q):
            kt = kt_ref[64 * hh:64 * hh + 64, pl.ds(koff, width)]
            s = _dot(q, kt)
            blk0 = koff // LANE
            ck = jnp.concatenate([ck_ref[hh, blk0 + c] for c in range(width // LANE)], axis=1)
            t = (s.reshape(tq // 8, 8, width) - ck[None]).reshape(tq, width)
            if masked:
                row = lax.broadcasted_iota(jnp.int32, (tq, width), 0)
                col = lax.broadcasted_iota(jnp.int32, (tq, width), 1)
                t = jnp.where(row >= col, t, NEG)
            _online_update(t, v_ref[pl.ds(koff, width), :], m_ref, l_ref, acc_ref)

        def body(j, carry):
            tile(pl.multiple_of(j * FOX_FAR, FOX_FAR), FOX_FAR, False)
            return carry

        lax.fori_loop(0, i // 2, body, 0)

        @pl.when(i % 2 == 1)
        def _():
            tile(pl.multiple_of((i - 1) * FOX_TK, FOX_TK), FOX_TK, False)

        tile(pl.multiple_of(i * FOX_TK, FOX_TK), FOX_TK, True)
        o = acc_ref[...] / l_ref[...]
        o_ref[:, 64 * hh:64 * hh + 64] = o[:, 64 * hh:64 * hh + 64]


def _fox_prompt(qa, kat, va, ck_rep):
    s = qa.shape[0]
    nblk = s // LANE
    return pl.pallas_call(
        _fox_kernel,
        grid=(FOX_HEADS // 2, s // FOX_TQ),
        in_specs=[pl.BlockSpec((FOX_TQ, LANE), lambda hp, i: (i, hp)),
                  pl.BlockSpec((LANE, s), lambda hp, i: (hp, 0)),
                  pl.BlockSpec((s, LANE), lambda hp, i: (0, hp)),
                  pl.BlockSpec((2, nblk, 8, LANE), lambda hp, i: (hp, 0, 0, 0))],
        out_specs=pl.BlockSpec((FOX_TQ, LANE), lambda hp, i: (i, hp)),
        out_shape=jax.ShapeDtypeStruct((s, 512), F32),
        scratch_shapes=[pltpu.VMEM((FOX_TQ, LANE), F32), pltpu.VMEM((FOX_TQ, LANE), F32),
                        pltpu.VMEM((FOX_TQ, LANE), F32)],
        compiler_params=_cparams(("arbitrary", "arbitrary")),
        name="fox",
    )(qa, kat, va, ck_rep)


def _gelu_tanh(x):
    return 0.5 * x * (1.0 + jnp.tanh(math.sqrt(2.0 / math.pi) * (x + 0.044715 * (x * x * x))))


def _nsaprep_kernel(*refs, n_parts, n_prefetch, feature_major):
    refs = refs[n_prefetch:]
    part_refs = refs[:n_parts]
    (look_ref, pe_ref, w1_ref, w2_ref, bd_ref, gkc_ref,
     kct_ref, vct_ref, kst_ref, vst_ref, xk_ref, xv_ref) = refs[n_parts:]
    xs_refs = (xk_ref, xv_ref)
    prow = PREP_ROWS // n_parts
    for k in range(n_parts):
        lo, hi = k * prow, (k + 1) * prow
        if feature_major:
            blk = part_refs[k][...].reshape(4, LANE, prow)
            xk_ref[lo:hi, :] = blk[0].T
            xv_ref[lo:hi, :] = blk[1].T
            kst_ref[0, :, lo:hi] = blk[2].astype(BF16)
            vst_ref[0, :, lo:hi] = blk[3].astype(BF16)
        else:
            blk = part_refs[k][...]
            xk_ref[lo:hi, :] = blk[:, 0:128]
            xv_ref[lo:hi, :] = blk[:, 128:256]
            kst_ref[0, :, lo:hi] = blk[:, 256:384].T.astype(BF16)
            vst_ref[0, :, lo:hi] = blk[:, 384:512].T.astype(BF16)
    if feature_major:
        look = look_ref[...].reshape(4, LANE, PAGE)
        xk_ref[PREP_ROWS:PREP_ROWS + CMP_STRIDE, :] = look[0].T[0:CMP_STRIDE, :]
        xv_ref[PREP_ROWS:PREP_ROWS + CMP_STRIDE, :] = look[1].T[0:CMP_STRIDE, :]
    else:
        look = look_ref[...]
        xk_ref[PREP_ROWS:PREP_ROWS + CMP_STRIDE, :] = look[:, 0:128]
        xv_ref[PREP_ROWS:PREP_ROWS + CMP_STRIDE, :] = look[:, 128:256]

    ntok = PREP_ROWS // CMP_STRIDE
    outs = []
    for kv in range(2):
        hid = jnp.zeros((ntok, NSA_G * CMP_HIDDEN), F32)
        for half in range(2):
            xcat = jnp.concatenate(
                [xs_refs[kv][pl.ds(half * CMP_STRIDE + r, ntok, stride=CMP_STRIDE), :]
                 for r in range(CMP_STRIDE)], axis=1)
            xcat = (xcat + pe_ref[kv, half]).astype(BF16)
            hid = hid + _dot(xcat, w1_ref[kv, half])
        act = _gelu_tanh(hid).astype(BF16)
        outs.append(_dot(act, w2_ref[kv]))
    kc, vc = outs
    msq = _dot((kc * kc).astype(BF16), bd_ref[...])
    kc = kc * lax.rsqrt(msq + EPS) * gkc_ref[...]
    kct_ref[0] = kc.T.astype(BF16)
    vct_ref[0] = vc.T.astype(BF16)


def _nsaprep(parts_arrays, part_specs, look_array, look_spec, grid, batch, length, prefetch,
             pe, w1, w2, bd128, gkc, feature_major):
    n_parts = len(part_specs)
    ntok = PREP_ROWS // CMP_STRIDE
    ncp = length // CMP_STRIDE
    npf = len(prefetch)

    def cm(shape):
        return pl.BlockSpec(shape, lambda b, t, *pf: tuple(0 for _ in shape))

    in_specs = list(part_specs) + [look_spec, cm(pe.shape), cm(w1.shape), cm(w2.shape),
                                   cm(bd128.shape), cm(gkc.shape)]
    out_specs = (
        pl.BlockSpec((1, 128, ntok), lambda b, t, *pf: (b, 0, t)),
        pl.BlockSpec((1, 128, ntok), lambda b, t, *pf: (b, 0, t)),
        pl.BlockSpec((1, 128, PREP_ROWS), lambda b, t, *pf: (b, 0, t)),
        pl.BlockSpec((1, 128, PREP_ROWS), lambda b, t, *pf: (b, 0, t)),
    )
    out_shape = (
        jax.ShapeDtypeStruct((batch, 128, ncp), BF16),
        jax.ShapeDtypeStruct((batch, 128, ncp), BF16),
        jax.ShapeDtypeStruct((batch, 128, length), BF16),
        jax.ShapeDtypeStruct((batch, 128, length), BF16),
    )
    grid_spec = pltpu.PrefetchScalarGridSpec(
        num_scalar_prefetch=npf, grid=grid, in_specs=in_specs, out_specs=out_specs,
        scratch_shapes=[pltpu.VMEM((PREP_ROWS + CMP_STRIDE, LANE), F32),
                        pltpu.VMEM((PREP_ROWS + CMP_STRIDE, LANE), F32)])
    return pl.pallas_call(
        functools.partial(_nsaprep_kernel, n_parts=n_parts, n_prefetch=npf, feature_major=feature_major),
        grid_spec=grid_spec,
        out_shape=out_shape,
        compiler_params=_cparams(("arbitrary", "arbitrary")),
        name="nsaprep",
    )(*prefetch, *parts_arrays, look_array, pe, w1, w2, bd128, gkc)


def _nsa_kernel(*refs, tq, tile_base, single_shot, ncp, nbp):
    if single_shot:
        (q_ref, kct_ref, vct_ref, kst_ref, vst_ref, kstt_ref, vstt_ref, kwt_ref, vwt_ref, g_ref,
         tw_ref, ts_ref, tc_ref, emain_ref, etail_ref, wcs_ref, eg_ref, o_ref) = refs
    else:
        (q_ref, kct_ref, vct_ref, kst_ref, vst_ref, kwt_ref, vwt_ref, g_ref,
         tw_ref, ts_ref, tc_ref, etab_ref, wcs_ref, eg_ref,
         o_ref, sel_ref, m_ref, l_ref, acc_ref) = refs
    rows = NSA_HPG * tq
    i_abs = tile_base + pl.program_id(2)
    q0 = i_abs * NSA_TQ
    q4 = q_ref[0]
    qhs = [q4[:, 64 * hh:64 * hh + 64] for hh in range(NSA_HPG)]
    qs = jnp.concatenate(qhs, axis=0)

    n_iota = lax.broadcasted_iota(jnp.int32, (1, ncp), 1)
    base = jnp.where(n_iota >= 16 * i_abs + 16, NEG, 0.0)
    pr = lax.broadcasted_iota(jnp.int32, (2 * CMP_NEAR, ncp), 0) % CMP_NEAR
    pn = lax.broadcasted_iota(jnp.int32, (2 * CMP_NEAR, ncp), 1)
    place = jnp.where(pn == 16 * i_abs - 16 + pr, 1.0, 0.0).astype(BF16)
    kct = kct_ref[0]
    vct = vct_ref[0]
    psum = jnp.zeros((tq, ncp), F32)
    o_cmp = []
    for hh in range(NSA_HPG):
        s = _dot(qhs[hh], kct) + _dot(tc_ref[0, hh * tq:(hh + 1) * tq, :], place) + base
        m = jnp.max(s, axis=1, keepdims=True)
        p = jnp.exp2(s - m)
        l = jnp.sum(p, axis=1, keepdims=True)
        inv = jnp.where(m > 0.5 * NEG, 1.0 / l, 0.0)
        pn_ = p * inv
        psum = psum + pn_
        o_cmp.append(_dot_nt(pn_.astype(BF16), vct))
    wcs = wcs_ref[...]
    imp = sum(_dot(part, wcs) for part in _split2(psum))

    posq = q0 + lax.broadcasted_iota(jnp.int32, (tq, 1), 0)
    blk = lax.broadcasted_iota(jnp.int32, (tq, nbp), 1)
    blk_f = blk.astype(F32)
    cur = posq // SEL_BLOCK
    forced = (blk == 0) | (blk == cur) | (blk == cur - 1)
    sel = jnp.where(forced, 0.0, NEG)
    score = jnp.where(forced | (blk * SEL_BLOCK > posq), -jnp.inf, imp)
    for _ in range(N_SEL - N_FORCED):
        mx = jnp.max(score, axis=1, keepdims=True)
        cand = jnp.where(score == mx, blk_f, float(4 * nbp))
        amin = jnp.min(cand, axis=1, keepdims=True)
        hit = blk_f == amin
        sel = jnp.where(hit & (mx > -jnp.inf), 0.0, sel)
        score = jnp.where(hit, -jnp.inf, score)

    if single_shot:
        skv = kst_ref.shape[2]
        slab = min(SEL_SLAB, skv)
        selb = sel.astype(BF16)
        emain = emain_ref[...]
        mb = jnp.concatenate([_dot(selb[:, LANE * c:LANE * (c + 1)], emain) for c in range(skv // slab)], axis=1)
        ct = (skv // SEL_BLOCK) // LANE
        mb_tail = _dot(selb[:, LANE * ct:LANE * (ct + 1)], etail_ref[...])
        near = ts_ref[0]
        s_main = _dot(qs, kst_ref[0])
        t_main = (s_main.reshape(NSA_HPG, tq, skv) + mb[None]).reshape(rows, skv)
        t_main = jnp.concatenate([t_main[:, :skv - NSA_TK], t_main[:, skv - NSA_TK:] + near[:, 0:NSA_TK]], axis=1)
        s_tail = _dot(qs, kstt_ref[0])
        t_tail = (s_tail.reshape(NSA_HPG, tq, NSA_TK) + mb_tail[None]).reshape(rows, NSA_TK) + near[:, NSA_TK:SEL_NEAR]
        m = jnp.maximum(jnp.max(t_main, axis=1, keepdims=True), jnp.max(t_tail, axis=1, keepdims=True))
        p_main = jnp.exp2(t_main - m)
        p_tail = jnp.exp2(t_tail - m)
        l = jnp.sum(p_main, axis=1, keepdims=True) + jnp.sum(p_tail, axis=1, keepdims=True)
        o_sel = (_dot_nt(p_main.astype(BF16), vst_ref[0]) + _dot_nt(p_tail.astype(BF16), vstt_ref[0])) / l
    else:
        sel_ref[...] = sel
        m_ref[...] = jnp.full(m_ref.shape, M_INIT, F32)
        l_ref[...] = jnp.zeros(l_ref.shape, F32)
        acc_ref[...] = jnp.zeros(acc_ref.shape, F32)

        def sel_tile(koff, width, near_tbl):
            kt = kst_ref[0, :, pl.ds(koff, width)]
            vt = vst_ref[0, :, pl.ds(koff, width)]
            s = _dot(qs, kt)
            mbs = []
            for u in range(width // NSA_TK):
                j = koff // NSA_TK + u
                slab_off = pl.multiple_of((j // 32) * LANE, LANE)
                sel128 = sel_ref[:, pl.ds(slab_off, LANE)]
                mbs.append(_dot(sel128.astype(BF16), etab_ref[j % 32]))
            mb = mbs[0] if len(mbs) == 1 else jnp.concatenate(mbs, axis=1)
            t = (s.reshape(NSA_HPG, tq, width) + mb[None]).reshape(rows, width)
            if near_tbl is not None:
                t = t + near_tbl
            _online_update(t, vt, m_ref, l_ref, acc_ref, v_transposed=True)

        n_far_tiles = jnp.maximum(i_abs - 1, 0)
        n_big = n_far_tiles // (NSA_FAR // NSA_TK)
        rem = n_far_tiles % (NSA_FAR // NSA_TK)

        def far_body(j, carry):
            sel_tile(pl.multiple_of(j * NSA_FAR, NSA_FAR), NSA_FAR, None)
            return carry

        lax.fori_loop(0, n_big, far_body, 0)
        rem_off = n_big * NSA_FAR

        @pl.when(rem >= 2)
        def _():
            sel_tile(pl.multiple_of(rem_off, 2 * NSA_TK), 2 * NSA_TK, None)

        @pl.when(rem % 2 == 1)
        def _():
            sel_tile(pl.multiple_of(rem_off + (rem // 2) * 2 * NSA_TK, NSA_TK), NSA_TK, None)

        @pl.when(i_abs >= 1)
        def _():
            sel_tile(pl.multiple_of((i_abs - 1) * NSA_TK, NSA_TK), NSA_TK, ts_ref[0, :, 0:NSA_TK])

        sel_tile(pl.multiple_of(i_abs * NSA_TK, NSA_TK), NSA_TK, ts_ref[0, :, NSA_TK:SEL_NEAR])
        o_sel = acc_ref[...] / l_ref[:, 0:64]

    band0 = pl.multiple_of((i_abs - tile_base if single_shot else i_abs) * NSA_TQ, NSA_TQ)
    ktw = kwt_ref[0, :, pl.ds(band0, WIN_BAND)]
    s = _dot(qs, ktw) + tw_ref[0]
    b_iota = lax.broadcasted_iota(jnp.int32, (1, WIN_BAND), 1)
    s = jnp.where(b_iota >= WINDOW - q0, s, NEG)
    m = jnp.max(s, axis=1, keepdims=True)
    p = jnp.exp2(s - m)
    l = jnp.sum(p, axis=1, keepdims=True)
    o_win = _dot_nt(p.astype(BF16), vwt_ref[0, :, pl.ds(band0, WIN_BAND)]) / l

    def heads_to_lanes(o):
        return jnp.concatenate([o[hh * tq:(hh + 1) * tq] for hh in range(NSA_HPG)], axis=1)

    branches = (jnp.concatenate(o_cmp, axis=1), heads_to_lanes(o_sel), heads_to_lanes(o_win))
    gparts = _split2(g_ref[0])
    out = jnp.zeros((tq, NSA_HPG * HEAD_DIM), F32)
    for br in range(3):
        gexp = sum(_dot(part, eg_ref[0, br]) for part in gparts)
        out = out + gexp * branches[br]
    o_ref[0] = out


def _nsa_attention(q, kct, vct, kst, vst, tail, kwt, vwt, gates, tables, tq, tile_base):
    batch, sq, _ = q.shape
    n_qt = sq // tq
    ncp = kct.shape[2]
    skv = kst.shape[2]
    lw = kwt.shape[2]
    rows = NSA_HPG * tq
    single_shot = tail is not None
    nbp = tables["wcs"].shape[1]
    grp = lambda n: pl.BlockSpec((1, 64, n), lambda b, g, i: (b, g, 0))
    per_g = lambda w: pl.BlockSpec((1, rows, w), lambda b, g, i: (g, 0, 0))
    const = lambda a: pl.BlockSpec(a.shape, lambda b, g, i: tuple(0 for _ in a.shape))
    in_specs = [pl.BlockSpec((1, tq, 256), lambda b, g, i: (b, i, g)), grp(ncp), grp(ncp), grp(skv), grp(skv)]
    args = [q, kct, vct, kst, vst]
    if single_shot:
        in_specs += [grp(NSA_TK), grp(NSA_TK)]
        args += list(tail)
    in_specs += [grp(lw), grp(lw), pl.BlockSpec((1, tq, 128), lambda b, g, i: (b, i, 0)),
                 per_g(WIN_BAND), per_g(SEL_NEAR), per_g(2 * CMP_NEAR)]
    args += [kwt, vwt, gates, tables["tw"], tables["ts"], tables["tc"]]
    if single_shot:
        in_specs += [const(tables["emain"]), const(tables["etail"])]
        args += [tables["emain"], tables["etail"]]
        scratch = []
    else:
        in_specs += [const(tables["etab"])]
        args += [tables["etab"]]
        scratch = [pltpu.VMEM((tq, nbp), F32), pltpu.VMEM((rows, LANE), F32),
                   pltpu.VMEM((rows, LANE), F32), pltpu.VMEM((rows, HEAD_DIM), F32)]
    in_specs += [const(tables["wcs"]), pl.BlockSpec((1, 3, 128, 256), lambda b, g, i: (g, 0, 0, 0))]
    args += [tables["wcs"], tables["eg"]]
    return pl.pallas_call(
        functools.partial(_nsa_kernel, tq=tq, tile_base=tile_base, single_shot=single_shot, ncp=ncp, nbp=nbp),
        grid=(batch, NSA_G, n_qt),
        in_specs=in_specs,
        out_specs=pl.BlockSpec((1, tq, 256), lambda b, g, i: (b, i, g)),
        out_shape=jax.ShapeDtypeStruct((batch, sq, 512), F32),
        scratch_shapes=scratch,
        compiler_params=_cparams(("arbitrary", "arbitrary", "arbitrary")),
        name="nsa",
    )(*args)


def _t5_bucket(dist):
    d = jnp.maximum(dist, 0)
    exact = REL_BUCKETS // 2
    far = exact + (jnp.log(jnp.maximum(d, 1).astype(F32) / exact)
                   / math.log(REL_MAX_DIST / exact) * (REL_BUCKETS - exact)).astype(jnp.int32)
    return jnp.where(d < exact, d, jnp.minimum(far, REL_BUCKETS - 1))


def _nsa_tables(rel_bias, tq, ncp, nbp, skv=None):
    dmax = WINDOW + NSA_TQ + 1
    by_dist = (rel_bias[_t5_bucket(jnp.arange(dmax))] - rel_bias[REL_BUCKETS - 1][None, :]) * LOG2E

    def toeplitz(w, off, lo, hi):
        n = w + tq
        k = np.arange(n)
        k = np.where(k < w, k, k - n)
        d = off - k
        g = jnp.where(jnp.asarray((d >= lo) & (d <= hi))[:, None], by_dist[np.clip(d, 0, dmax - 1)], NEG)
        flat = jnp.tile(g.T, (1, tq))[:, :tq * (n - 1)]
        t = flat.reshape(NSA_HEADS, tq, n - 1)[:, :, :w]
        return t.reshape(NSA_G, NSA_HPG * tq, w)

    tw = toeplitz(WIN_BAND, WINDOW, 0, WINDOW)
    ts = toeplitz(SEL_NEAR, NSA_TK, 0, dmax)
    tc = toeplitz(CMP_STRIDE * CMP_NEAR, NSA_TQ - CMP_BLOCK + 1, 0, dmax)[:, :, ::CMP_STRIDE]
    tc_hi = tc.astype(BF16)
    tc_lo = (tc - tc_hi.astype(F32)).astype(BF16)
    tables = {"tw": tw, "ts": ts, "tc": jnp.concatenate([tc_hi, tc_lo], axis=2)}

    m = np.arange(LANE)[:, None]
    if skv is None:
        key = np.arange(NSA_TK)[None, None, :]
        r = np.arange(32)[:, None, None]
        tables["etab"] = jnp.asarray((m[None] == 4 * r + key // SEL_BLOCK).astype(np.float32), BF16)
    else:
        slab = min(SEL_SLAB, skv)
        tables["emain"] = jnp.asarray((m == np.arange(slab)[None, :] // SEL_BLOCK).astype(np.float32), BF16)
        o = (skv // SEL_BLOCK) % LANE
        tables["etail"] = jnp.asarray((m == o + np.arange(NSA_TK)[None, :] // SEL_BLOCK).astype(np.float32), BF16)

    c0 = np.arange(ncp)[:, None] * CMP_STRIDE
    s0 = np.arange(nbp)[None, :] * SEL_BLOCK
    shared = np.minimum(c0 + CMP_BLOCK, s0 + SEL_BLOCK) - np.maximum(c0, s0)
    tables["wcs"] = jnp.asarray(np.maximum(shared, 0).astype(np.float32) / CMP_BLOCK, BF16)

    eg = np.zeros((NSA_G, 3, 128, 256), np.float32)
    for g in range(NSA_G):
        for br in range(3):
            for hh in range(NSA_HPG):
                eg[g, br, 3 * (NSA_HPG * g + hh) + br, 64 * hh:64 * hh + 64] = 1.0
    tables["eg"] = jnp.asarray(eg, BF16)
    return tables


def _foxdec_kernel(pt_ref, qbd_ref, knew_ref, vnew_ref, lfnew_ref, u_ref, *refs):
    npg = FOX_DEC_PAGES
    kv_refs = refs[:npg]
    lf_refs = refs[npg:2 * npg]
    o_ref, m_ref, l_ref, acc_ref, carry_ref = refs[2 * npg:]
    c = pl.program_id(1)
    nrow = FOX_HEADS * 8

    @pl.when(c == 0)
    def _():
        m_ref[...] = jnp.full(m_ref.shape, M_INIT, F32)
        l_ref[...] = jnp.zeros(l_ref.shape, F32)
        acc_ref[...] = jnp.zeros(acc_ref.shape, F32)
        carry_ref[...] = jnp.zeros(carry_ref.shape, F32)

    qbd = qbd_ref[0]
    u = u_ref[...]

    def cum_blocks(lf_blocks):
        lf = lf_blocks[0] if len(lf_blocks) == 1 else jnp.concatenate(lf_blocks, axis=0)
        res = sum(_dot(part, u) for part in _split3(lf))
        run = carry_ref[...]
        out = []
        for k in range(len(lf_blocks)):
            out.append((res[8 * k:8 * k + 8, 0:LANE] + run) * LOG2E)
            run = run + res[8 * k:8 * k + 8, LANE:2 * LANE]
        carry_ref[...] = run
        return out

    def biased(s, ck):
        return (s.reshape(FOX_HEADS, 8, s.shape[1]) - ck[:, None, :]).reshape(nrow, s.shape[1])

    kt = jnp.concatenate([kv_refs[k][0, 0].reshape(512, PAGE) for k in range(npg)], axis=1).astype(BF16)
    vt = jnp.concatenate([kv_refs[k][0, 1].reshape(512, PAGE) for k in range(npg)], axis=1).astype(BF16)
    ck = jnp.concatenate(cum_blocks([lf_refs[k][0] for k in range(npg)]), axis=1)
    t = biased(_dot(qbd, kt), ck)
    tk = t.shape[1]
    m_prev = m_ref[...]
    m_new = jnp.maximum(m_prev, jnp.max(t, axis=1, keepdims=True))
    alpha = jnp.exp2(m_prev - m_new)
    p = jnp.exp2(t - jnp.tile(m_new, (1, tk // LANE)))
    l_ref[...] = alpha * l_ref[...] + jnp.sum(p, axis=1, keepdims=True)
    acc_ref[...] = acc_ref[...] * jnp.tile(alpha, (1, 4)) + _dot_nt(p.astype(BF16), vt)
    m_ref[...] = m_new

    @pl.when(c == pl.num_programs(1) - 1)
    def _():
        t2 = biased(_dot(qbd, knew_ref[0]), cum_blocks([lfnew_ref[0]])[0])
        row = lax.broadcasted_iota(jnp.int32, (nrow, LANE), 0) % 8
        col = lax.broadcasted_iota(jnp.int32, (nrow, LANE), 1)
        t2 = jnp.where(col <= row, t2, NEG)
        m_prev2 = m_ref[...]
        m_new2 = jnp.maximum(m_prev2, jnp.max(t2, axis=1, keepdims=True))
        alpha2 = jnp.exp2(m_prev2 - m_new2)
        p2 = jnp.exp2(t2 - m_new2)
        l2 = alpha2 * l_ref[...] + jnp.sum(p2, axis=1, keepdims=True)
        acc2 = acc_ref[...] * jnp.tile(alpha2, (1, 4)) + _dot_nt(p2.astype(BF16), vnew_ref[0])
        o = acc2 / jnp.tile(l2, (1, 4))
        for h in range(FOX_HEADS):
            o_ref[0, :, 64 * h:64 * h + 64] = o[8 * h:8 * h + 8, 64 * h:64 * h + 64]


def _fox_decode(page_table, qbd, knew_t, vnew_t, lfnew, kv_pool, lft_pool):
    batch, n_pages = page_table.shape
    npg = FOX_DEC_PAGES
    u = jnp.asarray(np.concatenate([np.triu(np.ones((LANE, LANE), np.float32)),
                                    np.ones((LANE, LANE), np.float32)], axis=1), BF16)

    def seq(shape):
        return pl.BlockSpec(shape, lambda b, c, pt: (b,) + tuple(0 for _ in shape[1:]))

    def page(shape, k):
        return pl.BlockSpec(shape, lambda b, c, pt, k=k: (pt[b, c * npg + k],) + tuple(0 for _ in shape[1:]))

    in_specs = ([seq((1, 64, 512)), seq((1, 512, LANE)), seq((1, 512, LANE)), seq((1, 8, LANE)),
                 pl.BlockSpec((LANE, 2 * LANE), lambda b, c, pt: (0, 0))]
                + [page((1, 2, FOX_HEADS, HEAD_DIM, PAGE), k) for k in range(npg)]
                + [page((1, 8, PAGE), k) for k in range(npg)])
    grid_spec = pltpu.PrefetchScalarGridSpec(
        num_scalar_prefetch=1, grid=(batch, n_pages // npg), in_specs=in_specs,
        out_specs=seq((1, 8, 512)),
        scratch_shapes=[pltpu.VMEM((64, LANE), F32), pltpu.VMEM((64, LANE), F32),
                        pltpu.VMEM((64, 512), F32), pltpu.VMEM((8, LANE), F32)])
    return pl.pallas_call(
        _foxdec_kernel,
        grid_spec=grid_spec,
        out_shape=jax.ShapeDtypeStruct((batch, 8, 512), F32),
        compiler_params=_cparams(("arbitrary", "arbitrary")),
        name="foxdec",
    )(page_table, qbd, knew_t, vnew_t, lfnew, u, *([kv_pool] * npg), *([lft_pool] * npg))


def _outmlp_kernel(x_ref, oa_ref, ob_ref, gm_ref, m2_ref, m3_ref, m4_ref, m5_ref, g1_ref,
                   wf_ref, wn_ref, wo_ref, wu_ref, wd_ref, y_ref, y1_ref, h2_ref, acc_ref):
    c = pl.program_id(1)

    @pl.when(c == 0)
    def _():
        ya = _dot(oa_ref[...].astype(BF16), wf_ref[...])
        yb = _dot(ob_ref[...].astype(BF16), wn_ref[...])
        u = gm_ref[:, 0:D_MODEL] * ya + gm_ref[:, D_MODEL:2 * D_MODEL] * yb
        att = _dot(u.astype(BF16), wo_ref[...])
        y1 = x_ref[...] + m2_ref[...] * att
        y1_ref[...] = y1
        ms = jnp.mean(y1 * y1, axis=-1, keepdims=True)
        hn = y1 * lax.rsqrt(ms + EPS) * g1_ref[...]
        h2_ref[...] = (hn * (1.0 + m4_ref[...]) + m3_ref[...]).astype(BF16)
        acc_ref[...] = jnp.zeros(acc_ref.shape, F32)

    up = _dot(h2_ref[...], wu_ref[...])
    act = jnp.square(jnp.maximum(up, 0.0)).astype(BF16)
    acc_ref[...] += _dot(act, wd_ref[...])

    @pl.when(c == pl.num_programs(1) - 1)
    def _():
        y_ref[...] = y1_ref[...] + m5_ref[...] * acc_ref[...]


def _out_mlp(x, oa, ob, gm, mods, g1, wf, wn, wo, wu, wd, tr):
    rows = x.shape[0]
    ffc = 1024
    mrows = mods[0].shape[0]
    mod_spec = (pl.BlockSpec((1, D_MODEL), lambda i, c: (0, 0)) if mrows == 1
                else pl.BlockSpec((tr, D_MODEL), lambda i, c: (i, 0)))
    row = lambda w: pl.BlockSpec((tr, w), lambda i, c: (i, 0))
    const = lambda shape: pl.BlockSpec(shape, lambda i, c: (0, 0))
    return pl.pallas_call(
        _outmlp_kernel,
        grid=(rows // tr, D_FF // ffc),
        in_specs=[row(D_MODEL), row(512), row(512), row(2048), mod_spec, mod_spec, mod_spec, mod_spec,
                  const((1, D_MODEL)), const((512, D_MODEL)), const((512, D_MODEL)),
                  const((D_MODEL, D_MODEL)),
                  pl.BlockSpec((D_MODEL, ffc), lambda i, c: (0, c)),
                  pl.BlockSpec((ffc, D_MODEL), lambda i, c: (c, 0))],
        out_specs=row(D_MODEL),
        out_shape=jax.ShapeDtypeStruct((rows, D_MODEL), F32),
        scratch_shapes=[pltpu.VMEM((tr, D_MODEL), F32), pltpu.VMEM((tr, D_MODEL), BF16),
                        pltpu.VMEM((tr, D_MODEL), F32)],
        compiler_params=_cparams(("arbitrary", "arbitrary")),
        name="outmlp",
    )(x, oa, ob, gm, *mods, g1, wf, wn, wo, wu, wd)


def _prep_w_in(w_in):
    qa, ka, va, zf, qb, zkv, zg, zm = jnp.split(
        w_in, np.cumsum([512, 512, 512, 8, 512, 768, 24, 2048])[:-1].tolist(), axis=1)
    pad = lambda w: jnp.pad(w, ((0, 0), (0, 128 - w.shape[1])))
    return jnp.concatenate([qa, ka, va, qb, zkv, pad(zf), pad(zg), zm], axis=1).astype(BF16)


def _prep_cmp_weights(pe_cmp, w_cmp1, w_cmp2):
    eye = jnp.eye(NSA_G, dtype=F32)
    w1 = w_cmp1.reshape(2, 2, CMP_STRIDE, HEAD_DIM, CMP_HIDDEN)
    w1big = jnp.einsum('khrdj,gq->khrgdqj', w1, eye).reshape(
        2, 2, CMP_STRIDE * NSA_G * HEAD_DIM, NSA_G * CMP_HIDDEN)
    w2big = jnp.einsum('kjd,gq->kgjqd', w_cmp2, eye).reshape(2, NSA_G * CMP_HIDDEN, NSA_G * HEAD_DIM)
    pe = pe_cmp.reshape(2, 2, CMP_STRIDE, 1, HEAD_DIM)
    pe = jnp.broadcast_to(pe, (2, 2, CMP_STRIDE, NSA_G, HEAD_DIM)).reshape(2, 2, 1, CMP_STRIDE * NSA_G * HEAD_DIM)
    return pe, w1big.astype(BF16), w2big.astype(BF16)


def _block_diag_mean(n):
    r = np.arange(n)
    return jnp.asarray((r[:, None] // HEAD_DIM == r[None, :] // HEAD_DIM).astype(np.float32) / HEAD_DIM, BF16)


def kernel(x_prompt, x_sample, cache_fox_kv, cache_fox_logf, cache_nsa_kv, state_nsa_win, page_table,
           c_prompt, c_sample, w_ada, b_ada, g_norm, w_in, b_forget, g_qk_fox, g_qk_nsa,
           pe_cmp, w_cmp1, w_cmp2, rel_bias, w_out_fox, w_out_nsa, w_out, w_up, w_down):
    n_batch, seq, _ = x_prompt.shape
    dec_batch, dec_seq, _ = x_sample.shape
    n_pages = page_table.shape[1]
    past = n_pages * PAGE
    assert n_batch == 1 and dec_seq == 8 and w_ada.shape[0] == 1
    assert seq % PREP_ROWS == 0 and past % PREP_ROWS == 0 and state_nsa_win.shape[2] == WINDOW

    n_c = 1 + dec_batch
    c_rows = -(-n_c // 8) * 8
    c_all = jnp.pad(jnp.concatenate([c_prompt, c_sample], axis=0), ((0, c_rows - n_c), (0, 0)))
    mods = _ada(c_all, w_ada[0], b_ada[0])
    mod_p = [mods[0:1, k * D_MODEL:(k + 1) * D_MODEL] for k in range(6)]
    mod_s = [jnp.repeat(mods[1:n_c, k * D_MODEL:(k + 1) * D_MODEL], dec_seq, axis=0) for k in range(6)]

    w_cat = _prep_w_in(w_in[0])
    bd256 = _block_diag_mean(256)
    tile8 = lambda g: jnp.tile(g, 8)
    gq = jnp.concatenate([tile8(g_qk_fox[0, 0]), tile8(g_qk_fox[0, 1]), tile8(g_qk_nsa[0, 0]),
                          jnp.tile(g_qk_nsa[0, 2], 2), jnp.tile(g_qk_nsa[0, 3], 2)]).reshape(1, 1792)
    bf_pad = jnp.pad(b_forget[0], (0, 120)).reshape(1, 128)
    g0 = g_norm[0, 0].reshape(1, D_MODEL)
    xp = x_prompt.reshape(seq, D_MODEL)
    xs = x_sample.reshape(dec_batch * dec_seq, D_MODEL)
    (p_foxkv, p_nsakv, p_kvwin, p_lft, p_qa, p_kat, p_va, p_qb, p_kwt, p_vwt, p_gb, p_gm) = _inproj(
        xp, mod_p[0], mod_p[1], g0, w_cat, bd256, gq, bf_pad)
    (s_foxkv, s_nsakv, s_kvwin, s_lft, s_qa, s_kat, s_va, s_qb, s_kwt, s_vwt, s_gb, s_gm) = _inproj(
        xs, mod_s[0], mod_s[1], g0, w_cat, bd256, gq, bf_pad)

    ck = _cumsum(p_lft)
    ck_rep = jnp.broadcast_to(ck[:, :, None, :], (FOX_HEADS, seq // LANE, 8, LANE))
    oa_p = _fox_prompt(p_qa, p_kat, p_va, ck_rep)

    pe, w1big, w2big = _prep_cmp_weights(pe_cmp[0], w_cmp1[0], w_cmp2[0])
    bd128 = _block_diag_mean(128)
    gkc = jnp.tile(g_qk_nsa[0, 1], 2).reshape(1, 128)
    look_blocks = PREP_ROWS // CMP_STRIDE
    last_look_p = seq // CMP_STRIDE - 1
    kct_p, vct_p, kst_p, vst_p = _nsaprep(
        [p_nsakv], [pl.BlockSpec((PREP_ROWS, 512), lambda b, t: (t, 0))],
        p_nsakv, pl.BlockSpec((CMP_STRIDE, 512), lambda b, t: (jnp.minimum((t + 1) * look_blocks, last_look_p), 0)),
        (1, seq // PREP_ROWS), 1, seq, (), pe, w1big, w2big, bd128, gkc, False)
    tables_p = _nsa_tables(rel_bias, NSA_TQ, seq // CMP_STRIDE, max(seq // SEL_BLOCK, LANE))
    kwt_pad = jnp.pad(p_kwt, ((0, 0), (WINDOW, 0)))[None]
    vwt_pad = jnp.pad(p_vwt, ((0, 0), (WINDOW, 0)))[None]
    ob_p = _nsa_attention(p_qb[None], kct_p, vct_p, kst_p, vst_p, None, kwt_pad, vwt_pad, p_gb[None],
                          tables_p, NSA_TQ, 0)[0]

    kv_pool = jnp.transpose(cache_fox_kv[0], (0, 2, 3, 4, 1))
    lft_pool = jnp.transpose(cache_fox_logf[0], (0, 2, 1))
    eye_h = jnp.eye(FOX_HEADS, dtype=BF16)
    qa_s = s_qa.reshape(dec_batch, dec_seq, FOX_HEADS, HEAD_DIM)
    qbd = jnp.einsum('bjhd,hk->bhjkd', qa_s, eye_h).reshape(dec_batch, 64, 512)

    def new_cols(a_t):
        a = jnp.transpose(a_t.reshape(a_t.shape[0], dec_batch, dec_seq), (1, 0, 2))
        return jnp.pad(a, ((0, 0), (0, 0), (0, LANE - dec_seq)))

    knew_t = new_cols(s_kat)
    vnew_t = new_cols(jnp.transpose(s_va))
    lfnew = new_cols(s_lft)
    oa_s = _fox_decode(page_table, qbd, knew_t, vnew_t, lfnew, kv_pool, lft_pool)

    nsa_pool = jnp.transpose(cache_nsa_kv[0], (0, 2, 3, 4, 1))
    npg = NSA_DEC_PAGES
    page_shape = (1, 4, NSA_G, HEAD_DIM, PAGE)
    part_specs = [pl.BlockSpec(page_shape, lambda b, t, pt, k=k: (pt[b, t * npg + k], 0, 0, 0, 0))
                  for k in range(npg)]
    look_spec = pl.BlockSpec(
        page_shape, lambda b, t, pt: (pt[b, jnp.minimum((t + 1) * npg, n_pages - 1)], 0, 0, 0, 0))
    kct_s, vct_s, kst_s, vst_s = _nsaprep(
        [nsa_pool] * npg, part_specs, nsa_pool, look_spec,
        (dec_batch, past // PREP_ROWS), dec_batch, past, (page_table,), pe, w1big, w2big, bd128, gkc, True)
    tile_base = past // NSA_TQ
    nb_dec = past // SEL_BLOCK + NSA_TK // SEL_BLOCK
    tables_s = _nsa_tables(rel_bias, dec_seq, past // CMP_STRIDE, -(-nb_dec // LANE) * LANE, skv=past)
    new_nsa_t = jnp.transpose(s_nsakv.reshape(dec_batch, dec_seq, 512), (0, 2, 1))
    pad_tail = lambda a: jnp.pad(a, ((0, 0), (0, 0), (0, NSA_TK - dec_seq))).astype(BF16)
    kst_tail = pad_tail(new_nsa_t[:, 256:384])
    vst_tail = pad_tail(new_nsa_t[:, 384:512])
    win_t = jnp.transpose(state_nsa_win[0], (0, 2, 3, 4, 1)).reshape(dec_batch, 256, WINDOW)
    new_win_t = jnp.transpose(s_kvwin.reshape(dec_batch, dec_seq, 256), (0, 2, 1))
    band_t = jnp.concatenate([win_t, new_win_t], axis=2)
    band_pad = jnp.pad(band_t, ((0, 0), (0, 0), (0, WIN_BAND - band_t.shape[2]))).astype(BF16)
    ob_s = _nsa_attention(s_qb.reshape(dec_batch, dec_seq, 512), kct_s, vct_s, kst_s, vst_s,
                          (kst_tail, vst_tail), band_pad[:, 0:128], band_pad[:, 128:256],
                          s_gb.reshape(dec_batch, dec_seq, 128), tables_s, dec_seq, tile_base)

    g1 = g_norm[0, 1].reshape(1, D_MODEL)
    wf, wn, wo = w_out_fox[0].astype(BF16), w_out_nsa[0].astype(BF16), w_out[0].astype(BF16)
    wu, wd = w_up[0].astype(BF16), w_down[0].astype(BF16)
    y_p = _out_mlp(xp, oa_p, ob_p, p_gm, mod_p[2:6], g1, wf, wn, wo, wu, wd, 512)
    y_s = _out_mlp(xs, oa_s.reshape(dec_batch * dec_seq, 512), ob_s.reshape(dec_batch * dec_seq, 512),
                   s_gm, mod_s[2:6], g1, wf, wn, wo, wu, wd, dec_batch * dec_seq)

    win_keep = min(WINDOW, seq)
    new_win_sample = jnp.transpose(band_t[:, :, dec_seq:].reshape(dec_batch, 2, NSA_G, HEAD_DIM, WINDOW),
                                   (0, 4, 1, 2, 3))[None]
    return (
        y_p.reshape(1, seq, D_MODEL),
        y_s.reshape(dec_batch, dec_seq, D_MODEL),
        p_foxkv.reshape(1, 1, seq, 2, FOX_HEADS, HEAD_DIM),
        s_foxkv.reshape(1, dec_batch, dec_seq, 2, FOX_HEADS, HEAD_DIM),
        jnp.transpose(p_lft).reshape(1, 1, seq, FOX_HEADS),
        jnp.transpose(s_lft).reshape(1, dec_batch, dec_seq, FOX_HEADS),
        p_nsakv.reshape(1, 1, seq, 4, NSA_G, HEAD_DIM),
        s_nsakv.reshape(1, dec_batch, dec_seq, 4, NSA_G, HEAD_DIM),
        p_kvwin[seq - win_keep:].reshape(1, 1, win_keep, 2, NSA_G, HEAD_DIM),
        new_win_sample,
    )
```

```python
import functools
import math

import numpy as np
import jax
import jax.numpy as jnp
from jax import lax
from jax.experimental import pallas as pl
from jax.experimental.pallas import tpu as pltpu

F32 = jnp.float32
BF16 = jnp.bfloat16

D_MODEL = 1024
HEAD_DIM = 64
FOX_HEADS = 8
NSA_HEADS = 8
NSA_G = 2
NSA_HPG = 4
CMP_BLOCK = 32
CMP_STRIDE = 16
CMP_HIDDEN = 128
SEL_BLOCK = 64
N_SEL = 16
N_FORCED = 3
WINDOW = 512
REL_BUCKETS = 32
REL_MAX_DIST = 128
PAGE = 128
EPS = 1e-6
D_FF = 4 * D_MODEL

LANE = 128
LOG2E = 1.4426950408889634
QSCALE = LOG2E / math.sqrt(HEAD_DIM)
NEG = -float(2 ** 30)
M_INIT = -1e30
VMEM_LIMIT = 58 * 1024 * 1024

NSA_TQ = 256
NSA_TK = 256
NSA_FAR = 1024
WIN_BAND = WINDOW + NSA_TQ
SEL_NEAR = 2 * NSA_TK
CMP_NEAR = 32
SEL_SLAB = LANE * SEL_BLOCK
FOX_TQ = 512
FOX_TK = 512
FOX_FAR = 1024
FOX_STRIPS = 2
PREP_ROWS = 2048
FOX_DEC_PAGES = 8
NSA_DEC_PAGES = PREP_ROWS // PAGE


ATTN_FLAGS = None


def _cparams(sem, flags=None):
    return pltpu.CompilerParams(dimension_semantics=sem, vmem_limit_bytes=VMEM_LIMIT, flags=flags)


def _split2(x):
    hi = x.astype(BF16)
    lo = (x - hi.astype(F32)).astype(BF16)
    return hi, lo


def _split3(x):
    hi = x.astype(BF16)
    r = x - hi.astype(F32)
    mid = r.astype(BF16)
    lo = (r - mid.astype(F32)).astype(BF16)
    return hi, mid, lo


def _dot(a, b):
    return jnp.dot(a, b, preferred_element_type=F32)


def _dot_nt(a, b):
    return lax.dot_general(a, b, (((1,), (1,)), ((), ())), preferred_element_type=F32)


def _ada_kernel(c_ref, w_ref, b_ref, o_ref):
    c = c_ref[...]
    s = c * jax.nn.sigmoid(c)
    o_ref[...] = jnp.dot(s, w_ref[...], precision=lax.Precision.HIGHEST,
                         preferred_element_type=F32) + b_ref[...]


def _ada(c_all, w_ada, b_ada):
    rows = c_all.shape[0]
    n = w_ada.shape[1]
    tn = 1536
    return pl.pallas_call(
        _ada_kernel,
        grid=(n // tn,),
        in_specs=[pl.BlockSpec((rows, D_MODEL), lambda j: (0, 0)),
                  pl.BlockSpec((D_MODEL, tn), lambda j: (0, j)),
                  pl.BlockSpec((1, tn), lambda j: (0, j))],
        out_specs=pl.BlockSpec((rows, tn), lambda j: (0, j)),
        out_shape=jax.ShapeDtypeStruct((rows, n), F32),
        compiler_params=_cparams(("arbitrary",)),
        name="ada",
    )(c_all, w_ada, b_ada.reshape(1, n))


_C_QA, _C_KA, _C_VA, _C_QB = 0, 512, 1024, 1536
_C_KV = 2048
_C_F = 2816
_C_G = 2944
_C_M = 3072
_C_END = 5120


def _inproj_kernel(x_ref, sh_ref, sc_ref, g0_ref, w_ref, bd_ref, gq_ref, bf_ref,
                   foxkv_ref, nsakv_ref, kvwin_ref, lft_ref, qa_ref, kat_ref, va_ref,
                   qb_ref, kwt_ref, vwt_ref, gb_ref, gm_ref):
    x = x_ref[...]
    ms = jnp.mean(x * x, axis=-1, keepdims=True)
    y = x * lax.rsqrt(ms + EPS) * g0_ref[...]
    h = y * (1.0 + sc_ref[...]) + sh_ref[...]
    hb = h.astype(BF16)
    bd = bd_ref[...]

    def head_norm(z, gain):
        zz = (z * z).astype(BF16)
        w = z.shape[1]
        if w == LANE:
            msq = _dot(zz, bd[:LANE, :LANE])
        else:
            msq = jnp.concatenate([_dot(zz[:, k:k + 256], bd) for k in range(0, w, 256)], axis=1)
        return z * lax.rsqrt(msq + EPS) * gain

    za = _dot(hb, w_ref[:, _C_QA:_C_KV])
    qa = head_norm(za[:, _C_QA:_C_KA], gq_ref[:, 0:512])
    ka = head_norm(za[:, _C_KA:_C_VA], gq_ref[:, 512:1024])
    va = za[:, _C_VA:_C_QB]
    qb = head_norm(za[:, _C_QB:_C_KV], gq_ref[:, 1024:1536])
    foxkv_ref[:, 0:512] = ka
    foxkv_ref[:, 512:1024] = va
    qa_ref[...] = (qa * QSCALE).astype(BF16)
    kat_ref[...] = ka.T.astype(BF16)
    va_ref[...] = va.astype(BF16)
    qb_ref[...] = (qb * QSCALE).astype(BF16)

    zb = _dot(hb, w_ref[:, _C_KV:_C_M])
    ksel = head_norm(zb[:, 256:384], gq_ref[:, 1536:1664])
    kwin = head_norm(zb[:, 512:640], gq_ref[:, 1664:1792])
    vwin = zb[:, 640:768]
    nsakv_ref[:, 0:256] = zb[:, 0:256]
    nsakv_ref[:, 256:384] = ksel
    nsakv_ref[:, 384:512] = zb[:, 384:512]
    kvwin_ref[:, 0:128] = kwin
    kvwin_ref[:, 128:256] = vwin
    kwt_ref[...] = kwin.T.astype(BF16)
    vwt_ref[...] = vwin.T.astype(BF16)
    zf = zb[:, 768:896] + bf_ref[...]
    lf = jnp.minimum(zf, 0.0) - jnp.log(1.0 + jnp.exp(-jnp.abs(zf)))
    lft_ref[...] = lf.T[0:8, :]
    gb_ref[...] = jax.nn.sigmoid(zb[:, 896:1024])

    zm = _dot(hb, w_ref[:, _C_M:_C_END])
    gm_ref[...] = jax.nn.sigmoid(zm)


def _inproj(x, shift, scale, g0, w_cat, bd, gq, bf_pad):
    rows = x.shape[0]
    tr = 256
    mrows = shift.shape[0]
    mod_spec = (pl.BlockSpec((1, D_MODEL), lambda i: (0, 0)) if mrows == 1
                else pl.BlockSpec((tr, D_MODEL), lambda i: (i, 0)))
    const = lambda shape: pl.BlockSpec(shape, lambda i: tuple(0 for _ in shape))
    row_spec = lambda w: pl.BlockSpec((tr, w), lambda i: (i, 0))
    col_spec = lambda h: pl.BlockSpec((h, tr), lambda i: (0, i))
    out_shape = (
        jax.ShapeDtypeStruct((rows, 1024), F32),
        jax.ShapeDtypeStruct((rows, 512), F32),
        jax.ShapeDtypeStruct((rows, 256), F32),
        jax.ShapeDtypeStruct((8, rows), F32),
        jax.ShapeDtypeStruct((rows, 512), BF16),
        jax.ShapeDtypeStruct((512, rows), BF16),
        jax.ShapeDtypeStruct((rows, 512), BF16),
        jax.ShapeDtypeStruct((rows, 512), BF16),
        jax.ShapeDtypeStruct((128, rows), BF16),
        jax.ShapeDtypeStruct((128, rows), BF16),
        jax.ShapeDtypeStruct((rows, 128), F32),
        jax.ShapeDtypeStruct((rows, 2048), F32),
    )
    out_specs = (row_spec(1024), row_spec(512), row_spec(256), col_spec(8), row_spec(512),
                 col_spec(512), row_spec(512), row_spec(512), col_spec(128), col_spec(128),
                 row_spec(128), row_spec(2048))
    return pl.pallas_call(
        _inproj_kernel,
        grid=(rows // tr,),
        in_specs=[row_spec(D_MODEL), mod_spec, mod_spec, const((1, D_MODEL)),
                  const((D_MODEL, _C_END)), const((256, 256)), const((1, 1792)), const((1, 128))],
        out_specs=out_specs,
        out_shape=out_shape,
        compiler_params=_cparams(("arbitrary",)),
        name="inproj",
    )(x, shift, scale, g0, w_cat, bd, gq, bf_pad)


def _cumsum_kernel(x_ref, u_ref, pick_ref, bt_ref, o_ref):
    u = u_ref[...]
    c2 = sum(_dot(p, u) for p in _split3(x_ref[...]))
    pick = pick_ref[...]
    totb = sum(_dot(p, pick) for p in _split3(c2))
    bt = bt_ref[...]
    offs = sum(_dot(bt, p) for p in _split3(totb))
    o_ref[...] = (c2 + offs) * LOG2E


def _cumsum(lft):
    heads, s = lft.shape
    nblk = s // LANE
    rows = heads * nblk
    x = lft.reshape(rows, LANE)
    u = jnp.asarray(np.triu(np.ones((LANE, LANE), np.float32)), BF16)
    pick = np.zeros((LANE, LANE), np.float32)
    pick[LANE - 1, :] = 1.0
    r = np.arange(rows)
    bt = ((r[:, None] // nblk == r[None, :] // nblk) & (r[None, :] < r[:, None])).astype(np.float32)
    full = lambda shape: pl.BlockSpec(shape, lambda i: (0, 0))
    out = pl.pallas_call(
        _cumsum_kernel,
        grid=(1,),
        in_specs=[full((rows, LANE)), full((LANE, LANE)), full((LANE, LANE)), full((rows, rows))],
        out_specs=full((rows, LANE)),
        out_shape=jax.ShapeDtypeStruct((rows, LANE), F32),
        compiler_params=_cparams(("arbitrary",)),
        name="cumsum",
    )(x, u, jnp.asarray(pick, BF16), jnp.asarray(bt, BF16))
    return out.reshape(heads, nblk, LANE)


def _online_update(t, v1, m_ref, acc_ref, v_transposed=False):
    tk = t.shape[1]
    m_prev = m_ref[...]
    m_new = jnp.maximum(m_prev, jnp.max(t, axis=1, keepdims=True))
    alpha = jnp.exp2(m_prev - m_new)
    p = jnp.exp2(t - jnp.tile(m_new, (1, tk // LANE))).astype(BF16)
    pv = _dot_nt(p, v1) if v_transposed else _dot(p, v1)
    acc_ref[...] = acc_ref[...] * alpha + pv
    m_ref[...] = m_new


def _normalize(acc):
    return acc[:, 0:HEAD_DIM] / acc[:, HEAD_DIM:2 * HEAD_DIM]


def _with_ones(v, axis):
    shape = list(v.shape)
    shape[axis] = HEAD_DIM
    return jnp.concatenate([v, jnp.ones(shape, v.dtype)], axis=axis)


def _fox_kernel(q_ref, kt_ref, v_ref, ck_ref, o_ref, m_ref, acc_ref):
    i = pl.program_id(1)
    rs = FOX_TQ // FOX_STRIPS
    chains = [(hh, st) for hh in range(2) for st in range(FOX_STRIPS)]
    m_ref[...] = jnp.full(m_ref.shape, M_INIT, F32)
    acc_ref[...] = jnp.zeros(acc_ref.shape, F32)

    def tile(koff, width, masked):
        blk0 = koff // LANE
        widths = [min(width, (st + 1) * rs) if masked else width for _, st in chains]
        nck = [-jnp.concatenate([ck_ref[hh, blk0 + c] for c in range(width // LANE)], axis=1) for hh in range(2)]
        scores = [_dot(q_ref[st * rs:(st + 1) * rs, 64 * hh:64 * hh + 64], kt_ref[64 * hh:64 * hh + 64, pl.ds(koff, w)])
                  + jnp.tile(nck[hh][:, 0:w], (rs // 8, 1))
                  for (hh, st), w in zip(chains, widths)]
        for (hh, st), w, t in zip(chains, widths, scores):
            if masked:
                row = st * rs + lax.broadcasted_iota(jnp.int32, (rs, w), 0)
                col = lax.broadcasted_iota(jnp.int32, (rs, w), 1)
                t = jnp.where(row >= col, t, NEG)
            rows = pl.ds(st * rs, rs)
            _online_update(t, v_ref[pl.ds(koff, w), LANE * hh:LANE * (hh + 1)], m_ref.at[hh, rows],
                           acc_ref.at[hh, rows])

    def body(j, carry):
        tile(pl.multiple_of(j * FOX_FAR, FOX_FAR), FOX_FAR, False)
        return carry

    lax.fori_loop(0, i // 2, body, 0)

    @pl.when(i % 2 == 1)
    def _():
        tile(pl.multiple_of((i - 1) * FOX_TK, FOX_TK), FOX_TK, False)

    tile(pl.multiple_of(i * FOX_TK, FOX_TK), FOX_TK, True)
    o_ref[...] = jnp.concatenate([_normalize(acc_ref[hh]) for hh in range(2)], axis=1)


def _fox_prompt(qa, kat, va1, ck_rep):
    s = qa.shape[0]
    nblk = s // LANE
    return pl.pallas_call(
        _fox_kernel,
        grid=(FOX_HEADS // 2, s // FOX_TQ),
        in_specs=[pl.BlockSpec((FOX_TQ, LANE), lambda hp, i: (i, hp)),
                  pl.BlockSpec((LANE, s), lambda hp, i: (hp, 0)),
                  pl.BlockSpec((s, 2 * LANE), lambda hp, i: (0, hp)),
                  pl.BlockSpec((2, nblk, 8, LANE), lambda hp, i: (hp, 0, 0, 0))],
        out_specs=pl.BlockSpec((FOX_TQ, LANE), lambda hp, i: (i, hp)),
        out_shape=jax.ShapeDtypeStruct((s, 512), F32),
        scratch_shapes=[pltpu.VMEM((2, FOX_TQ, LANE), F32), pltpu.VMEM((2, FOX_TQ, LANE), F32)],
        compiler_params=_cparams(("arbitrary", "arbitrary"), ATTN_FLAGS),
        name="fox",
    )(qa, kat, va1, ck_rep)


def _gelu_tanh(x):
    return 0.5 * x * (1.0 + jnp.tanh(math.sqrt(2.0 / math.pi) * (x + 0.044715 * (x * x * x))))


def _nsaprep_kernel(*refs, n_parts, n_prefetch, feature_major):
    refs = refs[n_prefetch:]
    part_refs = refs[:n_parts]
    (look_ref, pe_ref, w1_ref, w2_ref, bd_ref, gkc_ref,
     kct_ref, vct_ref, kst_ref, vst_ref, xk_ref, xv_ref) = refs[n_parts:]
    xs_refs = (xk_ref, xv_ref)
    prow = PREP_ROWS // n_parts
    for k in range(n_parts):
        lo, hi = k * prow, (k + 1) * prow
        if feature_major:
            blk = part_refs[k][...].reshape(4, LANE, prow)
            xk_ref[lo:hi, :] = blk[0].T
            xv_ref[lo:hi, :] = blk[1].T
            kst_ref[0, :, lo:hi] = blk[2].astype(BF16)
            vsel_t = blk[3].astype(BF16)
        else:
            blk = part_refs[k][...]
            xk_ref[lo:hi, :] = blk[:, 0:128]
            xv_ref[lo:hi, :] = blk[:, 128:256]
            kst_ref[0, :, lo:hi] = blk[:, 256:384].T.astype(BF16)
            vsel_t = blk[:, 384:512].T.astype(BF16)
        for g in range(NSA_G):
            vst_ref[0, g, 0:HEAD_DIM, lo:hi] = vsel_t[HEAD_DIM * g:HEAD_DIM * (g + 1)]
            vst_ref[0, g, HEAD_DIM:2 * HEAD_DIM, lo:hi] = jnp.ones((HEAD_DIM, prow), BF16)
    if feature_major:
        look = look_ref[...].reshape(4, LANE, PAGE)
        xk_ref[PREP_ROWS:PREP_ROWS + CMP_STRIDE, :] = look[0].T[0:CMP_STRIDE, :]
        xv_ref[PREP_ROWS:PREP_ROWS + CMP_STRIDE, :] = look[1].T[0:CMP_STRIDE, :]
    else:
        look = look_ref[...]
        xk_ref[PREP_ROWS:PREP_ROWS + CMP_STRIDE, :] = look[:, 0:128]
        xv_ref[PREP_ROWS:PREP_ROWS + CMP_STRIDE, :] = look[:, 128:256]

    ntok = PREP_ROWS // CMP_STRIDE
    outs = []
    for kv in range(2):
        hid = jnp.zeros((ntok, NSA_G * CMP_HIDDEN), F32)
        for half in range(2):
            xcat = jnp.concatenate(
                [xs_refs[kv][pl.ds(half * CMP_STRIDE + r, ntok, stride=CMP_STRIDE), :]
                 for r in range(CMP_STRIDE)], axis=1)
            xcat = (xcat + pe_ref[kv, half]).astype(BF16)
            hid = hid + _dot(xcat, w1_ref[kv, half])
        act = _gelu_tanh(hid).astype(BF16)
        outs.append(_dot(act, w2_ref[kv]))
    kc, vc = outs
    msq = _dot((kc * kc).astype(BF16), bd_ref[...])
    kc = kc * lax.rsqrt(msq + EPS) * gkc_ref[...]
    kct_ref[0] = kc.T.astype(BF16)
    vct_ref[0] = vc.T.astype(BF16)


def _nsaprep(parts_arrays, part_specs, look_array, look_spec, grid, batch, length, prefetch,
             pe, w1, w2, bd128, gkc, feature_major):
    n_parts = len(part_specs)
    ntok = PREP_ROWS // CMP_STRIDE
    ncp = length // CMP_STRIDE
    npf = len(prefetch)

    def cm(shape):
        return pl.BlockSpec(shape, lambda b, t, *pf: tuple(0 for _ in shape))

    in_specs = list(part_specs) + [look_spec, cm(pe.shape), cm(w1.shape), cm(w2.shape),
                                   cm(bd128.shape), cm(gkc.shape)]
    out_specs = (
        pl.BlockSpec((1, 128, ntok), lambda b, t, *pf: (b, 0, t)),
        pl.BlockSpec((1, 128, ntok), lambda b, t, *pf: (b, 0, t)),
        pl.BlockSpec((1, 128, PREP_ROWS), lambda b, t, *pf: (b, 0, t)),
        pl.BlockSpec((1, NSA_G, 128, PREP_ROWS), lambda b, t, *pf: (b, 0, 0, t)),
    )
    out_shape = (
        jax.ShapeDtypeStruct((batch, 128, ncp), BF16),
        jax.ShapeDtypeStruct((batch, 128, ncp), BF16),
        jax.ShapeDtypeStruct((batch, 128, length), BF16),
        jax.ShapeDtypeStruct((batch, NSA_G, 128, length), BF16),
    )
    grid_spec = pltpu.PrefetchScalarGridSpec(
        num_scalar_prefetch=npf, grid=grid, in_specs=in_specs, out_specs=out_specs,
        scratch_shapes=[pltpu.VMEM((PREP_ROWS + CMP_STRIDE, LANE), F32),
                        pltpu.VMEM((PREP_ROWS + CMP_STRIDE, LANE), F32)])
    return pl.pallas_call(
        functools.partial(_nsaprep_kernel, n_parts=n_parts, n_prefetch=npf, feature_major=feature_major),
        grid_spec=grid_spec,
        out_shape=out_shape,
        compiler_params=_cparams(("arbitrary", "arbitrary")),
        name="nsaprep",
    )(*prefetch, *parts_arrays, look_array, pe, w1, w2, bd128, gkc)


def _nsa_kernel(*refs, tq, tile_base, single_shot, ncp, nbp):
    if single_shot:
        (q_ref, kct_ref, vct_ref, kst_ref, vst_ref, kstt_ref, vstt_ref, kwt_ref, vwt_ref, g_ref,
         tw_ref, ts_ref, tc_ref, emain_ref, etail_ref, wcs_ref, eg_ref, o_ref) = refs
    else:
        (q_ref, kct_ref, vct_ref, kst_ref, vst_ref, kwt_ref, vwt_ref, g_ref,
         tw_ref, ts_ref, tc_ref, etab_ref, wcs_ref, eg_ref,
         o_ref, sel_ref, m_ref, acc_ref) = refs
    rows = NSA_HPG * tq
    i_abs = tile_base + pl.program_id(2)
    q0 = i_abs * NSA_TQ
    q4 = q_ref[0]
    qhs = [q4[:, 64 * hh:64 * hh + 64] for hh in range(NSA_HPG)]
    qs = jnp.concatenate(qhs, axis=0)

    n_iota = lax.broadcasted_iota(jnp.int32, (1, ncp), 1)
    base = jnp.where(n_iota >= 16 * i_abs + 16, NEG, 0.0)
    pr = lax.broadcasted_iota(jnp.int32, (2 * CMP_NEAR, ncp), 0) % CMP_NEAR
    pn = lax.broadcasted_iota(jnp.int32, (2 * CMP_NEAR, ncp), 1)
    place = jnp.where(pn == 16 * i_abs - 16 + pr, 1.0, 0.0).astype(BF16)
    kct = kct_ref[0]
    vct = vct_ref[0]
    psum = jnp.zeros((tq, ncp), F32)
    o_cmp = []
    cmp_logits = [_dot(qhs[hh], kct) + (_dot(tc_ref[0, hh * tq:(hh + 1) * tq, :], place) + base)
                  for hh in range(NSA_HPG)]
    for hh in range(NSA_HPG):
        s = cmp_logits[hh]
        m = jnp.max(s, axis=1, keepdims=True)
        p = jnp.exp2(s - m)
        l = jnp.sum(p, axis=1, keepdims=True)
        inv = jnp.where(m > 0.5 * NEG, 1.0 / l, 0.0)
        pn_ = p * inv
        psum = psum + pn_
        o_cmp.append(_dot_nt(pn_.astype(BF16), vct))
    wcs = wcs_ref[...]
    imp = sum(_dot(part, wcs) for part in _split2(psum))

    posq = q0 + lax.broadcasted_iota(jnp.int32, (tq, 1), 0)
    blk = lax.broadcasted_iota(jnp.int32, (tq, nbp), 1)
    blk_f = blk.astype(F32)
    cur = posq // SEL_BLOCK
    forced = (blk == 0) | (blk == cur) | (blk == cur - 1)
    sel = jnp.where(forced, 0.0, NEG)
    score = jnp.where(forced | (blk * SEL_BLOCK > posq), -jnp.inf, imp)
    for _ in range(N_SEL - N_FORCED):
        mx = jnp.max(score, axis=1, keepdims=True)
        cand = jnp.where(score == mx, blk_f, float(4 * nbp))
        amin = jnp.min(cand, axis=1, keepdims=True)
        hit = blk_f == amin
        sel = jnp.where(hit & (mx > -jnp.inf), 0.0, sel)
        score = jnp.where(hit, -jnp.inf, score)

    if single_shot:
        skv = kst_ref.shape[2]
        slab = min(SEL_SLAB, skv)
        selb = sel.astype(BF16)
        emain = emain_ref[...]
        mb = jnp.concatenate([_dot(selb[:, LANE * c:LANE * (c + 1)], emain) for c in range(skv // slab)], axis=1)
        ct = (skv // SEL_BLOCK) // LANE
        mb_tail = _dot(selb[:, LANE * ct:LANE * (ct + 1)], etail_ref[...])
        near = ts_ref[0]
        s_main = _dot(qs, kst_ref[0])
        t_main = (s_main.reshape(NSA_HPG, tq, skv) + mb[None]).reshape(rows, skv)
        t_main = jnp.concatenate([t_main[:, :skv - NSA_TK], t_main[:, skv - NSA_TK:] + near[:, 0:NSA_TK]], axis=1)
        s_tail = _dot(qs, kstt_ref[0])
        t_tail = (s_tail.reshape(NSA_HPG, tq, NSA_TK) + mb_tail[None]).reshape(rows, NSA_TK) + near[:, NSA_TK:SEL_NEAR]
        m = jnp.maximum(jnp.max(t_main, axis=1, keepdims=True), jnp.max(t_tail, axis=1, keepdims=True))
        p_main = jnp.exp2(t_main - m).astype(BF16)
        p_tail = jnp.exp2(t_tail - m).astype(BF16)
        o_sel = _normalize(_dot_nt(p_main, vst_ref[0, 0]) + _dot_nt(p_tail, vstt_ref[0, 0]))
    else:
        sel_ref[...] = sel
        m_ref[...] = jnp.full(m_ref.shape, M_INIT, F32)
        acc_ref[...] = jnp.zeros(acc_ref.shape, F32)

        def sel_tile(koff, width, near_tbl):
            kt = kst_ref[0, :, pl.ds(koff, width)]
            vt = vst_ref[0, 0, :, pl.ds(koff, width)]
            mbs = []
            for u in range(width // NSA_TK):
                j = koff // NSA_TK + u
                slab_off = pl.multiple_of((j // 32) * LANE, LANE)
                sel128 = sel_ref[:, pl.ds(slab_off, LANE)]
                mbs.append(_dot(sel128.astype(BF16), etab_ref[j % 32]))
            mb = mbs[0] if len(mbs) == 1 else jnp.concatenate(mbs, axis=1)
            logits = []
            for hh in range(NSA_HPG):
                add = mb if near_tbl is None else mb + near_tbl[hh * tq:(hh + 1) * tq]
                logits.append(_dot(qhs[hh], kt) + add)
            for hh in range(NSA_HPG):
                hrows = pl.ds(hh * tq, tq)
                _online_update(logits[hh], vt, m_ref.at[hrows], acc_ref.at[hrows], v_transposed=True)

        n_far_tiles = jnp.maximum(i_abs - 1, 0)
        n_big = n_far_tiles // (NSA_FAR // NSA_TK)
        rem = n_far_tiles % (NSA_FAR // NSA_TK)

        def far_body(j, carry):
            sel_tile(pl.multiple_of(j * NSA_FAR, NSA_FAR), NSA_FAR, None)
            return carry

        lax.fori_loop(0, n_big, far_body, 0)
        rem_off = n_big * NSA_FAR

        @pl.when(rem >= 2)
        def _():
            sel_tile(pl.multiple_of(rem_off, 2 * NSA_TK), 2 * NSA_TK, None)

        @pl.when(rem % 2 == 1)
        def _():
            sel_tile(pl.multiple_of(rem_off + (rem // 2) * 2 * NSA_TK, NSA_TK), NSA_TK, None)

        @pl.when(i_abs >= 1)
        def _():
            sel_tile(pl.multiple_of((i_abs - 1) * NSA_TK, NSA_TK), NSA_TK, ts_ref[0, :, 0:NSA_TK])

        sel_tile(pl.multiple_of(i_abs * NSA_TK, NSA_TK), NSA_TK, ts_ref[0, :, NSA_TK:SEL_NEAR])
        o_sel = _normalize(acc_ref[...])

    band0 = pl.multiple_of((i_abs - tile_base if single_shot else i_abs) * NSA_TQ, NSA_TQ)
    ktw = kwt_ref[0, :, pl.ds(band0, WIN_BAND)]
    vtw = vwt_ref[0, 0, :, pl.ds(band0, WIN_BAND)]
    b_iota = lax.broadcasted_iota(jnp.int32, (1, WIN_BAND), 1)
    before_start = jnp.where(b_iota >= WINDOW - q0, 0.0, NEG)
    win_logits = [_dot(qhs[hh], ktw) + (tw_ref[0, hh * tq:(hh + 1) * tq, :] + before_start)
                  for hh in range(NSA_HPG)]
    o_win = []
    for hh in range(NSA_HPG):
        s = win_logits[hh]
        p = jnp.exp2(s - jnp.max(s, axis=1, keepdims=True)).astype(BF16)
        o_win.append(_normalize(_dot_nt(p, vtw)))

    def heads_to_lanes(o):
        return jnp.concatenate([o[hh * tq:(hh + 1) * tq] for hh in range(NSA_HPG)], axis=1)

    branches = (jnp.concatenate(o_cmp, axis=1), heads_to_lanes(o_sel), jnp.concatenate(o_win, axis=1))
    gparts = _split2(g_ref[0])
    out = jnp.zeros((tq, NSA_HPG * HEAD_DIM), F32)
    for br in range(3):
        gexp = sum(_dot(part, eg_ref[0, br]) for part in gparts)
        out = out + gexp * branches[br]
    o_ref[0] = out


def _nsa_attention(q, kct, vct, kst, vst, tail, kwt, vwt, gates, tables, tq, tile_base):
    batch, sq, _ = q.shape
    n_qt = sq // tq
    ncp = kct.shape[2]
    skv = kst.shape[2]
    lw = kwt.shape[2]
    rows = NSA_HPG * tq
    single_shot = tail is not None
    nbp = tables["wcs"].shape[1]
    grp = lambda n: pl.BlockSpec((1, 64, n), lambda b, g, i: (b, g, 0))
    per_g = lambda w: pl.BlockSpec((1, rows, w), lambda b, g, i: (g, 0, 0))
    const = lambda a: pl.BlockSpec(a.shape, lambda b, g, i: tuple(0 for _ in a.shape))
    v1 = lambda n: pl.BlockSpec((1, 1, 2 * HEAD_DIM, n), lambda b, g, i: (b, g, 0, 0))
    in_specs = [pl.BlockSpec((1, tq, 256), lambda b, g, i: (b, i, g)), grp(ncp), grp(ncp), grp(skv), v1(skv)]
    args = [q, kct, vct, kst, vst]
    if single_shot:
        in_specs += [grp(NSA_TK), v1(NSA_TK)]
        args += list(tail)
    in_specs += [grp(lw), v1(lw), pl.BlockSpec((1, tq, 128), lambda b, g, i: (b, i, 0)),
                 per_g(WIN_BAND), per_g(SEL_NEAR), per_g(2 * CMP_NEAR)]
    args += [kwt, vwt, gates, tables["tw"], tables["ts"], tables["tc"]]
    if single_shot:
        in_specs += [const(tables["emain"]), const(tables["etail"])]
        args += [tables["emain"], tables["etail"]]
        scratch = []
    else:
        in_specs += [const(tables["etab"])]
        args += [tables["etab"]]
        scratch = [pltpu.VMEM((tq, nbp), F32), pltpu.VMEM((rows, LANE), F32), pltpu.VMEM((rows, LANE), F32)]
    in_specs += [const(tables["wcs"]), pl.BlockSpec((1, 3, 128, 256), lambda b, g, i: (g, 0, 0, 0))]
    args += [tables["wcs"], tables["eg"]]
    return pl.pallas_call(
        functools.partial(_nsa_kernel, tq=tq, tile_base=tile_base, single_shot=single_shot, ncp=ncp, nbp=nbp),
        grid=(batch, NSA_G, n_qt),
        in_specs=in_specs,
        out_specs=pl.BlockSpec((1, tq, 256), lambda b, g, i: (b, i, g)),
        out_shape=jax.ShapeDtypeStruct((batch, sq, 512), F32),
        scratch_shapes=scratch,
        compiler_params=_cparams(("arbitrary", "arbitrary", "arbitrary"), ATTN_FLAGS),
        name="nsa",
    )(*args)


def _t5_bucket(dist):
    d = jnp.maximum(dist, 0)
    exact = REL_BUCKETS // 2
    far = exact + (jnp.log(jnp.maximum(d, 1).astype(F32) / exact)
                   / math.log(REL_MAX_DIST / exact) * (REL_BUCKETS - exact)).astype(jnp.int32)
    return jnp.where(d < exact, d, jnp.minimum(far, REL_BUCKETS - 1))


def _nsa_tables(rel_bias, tq, ncp, nbp, skv=None):
    dmax = WINDOW + NSA_TQ + 1
    by_dist = (rel_bias[_t5_bucket(jnp.arange(dmax))] - rel_bias[REL_BUCKETS - 1][None, :]) * LOG2E

    def toeplitz(w, off, lo, hi):
        n = w + tq
        k = np.arange(n)
        k = np.where(k < w, k, k - n)
        d = off - k
        g = jnp.where(jnp.asarray((d >= lo) & (d <= hi))[:, None], by_dist[np.clip(d, 0, dmax - 1)], NEG)
        flat = jnp.tile(g.T, (1, tq))[:, :tq * (n - 1)]
        t = flat.reshape(NSA_HEADS, tq, n - 1)[:, :, :w]
        return t.reshape(NSA_G, NSA_HPG * tq, w)

    tw = toeplitz(WIN_BAND, WINDOW, 0, WINDOW)
    ts = toeplitz(SEL_NEAR, NSA_TK, 0, dmax)
    tc = toeplitz(CMP_STRIDE * CMP_NEAR, NSA_TQ - CMP_BLOCK + 1, 0, dmax)[:, :, ::CMP_STRIDE]
    tc_hi = tc.astype(BF16)
    tc_lo = (tc - tc_hi.astype(F32)).astype(BF16)
    tables = {"tw": tw, "ts": ts, "tc": jnp.concatenate([tc_hi, tc_lo], axis=2)}

    m = np.arange(LANE)[:, None]
    if skv is None:
        key = np.arange(NSA_TK)[None, None, :]
        r = np.arange(32)[:, None, None]
        tables["etab"] = jnp.asarray((m[None] == 4 * r + key // SEL_BLOCK).astype(np.float32), BF16)
    else:
        slab = min(SEL_SLAB, skv)
        tables["emain"] = jnp.asarray((m == np.arange(slab)[None, :] // SEL_BLOCK).astype(np.float32), BF16)
        o = (skv // SEL_BLOCK) % LANE
        tables["etail"] = jnp.asarray((m == o + np.arange(NSA_TK)[None, :] // SEL_BLOCK).astype(np.float32), BF16)

    c0 = np.arange(ncp)[:, None] * CMP_STRIDE
    s0 = np.arange(nbp)[None, :] * SEL_BLOCK
    shared = np.minimum(c0 + CMP_BLOCK, s0 + SEL_BLOCK) - np.maximum(c0, s0)
    tables["wcs"] = jnp.asarray(np.maximum(shared, 0).astype(np.float32) / CMP_BLOCK, BF16)

    eg = np.zeros((NSA_G, 3, 128, 256), np.float32)
    for g in range(NSA_G):
        for br in range(3):
            for hh in range(NSA_HPG):
                eg[g, br, 3 * (NSA_HPG * g + hh) + br, 64 * hh:64 * hh + 64] = 1.0
    tables["eg"] = jnp.asarray(eg, BF16)
    return tables


def _foxdec_kernel(pt_ref, qbd_ref, knew_ref, vnew_ref, lfnew_ref, u_ref, *refs):
    npg = FOX_DEC_PAGES
    kv_refs = refs[:npg]
    lf_refs = refs[npg:2 * npg]
    o_ref, m_ref, l_ref, acc_ref, carry_ref = refs[2 * npg:]
    c = pl.program_id(1)
    nrow = FOX_HEADS * 8

    @pl.when(c == 0)
    def _():
        m_ref[...] = jnp.full(m_ref.shape, M_INIT, F32)
        l_ref[...] = jnp.zeros(l_ref.shape, F32)
        acc_ref[...] = jnp.zeros(acc_ref.shape, F32)
        carry_ref[...] = jnp.zeros(carry_ref.shape, F32)

    qbd = qbd_ref[0]
    u = u_ref[...]

    def cum_blocks(lf_blocks):
        lf = lf_blocks[0] if len(lf_blocks) == 1 else jnp.concatenate(lf_blocks, axis=0)
        res = sum(_dot(part, u) for part in _split3(lf))
        run = carry_ref[...]
        out = []
        for k in range(len(lf_blocks)):
            out.append((res[8 * k:8 * k + 8, 0:LANE] + run) * LOG2E)
            run = run + res[8 * k:8 * k + 8, LANE:2 * LANE]
        carry_ref[...] = run
        return out

    def biased(s, ck):
        return (s.reshape(FOX_HEADS, 8, s.shape[1]) - ck[:, None, :]).reshape(nrow, s.shape[1])

    kt = jnp.concatenate([kv_refs[k][0, 0].reshape(512, PAGE) for k in range(npg)], axis=1).astype(BF16)
    vt = jnp.concatenate([kv_refs[k][0, 1].reshape(512, PAGE) for k in range(npg)], axis=1).astype(BF16)
    ck = jnp.concatenate(cum_blocks([lf_refs[k][0] for k in range(npg)]), axis=1)
    t = biased(_dot(qbd, kt), ck)
    tk = t.shape[1]
    m_prev = m_ref[...]
    m_new = jnp.maximum(m_prev, jnp.max(t, axis=1, keepdims=True))
    alpha = jnp.exp2(m_prev - m_new)
    p = jnp.exp2(t - jnp.tile(m_new, (1, tk // LANE)))
    l_ref[...] = alpha * l_ref[...] + jnp.sum(p, axis=1, keepdims=True)
    acc_ref[...] = acc_ref[...] * jnp.tile(alpha, (1, 4)) + _dot_nt(p.astype(BF16), vt)
    m_ref[...] = m_new

    @pl.when(c == pl.num_programs(1) - 1)
    def _():
        t2 = biased(_dot(qbd, knew_ref[0]), cum_blocks([lfnew_ref[0]])[0])
        row = lax.broadcasted_iota(jnp.int32, (nrow, LANE), 0) % 8
        col = lax.broadcasted_iota(jnp.int32, (nrow, LANE), 1)
        t2 = jnp.where(col <= row, t2, NEG)
        m_prev2 = m_ref[...]
        m_new2 = jnp.maximum(m_prev2, jnp.max(t2, axis=1, keepdims=True))
        alpha2 = jnp.exp2(m_prev2 - m_new2)
        p2 = jnp.exp2(t2 - m_new2)
        l2 = alpha2 * l_ref[...] + jnp.sum(p2, axis=1, keepdims=True)
        acc2 = acc_ref[...] * jnp.tile(alpha2, (1, 4)) + _dot_nt(p2.astype(BF16), vnew_ref[0])
        o = acc2 / jnp.tile(l2, (1, 4))
        for h in range(FOX_HEADS):
            o_ref[0, :, 64 * h:64 * h + 64] = o[8 * h:8 * h + 8, 64 * h:64 * h + 64]


def _fox_decode(page_table, qbd, knew_t, vnew_t, lfnew, kv_pool, lft_pool):
    batch, n_pages = page_table.shape
    npg = FOX_DEC_PAGES
    u = jnp.asarray(np.concatenate([np.triu(np.ones((LANE, LANE), np.float32)),
                                    np.ones((LANE, LANE), np.float32)], axis=1), BF16)

    def seq(shape):
        return pl.BlockSpec(shape, lambda b, c, pt: (b,) + tuple(0 for _ in shape[1:]))

    def page(shape, k):
        return pl.BlockSpec(shape, lambda b, c, pt, k=k: (pt[b, c * npg + k],) + tuple(0 for _ in shape[1:]))

    in_specs = ([seq((1, 64, 512)), seq((1, 512, LANE)), seq((1, 512, LANE)), seq((1, 8, LANE)),
                 pl.BlockSpec((LANE, 2 * LANE), lambda b, c, pt: (0, 0))]
                + [page((1, 2, FOX_HEADS, HEAD_DIM, PAGE), k) for k in range(npg)]
                + [page((1, 8, PAGE), k) for k in range(npg)])
    grid_spec = pltpu.PrefetchScalarGridSpec(
        num_scalar_prefetch=1, grid=(batch, n_pages // npg), in_specs=in_specs,
        out_specs=seq((1, 8, 512)),
        scratch_shapes=[pltpu.VMEM((64, LANE), F32), pltpu.VMEM((64, LANE), F32),
                        pltpu.VMEM((64, 512), F32), pltpu.VMEM((8, LANE), F32)])
    return pl.pallas_call(
        _foxdec_kernel,
        grid_spec=grid_spec,
        out_shape=jax.ShapeDtypeStruct((batch, 8, 512), F32),
        compiler_params=_cparams(("arbitrary", "arbitrary")),
        name="foxdec",
    )(page_table, qbd, knew_t, vnew_t, lfnew, u, *([kv_pool] * npg), *([lft_pool] * npg))


def _outmlp_kernel(x_ref, oa_ref, ob_ref, gm_ref, m2_ref, m3_ref, m4_ref, m5_ref, g1_ref,
                   wf_ref, wn_ref, wo_ref, wu_ref, wd_ref, y_ref, y1_ref, h2_ref, acc_ref):
    c = pl.program_id(1)

    @pl.when(c == 0)
    def _():
        ya = _dot(oa_ref[...].astype(BF16), wf_ref[...])
        yb = _dot(ob_ref[...].astype(BF16), wn_ref[...])
        u = gm_ref[:, 0:D_MODEL] * ya + gm_ref[:, D_MODEL:2 * D_MODEL] * yb
        att = _dot(u.astype(BF16), wo_ref[...])
        y1 = x_ref[...] + m2_ref[...] * att
        y1_ref[...] = y1
        ms = jnp.mean(y1 * y1, axis=-1, keepdims=True)
        hn = y1 * lax.rsqrt(ms + EPS) * g1_ref[...]
        h2_ref[...] = (hn * (1.0 + m4_ref[...]) + m3_ref[...]).astype(BF16)
        acc_ref[...] = jnp.zeros(acc_ref.shape, F32)

    up = _dot(h2_ref[...], wu_ref[...])
    act = jnp.square(jnp.maximum(up, 0.0)).astype(BF16)
    acc_ref[...] += _dot(act, wd_ref[...])

    @pl.when(c == pl.num_programs(1) - 1)
    def _():
        y_ref[...] = y1_ref[...] + m5_ref[...] * acc_ref[...]


def _out_mlp(x, oa, ob, gm, mods, g1, wf, wn, wo, wu, wd, tr):
    rows = x.shape[0]
    ffc = 1024
    mrows = mods[0].shape[0]
    mod_spec = (pl.BlockSpec((1, D_MODEL), lambda i, c: (0, 0)) if mrows == 1
                else pl.BlockSpec((tr, D_MODEL), lambda i, c: (i, 0)))
    row = lambda w: pl.BlockSpec((tr, w), lambda i, c: (i, 0))
    const = lambda shape: pl.BlockSpec(shape, lambda i, c: (0, 0))
    return pl.pallas_call(
        _outmlp_kernel,
        grid=(rows // tr, D_FF // ffc),
        in_specs=[row(D_MODEL), row(512), row(512), row(2048), mod_spec, mod_spec, mod_spec, mod_spec,
                  const((1, D_MODEL)), const((512, D_MODEL)), const((512, D_MODEL)),
                  const((D_MODEL, D_MODEL)),
                  pl.BlockSpec((D_MODEL, ffc), lambda i, c: (0, c)),
                  pl.BlockSpec((ffc, D_MODEL), lambda i, c: (c, 0))],
        out_specs=row(D_MODEL),
        out_shape=jax.ShapeDtypeStruct((rows, D_MODEL), F32),
        scratch_shapes=[pltpu.VMEM((tr, D_MODEL), F32), pltpu.VMEM((tr, D_MODEL), BF16),
                        pltpu.VMEM((tr, D_MODEL), F32)],
        compiler_params=_cparams(("arbitrary", "arbitrary")),
        name="outmlp",
    )(x, oa, ob, gm, *mods, g1, wf, wn, wo, wu, wd)


def _prep_w_in(w_in):
    qa, ka, va, zf, qb, zkv, zg, zm = jnp.split(
        w_in, np.cumsum([512, 512, 512, 8, 512, 768, 24, 2048])[:-1].tolist(), axis=1)
    pad = lambda w: jnp.pad(w, ((0, 0), (0, 128 - w.shape[1])))
    return jnp.concatenate([qa, ka, va, qb, zkv, pad(zf), pad(zg), zm], axis=1).astype(BF16)


def _prep_cmp_weights(pe_cmp, w_cmp1, w_cmp2):
    eye = jnp.eye(NSA_G, dtype=F32)
    w1 = w_cmp1.reshape(2, 2, CMP_STRIDE, HEAD_DIM, CMP_HIDDEN)
    w1big = jnp.einsum('khrdj,gq->khrgdqj', w1, eye).reshape(
        2, 2, CMP_STRIDE * NSA_G * HEAD_DIM, NSA_G * CMP_HIDDEN)
    w2big = jnp.einsum('kjd,gq->kgjqd', w_cmp2, eye).reshape(2, NSA_G * CMP_HIDDEN, NSA_G * HEAD_DIM)
    pe = pe_cmp.reshape(2, 2, CMP_STRIDE, 1, HEAD_DIM)
    pe = jnp.broadcast_to(pe, (2, 2, CMP_STRIDE, NSA_G, HEAD_DIM)).reshape(2, 2, 1, CMP_STRIDE * NSA_G * HEAD_DIM)
    return pe, w1big.astype(BF16), w2big.astype(BF16)


def _block_diag_mean(n):
    r = np.arange(n)
    return jnp.asarray((r[:, None] // HEAD_DIM == r[None, :] // HEAD_DIM).astype(np.float32) / HEAD_DIM, BF16)


def kernel(x_prompt, x_sample, cache_fox_kv, cache_fox_logf, cache_nsa_kv, state_nsa_win, page_table,
           c_prompt, c_sample, w_ada, b_ada, g_norm, w_in, b_forget, g_qk_fox, g_qk_nsa,
           pe_cmp, w_cmp1, w_cmp2, rel_bias, w_out_fox, w_out_nsa, w_out, w_up, w_down):
    n_batch, seq, _ = x_prompt.shape
    dec_batch, dec_seq, _ = x_sample.shape
    n_pages = page_table.shape[1]
    past = n_pages * PAGE
    assert n_batch == 1 and dec_seq == 8 and w_ada.shape[0] == 1
    assert seq % PREP_ROWS == 0 and past % PREP_ROWS == 0 and state_nsa_win.shape[2] == WINDOW

    n_c = 1 + dec_batch
    c_rows = -(-n_c // 8) * 8
    c_all = jnp.pad(jnp.concatenate([c_prompt, c_sample], axis=0), ((0, c_rows - n_c), (0, 0)))
    mods = _ada(c_all, w_ada[0], b_ada[0])
    mod_p = [mods[0:1, k * D_MODEL:(k + 1) * D_MODEL] for k in range(6)]
    mod_s = [jnp.repeat(mods[1:n_c, k * D_MODEL:(k + 1) * D_MODEL], dec_seq, axis=0) for k in range(6)]

    w_cat = _prep_w_in(w_in[0])
    bd256 = _block_diag_mean(256)
    tile8 = lambda g: jnp.tile(g, 8)
    gq = jnp.concatenate([tile8(g_qk_fox[0, 0]), tile8(g_qk_fox[0, 1]), tile8(g_qk_nsa[0, 0]),
                          jnp.tile(g_qk_nsa[0, 2], 2), jnp.tile(g_qk_nsa[0, 3], 2)]).reshape(1, 1792)
    bf_pad = jnp.pad(b_forget[0], (0, 120)).reshape(1, 128)
    g0 = g_norm[0, 0].reshape(1, D_MODEL)
    xp = x_prompt.reshape(seq, D_MODEL)
    xs = x_sample.reshape(dec_batch * dec_seq, D_MODEL)
    (p_foxkv, p_nsakv, p_kvwin, p_lft, p_qa, p_kat, p_va, p_qb, p_kwt, p_vwt, p_gb, p_gm) = _inproj(
        xp, mod_p[0], mod_p[1], g0, w_cat, bd256, gq, bf_pad)
    (s_foxkv, s_nsakv, s_kvwin, s_lft, s_qa, s_kat, s_va, s_qb, s_kwt, s_vwt, s_gb, s_gm) = _inproj(
        xs, mod_s[0], mod_s[1], g0, w_cat, bd256, gq, bf_pad)

    ck = _cumsum(p_lft)
    ck_rep = jnp.broadcast_to(ck[:, :, None, :], (FOX_HEADS, seq // LANE, 8, LANE))
    va1 = _with_ones(p_va.reshape(seq, FOX_HEADS, HEAD_DIM), 2).reshape(seq, FOX_HEADS * 2 * HEAD_DIM)
    oa_p = _fox_prompt(p_qa, p_kat, va1, ck_rep)

    pe, w1big, w2big = _prep_cmp_weights(pe_cmp[0], w_cmp1[0], w_cmp2[0])
    bd128 = _block_diag_mean(128)
    gkc = jnp.tile(g_qk_nsa[0, 1], 2).reshape(1, 128)
    look_blocks = PREP_ROWS // CMP_STRIDE
    last_look_p = seq // CMP_STRIDE - 1
    kct_p, vct_p, kst_p, vst_p = _nsaprep(
        [p_nsakv], [pl.BlockSpec((PREP_ROWS, 512), lambda b, t: (t, 0))],
        p_nsakv, pl.BlockSpec((CMP_STRIDE, 512), lambda b, t: (jnp.minimum((t + 1) * look_blocks, last_look_p), 0)),
        (1, seq // PREP_ROWS), 1, seq, (), pe, w1big, w2big, bd128, gkc, False)
    tables_p = _nsa_tables(rel_bias, NSA_TQ, seq // CMP_STRIDE, max(seq // SEL_BLOCK, LANE))
    kwt_pad = jnp.pad(p_kwt, ((0, 0), (WINDOW, 0)))[None]
    vwt_pad = jnp.pad(_with_ones(p_vwt.reshape(NSA_G, HEAD_DIM, seq), 1), ((0, 0), (0, 0), (WINDOW, 0)))[None]
    ob_p = _nsa_attention(p_qb[None], kct_p, vct_p, kst_p, vst_p, None, kwt_pad, vwt_pad, p_gb[None],
                          tables_p, NSA_TQ, 0)[0]

    kv_pool = jnp.transpose(cache_fox_kv[0], (0, 2, 3, 4, 1))
    lft_pool = jnp.transpose(cache_fox_logf[0], (0, 2, 1))
    eye_h = jnp.eye(FOX_HEADS, dtype=BF16)
    qa_s = s_qa.reshape(dec_batch, dec_seq, FOX_HEADS, HEAD_DIM)
    qbd = jnp.einsum('bjhd,hk->bhjkd', qa_s, eye_h).reshape(dec_batch, 64, 512)

    def new_cols(a_t):
        a = jnp.transpose(a_t.reshape(a_t.shape[0], dec_batch, dec_seq), (1, 0, 2))
        return jnp.pad(a, ((0, 0), (0, 0), (0, LANE - dec_seq)))

    knew_t = new_cols(s_kat)
    vnew_t = new_cols(jnp.transpose(s_va))
    lfnew = new_cols(s_lft)
    oa_s = _fox_decode(page_table, qbd, knew_t, vnew_t, lfnew, kv_pool, lft_pool)

    nsa_pool = jnp.transpose(cache_nsa_kv[0], (0, 2, 3, 4, 1))
    npg = NSA_DEC_PAGES
    page_shape = (1, 4, NSA_G, HEAD_DIM, PAGE)
    part_specs = [pl.BlockSpec(page_shape, lambda b, t, pt, k=k: (pt[b, t * npg + k], 0, 0, 0, 0))
                  for k in range(npg)]
    look_spec = pl.BlockSpec(
        page_shape, lambda b, t, pt: (pt[b, jnp.minimum((t + 1) * npg, n_pages - 1)], 0, 0, 0, 0))
    kct_s, vct_s, kst_s, vst_s = _nsaprep(
        [nsa_pool] * npg, part_specs, nsa_pool, look_spec,
        (dec_batch, past // PREP_ROWS), dec_batch, past, (page_table,), pe, w1big, w2big, bd128, gkc, True)
    tile_base = past // NSA_TQ
    nb_dec = past // SEL_BLOCK + NSA_TK // SEL_BLOCK
    tables_s = _nsa_tables(rel_bias, dec_seq, past // CMP_STRIDE, -(-nb_dec // LANE) * LANE, skv=past)
    new_nsa_t = jnp.transpose(s_nsakv.reshape(dec_batch, dec_seq, 512), (0, 2, 1))
    pad_tail = lambda a: jnp.pad(a, ((0, 0), (0, 0), (0, NSA_TK - dec_seq))).astype(BF16)
    kst_tail = pad_tail(new_nsa_t[:, 256:384])
    vst_tail = _with_ones(pad_tail(new_nsa_t[:, 384:512]).reshape(dec_batch, NSA_G, HEAD_DIM, NSA_TK), 2)
    win_t = jnp.transpose(state_nsa_win[0], (0, 2, 3, 4, 1)).reshape(dec_batch, 256, WINDOW)
    new_win_t = jnp.transpose(s_kvwin.reshape(dec_batch, dec_seq, 256), (0, 2, 1))
    band_t = jnp.concatenate([win_t, new_win_t], axis=2)
    band_pad = jnp.pad(band_t, ((0, 0), (0, 0), (0, WIN_BAND - band_t.shape[2]))).astype(BF16)
    ob_s = _nsa_attention(s_qb.reshape(dec_batch, dec_seq, 512), kct_s, vct_s, kst_s, vst_s,
                          (kst_tail, vst_tail), band_pad[:, 0:128],
                          _with_ones(band_pad[:, 128:256].reshape(dec_batch, NSA_G, HEAD_DIM, WIN_BAND), 2),
                          s_gb.reshape(dec_batch, dec_seq, 128), tables_s, dec_seq, tile_base)

    g1 = g_norm[0, 1].reshape(1, D_MODEL)
    wf, wn, wo = w_out_fox[0].astype(BF16), w_out_nsa[0].astype(BF16), w_out[0].astype(BF16)
    wu, wd = w_up[0].astype(BF16), w_down[0].astype(BF16)
    y_p = _out_mlp(xp, oa_p, ob_p, p_gm, mod_p[2:6], g1, wf, wn, wo, wu, wd, 512)
    y_s = _out_mlp(xs, oa_s.reshape(dec_batch * dec_seq, 512), ob_s.reshape(dec_batch * dec_seq, 512),
                   s_gm, mod_s[2:6], g1, wf, wn, wo, wu, wd, dec_batch * dec_seq)

    win_keep = min(WINDOW, seq)
    new_win_sample = jnp.transpose(band_t[:, :, dec_seq:].reshape(dec_batch, 2, NSA_G, HEAD_DIM, WINDOW),
                                   (0, 4, 1, 2, 3))[None]
    return (
        y_p.reshape(1, seq, D_MODEL),
        y_s.reshape(dec_batch, dec_seq, D_MODEL),
        p_foxkv.reshape(1, 1, seq, 2, FOX_HEADS, HEAD_DIM),
        s_foxkv.reshape(1, dec_batch, dec_seq, 2, FOX_HEADS, HEAD_DIM),
        jnp.transpose(p_lft).reshape(1, 1, seq, FOX_HEADS),
        jnp.transpose(s_lft).reshape(1, dec_batch, dec_seq, FOX_HEADS),
        p_nsakv.reshape(1, 1, seq, 4, NSA_G, HEAD_DIM),
        s_nsakv.reshape(1, dec_batch, dec_seq, 4, NSA_G, HEAD_DIM),
        p_kvwin[seq - win_keep:].reshape(1, 1, win_keep, 2, NSA_G, HEAD_DIM),
        new_win_sample,
    )
```

```python
import functools
import math

import numpy as np
import jax
import jax.numpy as jnp
from jax import lax
from jax.experimental import pallas as pl
from jax.experimental.pallas import tpu as pltpu

F32 = jnp.float32
BF16 = jnp.bfloat16

D_MODEL = 1024
HEAD_DIM = 64
FOX_HEADS = 8
NSA_HEADS = 8
NSA_G = 2
NSA_HPG = 4
CMP_BLOCK = 32
CMP_STRIDE = 16
CMP_HIDDEN = 128
SEL_BLOCK = 64
N_SEL = 16
N_FORCED = 3
WINDOW = 512
REL_BUCKETS = 32
REL_MAX_DIST = 128
PAGE = 128
EPS = 1e-6
D_FF = 4 * D_MODEL

LANE = 128
LOG2E = 1.4426950408889634
QSCALE = LOG2E / math.sqrt(HEAD_DIM)
NEG = -float(2 ** 30)
M_INIT = -1e30
VMEM_LIMIT = 58 * 1024 * 1024

NSA_TQ = 256
NSA_TK = 256
NSA_FAR = 1024
WIN_BAND = WINDOW + NSA_TQ
SEL_NEAR = 2 * NSA_TK
CMP_NEAR = 32
SEL_SLAB = LANE * SEL_BLOCK
FOX_TQ = 512
FOX_TK = 512
FOX_FAR = 1024
FOX_STRIPS = 2
PREP_ROWS = 2048
FOX_DEC_PAGES = 16
NSA_DEC_PAGES = PREP_ROWS // PAGE


ATTN_FLAGS = None


def _cparams(sem, flags=None):
    return pltpu.CompilerParams(dimension_semantics=sem, vmem_limit_bytes=VMEM_LIMIT, flags=flags)


def _split2(x):
    hi = x.astype(BF16)
    lo = (x - hi.astype(F32)).astype(BF16)
    return hi, lo


def _split3(x):
    hi = x.astype(BF16)
    r = x - hi.astype(F32)
    mid = r.astype(BF16)
    lo = (r - mid.astype(F32)).astype(BF16)
    return hi, mid, lo


def _dot(a, b):
    return jnp.dot(a, b, preferred_element_type=F32)


def _dot_nt(a, b):
    return lax.dot_general(a, b, (((1,), (1,)), ((), ())), preferred_element_type=F32)


def _ada_kernel(c_ref, w_ref, b_ref, o_ref):
    c = c_ref[...]
    s = c * jax.nn.sigmoid(c)
    o_ref[...] = jnp.dot(s, w_ref[...], precision=lax.Precision.HIGHEST,
                         preferred_element_type=F32) + b_ref[...]


def _ada(c_all, w_ada, b_ada):
    rows = c_all.shape[0]
    n = w_ada.shape[1]
    tn = 1536
    return pl.pallas_call(
        _ada_kernel,
        grid=(n // tn,),
        in_specs=[pl.BlockSpec((rows, D_MODEL), lambda j: (0, 0)),
                  pl.BlockSpec((D_MODEL, tn), lambda j: (0, j)),
                  pl.BlockSpec((1, tn), lambda j: (0, j))],
        out_specs=pl.BlockSpec((rows, tn), lambda j: (0, j)),
        out_shape=jax.ShapeDtypeStruct((rows, n), F32),
        compiler_params=_cparams(("arbitrary",)),
        name="ada",
    )(c_all, w_ada, b_ada.reshape(1, n))


_C_QA, _C_KA, _C_VA, _C_QB = 0, 512, 1024, 1536
_C_KV = 2048
_C_F = 2816
_C_G = 2944
_C_M = 3072
_C_END = 5120


def _inproj_kernel(x_ref, sh_ref, sc_ref, g0_ref, w_ref, bd_ref, gq_ref, bf_ref,
                   foxkv_ref, nsakv_ref, kvwin_ref, lft_ref, qa_ref, kat_ref, va_ref,
                   qb_ref, kwt_ref, vwt_ref, gb_ref, gm_ref):
    x = x_ref[...]
    ms = jnp.mean(x * x, axis=-1, keepdims=True)
    y = x * lax.rsqrt(ms + EPS) * g0_ref[...]
    h = y * (1.0 + sc_ref[...]) + sh_ref[...]
    hb = h.astype(BF16)
    bd = bd_ref[...]

    def head_norm(z, gain):
        zz = (z * z).astype(BF16)
        w = z.shape[1]
        if w == LANE:
            msq = _dot(zz, bd[:LANE, :LANE])
        else:
            msq = jnp.concatenate([_dot(zz[:, k:k + 256], bd) for k in range(0, w, 256)], axis=1)
        return z * lax.rsqrt(msq + EPS) * gain

    za = _dot(hb, w_ref[:, _C_QA:_C_KV])
    qa = head_norm(za[:, _C_QA:_C_KA], gq_ref[:, 0:512])
    ka = head_norm(za[:, _C_KA:_C_VA], gq_ref[:, 512:1024])
    va = za[:, _C_VA:_C_QB]
    qb = head_norm(za[:, _C_QB:_C_KV], gq_ref[:, 1024:1536])
    foxkv_ref[:, 0:512] = ka
    foxkv_ref[:, 512:1024] = va
    qa_ref[...] = (qa * QSCALE).astype(BF16)
    kat_ref[...] = ka.T.astype(BF16)
    va_ref[...] = va.astype(BF16)
    qb_ref[...] = (qb * QSCALE).astype(BF16)

    zb = _dot(hb, w_ref[:, _C_KV:_C_M])
    ksel = head_norm(zb[:, 256:384], gq_ref[:, 1536:1664])
    kwin = head_norm(zb[:, 512:640], gq_ref[:, 1664:1792])
    vwin = zb[:, 640:768]
    nsakv_ref[:, 0:256] = zb[:, 0:256]
    nsakv_ref[:, 256:384] = ksel
    nsakv_ref[:, 384:512] = zb[:, 384:512]
    kvwin_ref[:, 0:128] = kwin
    kvwin_ref[:, 128:256] = vwin
    kwt_ref[...] = kwin.T.astype(BF16)
    vwt_ref[...] = vwin.T.astype(BF16)
    zf = zb[:, 768:896] + bf_ref[...]
    lf = jnp.minimum(zf, 0.0) - jnp.log(1.0 + jnp.exp(-jnp.abs(zf)))
    lft_ref[...] = lf.T[0:8, :]
    gb_ref[...] = jax.nn.sigmoid(zb[:, 896:1024])

    zm = _dot(hb, w_ref[:, _C_M:_C_END])
    gm_ref[...] = jax.nn.sigmoid(zm)


def _inproj(x, shift, scale, g0, w_cat, bd, gq, bf_pad):
    rows = x.shape[0]
    tr = 256
    mrows = shift.shape[0]
    mod_spec = (pl.BlockSpec((1, D_MODEL), lambda i: (0, 0)) if mrows == 1
                else pl.BlockSpec((tr, D_MODEL), lambda i: (i, 0)))
    const = lambda shape: pl.BlockSpec(shape, lambda i: tuple(0 for _ in shape))
    row_spec = lambda w: pl.BlockSpec((tr, w), lambda i: (i, 0))
    col_spec = lambda h: pl.BlockSpec((h, tr), lambda i: (0, i))
    out_shape = (
        jax.ShapeDtypeStruct((rows, 1024), F32),
        jax.ShapeDtypeStruct((rows, 512), F32),
        jax.ShapeDtypeStruct((rows, 256), F32),
        jax.ShapeDtypeStruct((8, rows), F32),
        jax.ShapeDtypeStruct((rows, 512), BF16),
        jax.ShapeDtypeStruct((512, rows), BF16),
        jax.ShapeDtypeStruct((rows, 512), BF16),
        jax.ShapeDtypeStruct((rows, 512), BF16),
        jax.ShapeDtypeStruct((128, rows), BF16),
        jax.ShapeDtypeStruct((128, rows), BF16),
        jax.ShapeDtypeStruct((rows, 128), F32),
        jax.ShapeDtypeStruct((rows, 2048), F32),
    )
    out_specs = (row_spec(1024), row_spec(512), row_spec(256), col_spec(8), row_spec(512),
                 col_spec(512), row_spec(512), row_spec(512), col_spec(128), col_spec(128),
                 row_spec(128), row_spec(2048))
    return pl.pallas_call(
        _inproj_kernel,
        grid=(rows // tr,),
        in_specs=[row_spec(D_MODEL), mod_spec, mod_spec, const((1, D_MODEL)),
                  const((D_MODEL, _C_END)), const((256, 256)), const((1, 1792)), const((1, 128))],
        out_specs=out_specs,
        out_shape=out_shape,
        compiler_params=_cparams(("arbitrary",)),
        name="inproj",
    )(x, shift, scale, g0, w_cat, bd, gq, bf_pad)


def _cumsum_kernel(x_ref, u_ref, pick_ref, bt_ref, o_ref):
    u = u_ref[...]
    c2 = sum(_dot(p, u) for p in _split3(x_ref[...]))
    pick = pick_ref[...]
    totb = sum(_dot(p, pick) for p in _split3(c2))
    bt = bt_ref[...]
    offs = sum(_dot(bt, p) for p in _split3(totb))
    o_ref[...] = (c2 + offs) * LOG2E


def _cumsum(lft):
    heads, s = lft.shape
    nblk = s // LANE
    rows = heads * nblk
    x = lft.reshape(rows, LANE)
    u = jnp.asarray(np.triu(np.ones((LANE, LANE), np.float32)), BF16)
    pick = np.zeros((LANE, LANE), np.float32)
    pick[LANE - 1, :] = 1.0
    r = np.arange(rows)
    bt = ((r[:, None] // nblk == r[None, :] // nblk) & (r[None, :] < r[:, None])).astype(np.float32)
    full = lambda shape: pl.BlockSpec(shape, lambda i: (0, 0))
    out = pl.pallas_call(
        _cumsum_kernel,
        grid=(1,),
        in_specs=[full((rows, LANE)), full((LANE, LANE)), full((LANE, LANE)), full((rows, rows))],
        out_specs=full((rows, LANE)),
        out_shape=jax.ShapeDtypeStruct((rows, LANE), F32),
        compiler_params=_cparams(("arbitrary",)),
        name="cumsum",
    )(x, u, jnp.asarray(pick, BF16), jnp.asarray(bt, BF16))
    return out.reshape(heads, nblk, LANE)


def _online_update(t, v1, m_ref, acc_ref, v_transposed=False):
    tk = t.shape[1]
    m_prev = m_ref[...]
    m_new = jnp.maximum(m_prev, jnp.max(t, axis=1, keepdims=True))
    alpha = jnp.exp2(m_prev - m_new)
    p = jnp.exp2(t - jnp.tile(m_new, (1, tk // LANE))).astype(BF16)
    pv = _dot_nt(p, v1) if v_transposed else _dot(p, v1)
    acc_ref[...] = acc_ref[...] * alpha + pv
    m_ref[...] = m_new


def _normalize(acc):
    return acc[:, 0:HEAD_DIM] / acc[:, HEAD_DIM:2 * HEAD_DIM]


def _with_ones(v, axis):
    shape = list(v.shape)
    shape[axis] = HEAD_DIM
    return jnp.concatenate([v, jnp.ones(shape, v.dtype)], axis=axis)


def _fox_kernel(q_ref, kt_ref, v_ref, ck_ref, o_ref, m_ref, acc_ref):
    i = pl.program_id(1)
    rs = FOX_TQ // FOX_STRIPS
    chains = [(hh, st) for hh in range(2) for st in range(FOX_STRIPS)]
    m_ref[...] = jnp.full(m_ref.shape, M_INIT, F32)
    acc_ref[...] = jnp.zeros(acc_ref.shape, F32)

    def tile(koff, width, masked):
        blk0 = koff // LANE
        widths = [min(width, (st + 1) * rs) if masked else width for _, st in chains]
        nck = [-jnp.concatenate([ck_ref[hh, blk0 + c] for c in range(width // LANE)], axis=1) for hh in range(2)]
        scores = [_dot(q_ref[st * rs:(st + 1) * rs, 64 * hh:64 * hh + 64], kt_ref[64 * hh:64 * hh + 64, pl.ds(koff, w)])
                  + jnp.tile(nck[hh][:, 0:w], (rs // 8, 1))
                  for (hh, st), w in zip(chains, widths)]
        for (hh, st), w, t in zip(chains, widths, scores):
            if masked:
                row = st * rs + lax.broadcasted_iota(jnp.int32, (rs, w), 0)
                col = lax.broadcasted_iota(jnp.int32, (rs, w), 1)
                t = jnp.where(row >= col, t, NEG)
            rows = pl.ds(st * rs, rs)
            _online_update(t, v_ref[pl.ds(koff, w), LANE * hh:LANE * (hh + 1)], m_ref.at[hh, rows],
                           acc_ref.at[hh, rows])

    def body(j, carry):
        tile(pl.multiple_of(j * FOX_FAR, FOX_FAR), FOX_FAR, False)
        return carry

    lax.fori_loop(0, i // 2, body, 0)

    @pl.when(i % 2 == 1)
    def _():
        tile(pl.multiple_of((i - 1) * FOX_TK, FOX_TK), FOX_TK, False)

    tile(pl.multiple_of(i * FOX_TK, FOX_TK), FOX_TK, True)
    o_ref[...] = jnp.concatenate([_normalize(acc_ref[hh]) for hh in range(2)], axis=1)


def _fox_prompt(qa, kat, va1, ck_rep):
    s = qa.shape[0]
    nblk = s // LANE
    return pl.pallas_call(
        _fox_kernel,
        grid=(FOX_HEADS // 2, s // FOX_TQ),
        in_specs=[pl.BlockSpec((FOX_TQ, LANE), lambda hp, i: (i, hp)),
                  pl.BlockSpec((LANE, s), lambda hp, i: (hp, 0)),
                  pl.BlockSpec((s, 2 * LANE), lambda hp, i: (0, hp)),
                  pl.BlockSpec((2, nblk, 8, LANE), lambda hp, i: (hp, 0, 0, 0))],
        out_specs=pl.BlockSpec((FOX_TQ, LANE), lambda hp, i: (i, hp)),
        out_shape=jax.ShapeDtypeStruct((s, 512), F32),
        scratch_shapes=[pltpu.VMEM((2, FOX_TQ, LANE), F32), pltpu.VMEM((2, FOX_TQ, LANE), F32)],
        compiler_params=_cparams(("arbitrary", "arbitrary"), ATTN_FLAGS),
        name="fox",
    )(qa, kat, va1, ck_rep)


def _gelu_tanh(x):
    return 0.5 * x * (1.0 + jnp.tanh(math.sqrt(2.0 / math.pi) * (x + 0.044715 * (x * x * x))))


def _nsaprep_kernel(*refs, n_parts, n_prefetch, feature_major):
    refs = refs[n_prefetch:]
    part_refs = refs[:n_parts]
    (look_ref, pe_ref, w1_ref, w2_ref, bd_ref, gkc_ref,
     kct_ref, vct_ref, kst_ref, vst_ref, xk_ref, xv_ref) = refs[n_parts:]
    xs_refs = (xk_ref, xv_ref)
    prow = PREP_ROWS // n_parts
    for k in range(n_parts):
        lo, hi = k * prow, (k + 1) * prow
        if feature_major:
            blk = part_refs[k][...].reshape(4, LANE, prow)
            xk_ref[lo:hi, :] = blk[0].T
            xv_ref[lo:hi, :] = blk[1].T
            kst_ref[0, :, lo:hi] = blk[2].astype(BF16)
            vsel_t = blk[3].astype(BF16)
        else:
            blk = part_refs[k][...]
            xk_ref[lo:hi, :] = blk[:, 0:128]
            xv_ref[lo:hi, :] = blk[:, 128:256]
            kst_ref[0, :, lo:hi] = blk[:, 256:384].T.astype(BF16)
            vsel_t = blk[:, 384:512].T.astype(BF16)
        for g in range(NSA_G):
            vst_ref[0, g, 0:HEAD_DIM, lo:hi] = vsel_t[HEAD_DIM * g:HEAD_DIM * (g + 1)]
            vst_ref[0, g, HEAD_DIM:2 * HEAD_DIM, lo:hi] = jnp.ones((HEAD_DIM, prow), BF16)
    if feature_major:
        look = look_ref[...].reshape(4, LANE, PAGE)
        xk_ref[PREP_ROWS:PREP_ROWS + CMP_STRIDE, :] = look[0].T[0:CMP_STRIDE, :]
        xv_ref[PREP_ROWS:PREP_ROWS + CMP_STRIDE, :] = look[1].T[0:CMP_STRIDE, :]
    else:
        look = look_ref[...]
        xk_ref[PREP_ROWS:PREP_ROWS + CMP_STRIDE, :] = look[:, 0:128]
        xv_ref[PREP_ROWS:PREP_ROWS + CMP_STRIDE, :] = look[:, 128:256]

    ntok = PREP_ROWS // CMP_STRIDE
    outs = []
    for kv in range(2):
        hid = jnp.zeros((ntok, NSA_G * CMP_HIDDEN), F32)
        for half in range(2):
            xcat = jnp.concatenate(
                [xs_refs[kv][pl.ds(half * CMP_STRIDE + r, ntok, stride=CMP_STRIDE), :]
                 for r in range(CMP_STRIDE)], axis=1)
            xcat = (xcat + pe_ref[kv, half]).astype(BF16)
            hid = hid + _dot(xcat, w1_ref[kv, half])
        act = _gelu_tanh(hid).astype(BF16)
        outs.append(_dot(act, w2_ref[kv]))
    kc, vc = outs
    msq = _dot((kc * kc).astype(BF16), bd_ref[...])
    kc = kc * lax.rsqrt(msq + EPS) * gkc_ref[...]
    kct_ref[0] = kc.T.astype(BF16)
    vct_ref[0] = vc.T.astype(BF16)


def _nsaprep(parts_arrays, part_specs, look_array, look_spec, grid, batch, length, prefetch,
             pe, w1, w2, bd128, gkc, feature_major):
    n_parts = len(part_specs)
    ntok = PREP_ROWS // CMP_STRIDE
    ncp = length // CMP_STRIDE
    npf = len(prefetch)

    def cm(shape):
        return pl.BlockSpec(shape, lambda b, t, *pf: tuple(0 for _ in shape))

    in_specs = list(part_specs) + [look_spec, cm(pe.shape), cm(w1.shape), cm(w2.shape),
                                   cm(bd128.shape), cm(gkc.shape)]
    out_specs = (
        pl.BlockSpec((1, 128, ntok), lambda b, t, *pf: (b, 0, t)),
        pl.BlockSpec((1, 128, ntok), lambda b, t, *pf: (b, 0, t)),
        pl.BlockSpec((1, 128, PREP_ROWS), lambda b, t, *pf: (b, 0, t)),
        pl.BlockSpec((1, NSA_G, 128, PREP_ROWS), lambda b, t, *pf: (b, 0, 0, t)),
    )
    out_shape = (
        jax.ShapeDtypeStruct((batch, 128, ncp), BF16),
        jax.ShapeDtypeStruct((batch, 128, ncp), BF16),
        jax.ShapeDtypeStruct((batch, 128, length), BF16),
        jax.ShapeDtypeStruct((batch, NSA_G, 128, length), BF16),
    )
    grid_spec = pltpu.PrefetchScalarGridSpec(
        num_scalar_prefetch=npf, grid=grid, in_specs=in_specs, out_specs=out_specs,
        scratch_shapes=[pltpu.VMEM((PREP_ROWS + CMP_STRIDE, LANE), F32),
                        pltpu.VMEM((PREP_ROWS + CMP_STRIDE, LANE), F32)])
    return pl.pallas_call(
        functools.partial(_nsaprep_kernel, n_parts=n_parts, n_prefetch=npf, feature_major=feature_major),
        grid_spec=grid_spec,
        out_shape=out_shape,
        compiler_params=_cparams(("arbitrary", "arbitrary")),
        name="nsaprep",
    )(*prefetch, *parts_arrays, look_array, pe, w1, w2, bd128, gkc)


def _nsa_kernel(*refs, tq, tile_base, single_shot, ncp, nbp):
    if single_shot:
        (q_ref, kct_ref, vct_ref, kst_ref, vst_ref, kstt_ref, vstt_ref, kwt_ref, vwt_ref, g_ref,
         tw_ref, ts_ref, tc_ref, emain_ref, etail_ref, wcs_ref, eg_ref, o_ref) = refs
    else:
        (q_ref, kct_ref, vct_ref, kst_ref, vst_ref, kwt_ref, vwt_ref, g_ref,
         tw_ref, ts_ref, tc_ref, etab_ref, wcs_ref, eg_ref,
         o_ref, sel_ref, m_ref, acc_ref) = refs
    rows = NSA_HPG * tq
    i_abs = tile_base + pl.program_id(2)
    q0 = i_abs * NSA_TQ
    q4 = q_ref[0]
    qhs = [q4[:, 64 * hh:64 * hh + 64] for hh in range(NSA_HPG)]
    qs = jnp.concatenate(qhs, axis=0)

    n_iota = lax.broadcasted_iota(jnp.int32, (1, ncp), 1)
    base = jnp.where(n_iota >= 16 * i_abs + 16, NEG, 0.0)
    pr = lax.broadcasted_iota(jnp.int32, (2 * CMP_NEAR, ncp), 0) % CMP_NEAR
    pn = lax.broadcasted_iota(jnp.int32, (2 * CMP_NEAR, ncp), 1)
    place = jnp.where(pn == 16 * i_abs - 16 + pr, 1.0, 0.0).astype(BF16)
    kct = kct_ref[0]
    vct = vct_ref[0]
    psum = jnp.zeros((tq, ncp), F32)
    o_cmp = []
    cmp_logits = [_dot(qhs[hh], kct) + (_dot(tc_ref[0, hh * tq:(hh + 1) * tq, :], place) + base)
                  for hh in range(NSA_HPG)]
    for hh in range(NSA_HPG):
        s = cmp_logits[hh]
        m = jnp.max(s, axis=1, keepdims=True)
        p = jnp.exp2(s - m)
        l = jnp.sum(p, axis=1, keepdims=True)
        inv = jnp.where(m > 0.5 * NEG, 1.0 / l, 0.0)
        pn_ = p * inv
        psum = psum + pn_
        o_cmp.append(_dot_nt(pn_.astype(BF16), vct))
    wcs = wcs_ref[...]

    ax = 1 if single_shot else 0
    if single_shot:
        imp = sum(_dot(part, wcs) for part in _split2(psum))
        posq = q0 + lax.broadcasted_iota(jnp.int32, (tq, 1), 0)
        blk = lax.broadcasted_iota(jnp.int32, (tq, nbp), 1)
    else:
        imp = sum(_dot_nt(wcs, part) for part in _split2(psum))
        posq = q0 + lax.broadcasted_iota(jnp.int32, (1, tq), 1)
        blk = lax.broadcasted_iota(jnp.int32, (nbp, tq), 0)
    blk_f = blk.astype(F32)
    cur = posq // SEL_BLOCK
    forced = (blk == 0) | (blk == cur) | (blk == cur - 1)
    sel = jnp.where(forced, 0.0, NEG)
    score = jnp.where(forced | (blk * SEL_BLOCK > posq), -jnp.inf, imp)
    for _ in range(N_SEL - N_FORCED):
        mx = jnp.max(score, axis=ax, keepdims=True)
        cand = jnp.where(score == mx, blk_f, float(4 * nbp))
        amin = jnp.min(cand, axis=ax, keepdims=True)
        hit = blk_f == amin
        sel = jnp.where(hit & (mx > -jnp.inf), 0.0, sel)
        score = jnp.where(hit, -jnp.inf, score)
    if not single_shot:
        sel = sel.T

    if single_shot:
        skv = kst_ref.shape[2]
        slab = min(SEL_SLAB, skv)
        selb = sel.astype(BF16)
        emain = emain_ref[...]
        mb = jnp.concatenate([_dot(selb[:, LANE * c:LANE * (c + 1)], emain) for c in range(skv // slab)], axis=1)
        ct = (skv // SEL_BLOCK) // LANE
        mb_tail = _dot(selb[:, LANE * ct:LANE * (ct + 1)], etail_ref[...])
        near = ts_ref[0]
        s_main = _dot(qs, kst_ref[0])
        t_main = (s_main.reshape(NSA_HPG, tq, skv) + mb[None]).reshape(rows, skv)
        t_main = jnp.concatenate([t_main[:, :skv - NSA_TK], t_main[:, skv - NSA_TK:] + near[:, 0:NSA_TK]], axis=1)
        s_tail = _dot(qs, kstt_ref[0])
        t_tail = (s_tail.reshape(NSA_HPG, tq, NSA_TK) + mb_tail[None]).reshape(rows, NSA_TK) + near[:, NSA_TK:SEL_NEAR]
        m = jnp.maximum(jnp.max(t_main, axis=1, keepdims=True), jnp.max(t_tail, axis=1, keepdims=True))
        p_main = jnp.exp2(t_main - m).astype(BF16)
        p_tail = jnp.exp2(t_tail - m).astype(BF16)
        o_sel = _normalize(_dot_nt(p_main, vst_ref[0, 0]) + _dot_nt(p_tail, vstt_ref[0, 0]))
    else:
        sel_ref[...] = sel
        m_ref[...] = jnp.full(m_ref.shape, M_INIT, F32)
        acc_ref[...] = jnp.zeros(acc_ref.shape, F32)

        def sel_tile(koff, width, near_tbl):
            kt = kst_ref[0, :, pl.ds(koff, width)]
            vt = vst_ref[0, 0, :, pl.ds(koff, width)]
            mbs = []
            for u in range(width // NSA_TK):
                j = koff // NSA_TK + u
                slab_off = pl.multiple_of((j // 32) * LANE, LANE)
                sel128 = sel_ref[:, pl.ds(slab_off, LANE)]
                mbs.append(_dot(sel128.astype(BF16), etab_ref[j % 32]))
            mb = mbs[0] if len(mbs) == 1 else jnp.concatenate(mbs, axis=1)
            logits = []
            for hh in range(NSA_HPG):
                add = mb if near_tbl is None else mb + near_tbl[hh * tq:(hh + 1) * tq]
                logits.append(_dot(qhs[hh], kt) + add)
            for hh in range(NSA_HPG):
                hrows = pl.ds(hh * tq, tq)
                _online_update(logits[hh], vt, m_ref.at[hrows], acc_ref.at[hrows], v_transposed=True)

        n_far_tiles = jnp.maximum(i_abs - 1, 0)
        n_big = n_far_tiles // (NSA_FAR // NSA_TK)
        rem = n_far_tiles % (NSA_FAR // NSA_TK)

        def far_body(j, carry):
            sel_tile(pl.multiple_of(j * NSA_FAR, NSA_FAR), NSA_FAR, None)
            return carry

        lax.fori_loop(0, n_big, far_body, 0)
        rem_off = n_big * NSA_FAR

        @pl.when(rem >= 2)
        def _():
            sel_tile(pl.multiple_of(rem_off, 2 * NSA_TK), 2 * NSA_TK, None)

        @pl.when(rem % 2 == 1)
        def _():
            sel_tile(pl.multiple_of(rem_off + (rem // 2) * 2 * NSA_TK, NSA_TK), NSA_TK, None)

        @pl.when(i_abs >= 1)
        def _():
            sel_tile(pl.multiple_of((i_abs - 1) * NSA_TK, NSA_TK), SEL_NEAR, ts_ref[0])

        @pl.when(i_abs == 0)
        def _():
            sel_tile(pl.multiple_of(i_abs * NSA_TK, NSA_TK), NSA_TK, ts_ref[0, :, NSA_TK:SEL_NEAR])
        o_sel = _normalize(acc_ref[...])

    band0 = pl.multiple_of((i_abs - tile_base if single_shot else i_abs) * NSA_TQ, NSA_TQ)
    ktw = kwt_ref[0, :, pl.ds(band0, WIN_BAND)]
    vtw = vwt_ref[0, 0, :, pl.ds(band0, WIN_BAND)]
    b_iota = lax.broadcasted_iota(jnp.int32, (1, WIN_BAND), 1)
    before_start = jnp.where(b_iota >= WINDOW - q0, 0.0, NEG)
    win_logits = [_dot(qhs[hh], ktw) + (tw_ref[0, hh * tq:(hh + 1) * tq, :] + before_start)
                  for hh in range(NSA_HPG)]
    o_win = []
    for hh in range(NSA_HPG):
        s = win_logits[hh]
        p = jnp.exp2(s - jnp.max(s, axis=1, keepdims=True)).astype(BF16)
        o_win.append(_normalize(_dot_nt(p, vtw)))

    def heads_to_lanes(o):
        return jnp.concatenate([o[hh * tq:(hh + 1) * tq] for hh in range(NSA_HPG)], axis=1)

    branches = (jnp.concatenate(o_cmp, axis=1), heads_to_lanes(o_sel), jnp.concatenate(o_win, axis=1))
    gparts = _split2(g_ref[0])
    out = jnp.zeros((tq, NSA_HPG * HEAD_DIM), F32)
    for br in range(3):
        gexp = sum(_dot(part, eg_ref[0, br]) for part in gparts)
        out = out + gexp * branches[br]
    o_ref[0] = out


def _nsa_attention(q, kct, vct, kst, vst, tail, kwt, vwt, gates, tables, tq, tile_base):
    batch, sq, _ = q.shape
    n_qt = sq // tq
    ncp = kct.shape[2]
    skv = kst.shape[2]
    lw = kwt.shape[2]
    rows = NSA_HPG * tq
    single_shot = tail is not None
    nbp = tables["wcs"].shape[1 if single_shot else 0]
    grp = lambda n: pl.BlockSpec((1, 64, n), lambda b, g, i: (b, g, 0))
    per_g = lambda w: pl.BlockSpec((1, rows, w), lambda b, g, i: (g, 0, 0))
    const = lambda a: pl.BlockSpec(a.shape, lambda b, g, i: tuple(0 for _ in a.shape))
    v1 = lambda n: pl.BlockSpec((1, 1, 2 * HEAD_DIM, n), lambda b, g, i: (b, g, 0, 0))
    in_specs = [pl.BlockSpec((1, tq, 256), lambda b, g, i: (b, i, g)), grp(ncp), grp(ncp), grp(skv), v1(skv)]
    args = [q, kct, vct, kst, vst]
    if single_shot:
        in_specs += [grp(NSA_TK), v1(NSA_TK)]
        args += list(tail)
    in_specs += [grp(lw), v1(lw), pl.BlockSpec((1, tq, 128), lambda b, g, i: (b, i, 0)),
                 per_g(WIN_BAND), per_g(SEL_NEAR), per_g(2 * CMP_NEAR)]
    args += [kwt, vwt, gates, tables["tw"], tables["ts"], tables["tc"]]
    if single_shot:
        in_specs += [const(tables["emain"]), const(tables["etail"])]
        args += [tables["emain"], tables["etail"]]
        scratch = []
    else:
        in_specs += [const(tables["etab"])]
        args += [tables["etab"]]
        scratch = [pltpu.VMEM((tq, nbp), F32), pltpu.VMEM((rows, LANE), F32), pltpu.VMEM((rows, LANE), F32)]
    in_specs += [const(tables["wcs"]), pl.BlockSpec((1, 3, 128, 256), lambda b, g, i: (g, 0, 0, 0))]
    args += [tables["wcs"], tables["eg"]]
    return pl.pallas_call(
        functools.partial(_nsa_kernel, tq=tq, tile_base=tile_base, single_shot=single_shot, ncp=ncp, nbp=nbp),
        grid=(batch, NSA_G, n_qt),
        in_specs=in_specs,
        out_specs=pl.BlockSpec((1, tq, 256), lambda b, g, i: (b, i, g)),
        out_shape=jax.ShapeDtypeStruct((batch, sq, 512), F32),
        scratch_shapes=scratch,
        compiler_params=_cparams(("arbitrary", "arbitrary", "arbitrary"), ATTN_FLAGS),
        name="nsa",
    )(*args)


def _t5_bucket(dist):
    d = jnp.maximum(dist, 0)
    exact = REL_BUCKETS // 2
    far = exact + (jnp.log(jnp.maximum(d, 1).astype(F32) / exact)
                   / math.log(REL_MAX_DIST / exact) * (REL_BUCKETS - exact)).astype(jnp.int32)
    return jnp.where(d < exact, d, jnp.minimum(far, REL_BUCKETS - 1))


def _nsa_tables(rel_bias, tq, ncp, nbp, skv=None):
    dmax = WINDOW + NSA_TQ + 1
    by_dist = (rel_bias[_t5_bucket(jnp.arange(dmax))] - rel_bias[REL_BUCKETS - 1][None, :]) * LOG2E

    def toeplitz(w, off, lo, hi):
        n = w + tq
        k = np.arange(n)
        k = np.where(k < w, k, k - n)
        d = off - k
        g = jnp.where(jnp.asarray((d >= lo) & (d <= hi))[:, None], by_dist[np.clip(d, 0, dmax - 1)], NEG)
        flat = jnp.tile(g.T, (1, tq))[:, :tq * (n - 1)]
        t = flat.reshape(NSA_HEADS, tq, n - 1)[:, :, :w]
        return t.reshape(NSA_G, NSA_HPG * tq, w)

    tw = toeplitz(WIN_BAND, WINDOW, 0, WINDOW)
    ts = toeplitz(SEL_NEAR, NSA_TK, 0, dmax)
    tc = toeplitz(CMP_STRIDE * CMP_NEAR, NSA_TQ - CMP_BLOCK + 1, 0, dmax)[:, :, ::CMP_STRIDE]
    tc_hi = tc.astype(BF16)
    tc_lo = (tc - tc_hi.astype(F32)).astype(BF16)
    tables = {"tw": tw, "ts": ts, "tc": jnp.concatenate([tc_hi, tc_lo], axis=2)}

    m = np.arange(LANE)[:, None]
    if skv is None:
        key = np.arange(NSA_TK)[None, None, :]
        r = np.arange(32)[:, None, None]
        tables["etab"] = jnp.asarray((m[None] == 4 * r + key // SEL_BLOCK).astype(np.float32), BF16)
    else:
        slab = min(SEL_SLAB, skv)
        tables["emain"] = jnp.asarray((m == np.arange(slab)[None, :] // SEL_BLOCK).astype(np.float32), BF16)
        o = (skv // SEL_BLOCK) % LANE
        tables["etail"] = jnp.asarray((m == o + np.arange(NSA_TK)[None, :] // SEL_BLOCK).astype(np.float32), BF16)

    c0 = np.arange(ncp)[:, None] * CMP_STRIDE
    s0 = np.arange(nbp)[None, :] * SEL_BLOCK
    shared = np.minimum(c0 + CMP_BLOCK, s0 + SEL_BLOCK) - np.maximum(c0, s0)
    wcs = np.maximum(shared, 0).astype(np.float32) / CMP_BLOCK
    tables["wcs"] = jnp.asarray(wcs.T if skv is None else wcs, BF16)

    eg = np.zeros((NSA_G, 3, 128, 256), np.float32)
    for g in range(NSA_G):
        for br in range(3):
            for hh in range(NSA_HPG):
                eg[g, br, 3 * (NSA_HPG * g + hh) + br, 64 * hh:64 * hh + 64] = 1.0
    tables["eg"] = jnp.asarray(eg, BF16)
    return tables


def _foxdec_kernel(pt_ref, qbd_ref, knew_ref, vnew_ref, lfnew_ref, u_ref, *refs):
    npg = FOX_DEC_PAGES
    kv_refs = refs[:npg]
    lf_refs = refs[npg:2 * npg]
    o_ref, m_ref, l_ref, acc_ref, carry_ref = refs[2 * npg:]
    c = pl.program_id(1)
    nrow = FOX_HEADS * 8

    @pl.when(c == 0)
    def _():
        m_ref[...] = jnp.full(m_ref.shape, M_INIT, F32)
        l_ref[...] = jnp.zeros(l_ref.shape, F32)
        acc_ref[...] = jnp.zeros(acc_ref.shape, F32)
        carry_ref[...] = jnp.zeros(carry_ref.shape, F32)

    qbd = qbd_ref[0]
    u = u_ref[...]

    def cum_blocks(lf_blocks):
        lf = lf_blocks[0] if len(lf_blocks) == 1 else jnp.concatenate(lf_blocks, axis=0)
        res = sum(_dot(part, u) for part in _split3(lf))
        run = carry_ref[...]
        out = []
        for k in range(len(lf_blocks)):
            out.append((res[8 * k:8 * k + 8, 0:LANE] + run) * LOG2E)
            run = run + res[8 * k:8 * k + 8, LANE:2 * LANE]
        carry_ref[...] = run
        return out

    def biased(s, ck):
        return (s.reshape(FOX_HEADS, 8, s.shape[1]) - ck[:, None, :]).reshape(nrow, s.shape[1])

    kt = jnp.concatenate([kv_refs[k][0, 0].reshape(512, PAGE) for k in range(npg)], axis=1).astype(BF16)
    vt = jnp.concatenate([kv_refs[k][0, 1].reshape(512, PAGE) for k in range(npg)], axis=1).astype(BF16)
    ck = jnp.concatenate(cum_blocks([lf_refs[k][0] for k in range(npg)]), axis=1)
    t = biased(_dot(qbd, kt), ck)
    tk = t.shape[1]
    m_prev = m_ref[...]
    m_new = jnp.maximum(m_prev, jnp.max(t, axis=1, keepdims=True))
    alpha = jnp.exp2(m_prev - m_new)
    p = jnp.exp2(t - jnp.tile(m_new, (1, tk // LANE)))
    l_ref[...] = alpha * l_ref[...] + jnp.sum(p, axis=1, keepdims=True)
    acc_ref[...] = acc_ref[...] * jnp.tile(alpha, (1, 4)) + _dot_nt(p.astype(BF16), vt)
    m_ref[...] = m_new

    @pl.when(c == pl.num_programs(1) - 1)
    def _():
        t2 = biased(_dot(qbd, knew_ref[0]), cum_blocks([lfnew_ref[0]])[0])
        row = lax.broadcasted_iota(jnp.int32, (nrow, LANE), 0) % 8
        col = lax.broadcasted_iota(jnp.int32, (nrow, LANE), 1)
        t2 = jnp.where(col <= row, t2, NEG)
        m_prev2 = m_ref[...]
        m_new2 = jnp.maximum(m_prev2, jnp.max(t2, axis=1, keepdims=True))
        alpha2 = jnp.exp2(m_prev2 - m_new2)
        p2 = jnp.exp2(t2 - m_new2)
        l2 = alpha2 * l_ref[...] + jnp.sum(p2, axis=1, keepdims=True)
        acc2 = acc_ref[...] * jnp.tile(alpha2, (1, 4)) + _dot_nt(p2.astype(BF16), vnew_ref[0])
        o = acc2 / jnp.tile(l2, (1, 4))
        for h in range(FOX_HEADS):
            o_ref[0, :, 64 * h:64 * h + 64] = o[8 * h:8 * h + 8, 64 * h:64 * h + 64]


def _fox_decode(page_table, qbd, knew_t, vnew_t, lfnew, kv_pool, lft_pool):
    batch, n_pages = page_table.shape
    npg = FOX_DEC_PAGES
    u = jnp.asarray(np.concatenate([np.triu(np.ones((LANE, LANE), np.float32)),
                                    np.ones((LANE, LANE), np.float32)], axis=1), BF16)

    def seq(shape):
        return pl.BlockSpec(shape, lambda b, c, pt: (b,) + tuple(0 for _ in shape[1:]))

    def page(shape, k):
        return pl.BlockSpec(shape, lambda b, c, pt, k=k: (pt[b, c * npg + k],) + tuple(0 for _ in shape[1:]))

    in_specs = ([seq((1, 64, 512)), seq((1, 512, LANE)), seq((1, 512, LANE)), seq((1, 8, LANE)),
                 pl.BlockSpec((LANE, 2 * LANE), lambda b, c, pt: (0, 0))]
                + [page((1, 2, FOX_HEADS, HEAD_DIM, PAGE), k) for k in range(npg)]
                + [page((1, 8, PAGE), k) for k in range(npg)])
    grid_spec = pltpu.PrefetchScalarGridSpec(
        num_scalar_prefetch=1, grid=(batch, n_pages // npg), in_specs=in_specs,
        out_specs=seq((1, 8, 512)),
        scratch_shapes=[pltpu.VMEM((64, LANE), F32), pltpu.VMEM((64, LANE), F32),
                        pltpu.VMEM((64, 512), F32), pltpu.VMEM((8, LANE), F32)])
    return pl.pallas_call(
        _foxdec_kernel,
        grid_spec=grid_spec,
        out_shape=jax.ShapeDtypeStruct((batch, 8, 512), F32),
        compiler_params=_cparams(("arbitrary", "arbitrary")),
        name="foxdec",
    )(page_table, qbd, knew_t, vnew_t, lfnew, u, *([kv_pool] * npg), *([lft_pool] * npg))


def _outmlp_kernel(x_ref, oa_ref, ob_ref, gm_ref, m2_ref, m3_ref, m4_ref, m5_ref, g1_ref,
                   wf_ref, wn_ref, wo_ref, wu_ref, wd_ref, y_ref, y1_ref, h2_ref, acc_ref):
    c = pl.program_id(1)

    @pl.when(c == 0)
    def _():
        ya = _dot(oa_ref[...].astype(BF16), wf_ref[...])
        yb = _dot(ob_ref[...].astype(BF16), wn_ref[...])
        u = gm_ref[:, 0:D_MODEL] * ya + gm_ref[:, D_MODEL:2 * D_MODEL] * yb
        att = _dot(u.astype(BF16), wo_ref[...])
        y1 = x_ref[...] + m2_ref[...] * att
        y1_ref[...] = y1
        ms = jnp.mean(y1 * y1, axis=-1, keepdims=True)
        hn = y1 * lax.rsqrt(ms + EPS) * g1_ref[...]
        h2_ref[...] = (hn * (1.0 + m4_ref[...]) + m3_ref[...]).astype(BF16)
        acc_ref[...] = jnp.zeros(acc_ref.shape, F32)

    up = _dot(h2_ref[...], wu_ref[...])
    act = jnp.square(jnp.maximum(up, 0.0)).astype(BF16)
    acc_ref[...] += _dot(act, wd_ref[...])

    @pl.when(c == pl.num_programs(1) - 1)
    def _():
        y_ref[...] = y1_ref[...] + m5_ref[...] * acc_ref[...]


def _out_mlp(x, oa, ob, gm, mods, g1, wf, wn, wo, wu, wd, tr):
    rows = x.shape[0]
    ffc = 1024
    mrows = mods[0].shape[0]
    mod_spec = (pl.BlockSpec((1, D_MODEL), lambda i, c: (0, 0)) if mrows == 1
                else pl.BlockSpec((tr, D_MODEL), lambda i, c: (i, 0)))
    row = lambda w: pl.BlockSpec((tr, w), lambda i, c: (i, 0))
    const = lambda shape: pl.BlockSpec(shape, lambda i, c: (0, 0))
    return pl.pallas_call(
        _outmlp_kernel,
        grid=(rows // tr, D_FF // ffc),
        in_specs=[row(D_MODEL), row(512), row(512), row(2048), mod_spec, mod_spec, mod_spec, mod_spec,
                  const((1, D_MODEL)), const((512, D_MODEL)), const((512, D_MODEL)),
                  const((D_MODEL, D_MODEL)),
                  pl.BlockSpec((D_MODEL, ffc), lambda i, c: (0, c)),
                  pl.BlockSpec((ffc, D_MODEL), lambda i, c: (c, 0))],
        out_specs=row(D_MODEL),
        out_shape=jax.ShapeDtypeStruct((rows, D_MODEL), F32),
        scratch_shapes=[pltpu.VMEM((tr, D_MODEL), F32), pltpu.VMEM((tr, D_MODEL), BF16),
                        pltpu.VMEM((tr, D_MODEL), F32)],
        compiler_params=_cparams(("arbitrary", "arbitrary")),
        name="outmlp",
    )(x, oa, ob, gm, *mods, g1, wf, wn, wo, wu, wd)


def _prep_w_in(w_in):
    qa, ka, va, zf, qb, zkv, zg, zm = jnp.split(
        w_in, np.cumsum([512, 512, 512, 8, 512, 768, 24, 2048])[:-1].tolist(), axis=1)
    pad = lambda w: jnp.pad(w, ((0, 0), (0, 128 - w.shape[1])))
    return jnp.concatenate([qa, ka, va, qb, zkv, pad(zf), pad(zg), zm], axis=1).astype(BF16)


def _prep_cmp_weights(pe_cmp, w_cmp1, w_cmp2):
    eye = jnp.eye(NSA_G, dtype=F32)
    w1 = w_cmp1.reshape(2, 2, CMP_STRIDE, HEAD_DIM, CMP_HIDDEN)
    w1big = jnp.einsum('khrdj,gq->khrgdqj', w1, eye).reshape(
        2, 2, CMP_STRIDE * NSA_G * HEAD_DIM, NSA_G * CMP_HIDDEN)
    w2big = jnp.einsum('kjd,gq->kgjqd', w_cmp2, eye).reshape(2, NSA_G * CMP_HIDDEN, NSA_G * HEAD_DIM)
    pe = pe_cmp.reshape(2, 2, CMP_STRIDE, 1, HEAD_DIM)
    pe = jnp.broadcast_to(pe, (2, 2, CMP_STRIDE, NSA_G, HEAD_DIM)).reshape(2, 2, 1, CMP_STRIDE * NSA_G * HEAD_DIM)
    return pe, w1big.astype(BF16), w2big.astype(BF16)


def _block_diag_mean(n):
    r = np.arange(n)
    return jnp.asarray((r[:, None] // HEAD_DIM == r[None, :] // HEAD_DIM).astype(np.float32) / HEAD_DIM, BF16)


def kernel(x_prompt, x_sample, cache_fox_kv, cache_fox_logf, cache_nsa_kv, state_nsa_win, page_table,
           c_prompt, c_sample, w_ada, b_ada, g_norm, w_in, b_forget, g_qk_fox, g_qk_nsa,
           pe_cmp, w_cmp1, w_cmp2, rel_bias, w_out_fox, w_out_nsa, w_out, w_up, w_down):
    n_batch, seq, _ = x_prompt.shape
    dec_batch, dec_seq, _ = x_sample.shape
    n_pages = page_table.shape[1]
    past = n_pages * PAGE
    assert n_batch == 1 and dec_seq == 8 and w_ada.shape[0] == 1
    assert seq % PREP_ROWS == 0 and past % PREP_ROWS == 0 and state_nsa_win.shape[2] == WINDOW

    n_c = 1 + dec_batch
    c_rows = -(-n_c // 8) * 8
    c_all = jnp.pad(jnp.concatenate([c_prompt, c_sample], axis=0), ((0, c_rows - n_c), (0, 0)))
    mods = _ada(c_all, w_ada[0], b_ada[0])
    mod_p = [mods[0:1, k * D_MODEL:(k + 1) * D_MODEL] for k in range(6)]
    mod_s = [jnp.repeat(mods[1:n_c, k * D_MODEL:(k + 1) * D_MODEL], dec_seq, axis=0) for k in range(6)]

    w_cat = _prep_w_in(w_in[0])
    bd256 = _block_diag_mean(256)
    tile8 = lambda g: jnp.tile(g, 8)
    gq = jnp.concatenate([tile8(g_qk_fox[0, 0]), tile8(g_qk_fox[0, 1]), tile8(g_qk_nsa[0, 0]),
                          jnp.tile(g_qk_nsa[0, 2], 2), jnp.tile(g_qk_nsa[0, 3], 2)]).reshape(1, 1792)
    bf_pad = jnp.pad(b_forget[0], (0, 120)).reshape(1, 128)
    g0 = g_norm[0, 0].reshape(1, D_MODEL)
    xp = x_prompt.reshape(seq, D_MODEL)
    xs = x_sample.reshape(dec_batch * dec_seq, D_MODEL)
    (p_foxkv, p_nsakv, p_kvwin, p_lft, p_qa, p_kat, p_va, p_qb, p_kwt, p_vwt, p_gb, p_gm) = _inproj(
        xp, mod_p[0], mod_p[1], g0, w_cat, bd256, gq, bf_pad)
    (s_foxkv, s_nsakv, s_kvwin, s_lft, s_qa, s_kat, s_va, s_qb, s_kwt, s_vwt, s_gb, s_gm) = _inproj(
        xs, mod_s[0], mod_s[1], g0, w_cat, bd256, gq, bf_pad)

    ck = _cumsum(p_lft)
    ck_rep = jnp.broadcast_to(ck[:, :, None, :], (FOX_HEADS, seq // LANE, 8, LANE))
    va1 = _with_ones(p_va.reshape(seq, FOX_HEADS, HEAD_DIM), 2).reshape(seq, FOX_HEADS * 2 * HEAD_DIM)
    oa_p = _fox_prompt(p_qa, p_kat, va1, ck_rep)

    pe, w1big, w2big = _prep_cmp_weights(pe_cmp[0], w_cmp1[0], w_cmp2[0])
    bd128 = _block_diag_mean(128)
    gkc = jnp.tile(g_qk_nsa[0, 1], 2).reshape(1, 128)
    look_blocks = PREP_ROWS // CMP_STRIDE
    last_look_p = seq // CMP_STRIDE - 1
    kct_p, vct_p, kst_p, vst_p = _nsaprep(
        [p_nsakv], [pl.BlockSpec((PREP_ROWS, 512), lambda b, t: (t, 0))],
        p_nsakv, pl.BlockSpec((CMP_STRIDE, 512), lambda b, t: (jnp.minimum((t + 1) * look_blocks, last_look_p), 0)),
        (1, seq // PREP_ROWS), 1, seq, (), pe, w1big, w2big, bd128, gkc, False)
    tables_p = _nsa_tables(rel_bias, NSA_TQ, seq // CMP_STRIDE, max(seq // SEL_BLOCK, LANE))
    kwt_pad = jnp.pad(p_kwt, ((0, 0), (WINDOW, 0)))[None]
    vwt_pad = jnp.pad(_with_ones(p_vwt.reshape(NSA_G, HEAD_DIM, seq), 1), ((0, 0), (0, 0), (WINDOW, 0)))[None]
    ob_p = _nsa_attention(p_qb[None], kct_p, vct_p, kst_p, vst_p, None, kwt_pad, vwt_pad, p_gb[None],
                          tables_p, NSA_TQ, 0)[0]

    kv_pool = jnp.transpose(cache_fox_kv[0], (0, 2, 3, 4, 1))
    lft_pool = jnp.transpose(cache_fox_logf[0], (0, 2, 1))
    eye_h = jnp.eye(FOX_HEADS, dtype=BF16)
    qa_s = s_qa.reshape(dec_batch, dec_seq, FOX_HEADS, HEAD_DIM)
    qbd = jnp.einsum('bjhd,hk->bhjkd', qa_s, eye_h).reshape(dec_batch, 64, 512)

    def new_cols(a_t):
        a = jnp.transpose(a_t.reshape(a_t.shape[0], dec_batch, dec_seq), (1, 0, 2))
        return jnp.pad(a, ((0, 0), (0, 0), (0, LANE - dec_seq)))

    knew_t = new_cols(s_kat)
    vnew_t = new_cols(jnp.transpose(s_va))
    lfnew = new_cols(s_lft)
    oa_s = _fox_decode(page_table, qbd, knew_t, vnew_t, lfnew, kv_pool, lft_pool)

    nsa_pool = jnp.transpose(cache_nsa_kv[0], (0, 2, 3, 4, 1))
    npg = NSA_DEC_PAGES
    page_shape = (1, 4, NSA_G, HEAD_DIM, PAGE)
    part_specs = [pl.BlockSpec(page_shape, lambda b, t, pt, k=k: (pt[b, t * npg + k], 0, 0, 0, 0))
                  for k in range(npg)]
    look_spec = pl.BlockSpec(
        page_shape, lambda b, t, pt: (pt[b, jnp.minimum((t + 1) * npg, n_pages - 1)], 0, 0, 0, 0))
    kct_s, vct_s, kst_s, vst_s = _nsaprep(
        [nsa_pool] * npg, part_specs, nsa_pool, look_spec,
        (dec_batch, past // PREP_ROWS), dec_batch, past, (page_table,), pe, w1big, w2big, bd128, gkc, True)
    tile_base = past // NSA_TQ
    nb_dec = past // SEL_BLOCK + NSA_TK // SEL_BLOCK
    tables_s = _nsa_tables(rel_bias, dec_seq, past // CMP_STRIDE, -(-nb_dec // LANE) * LANE, skv=past)
    new_nsa_t = jnp.transpose(s_nsakv.reshape(dec_batch, dec_seq, 512), (0, 2, 1))
    pad_tail = lambda a: jnp.pad(a, ((0, 0), (0, 0), (0, NSA_TK - dec_seq))).astype(BF16)
    kst_tail = pad_tail(new_nsa_t[:, 256:384])
    vst_tail = _with_ones(pad_tail(new_nsa_t[:, 384:512]).reshape(dec_batch, NSA_G, HEAD_DIM, NSA_TK), 2)
    win_t = jnp.transpose(state_nsa_win[0], (0, 2, 3, 4, 1)).reshape(dec_batch, 256, WINDOW)
    new_win_t = jnp.transpose(s_kvwin.reshape(dec_batch, dec_seq, 256), (0, 2, 1))
    band_t = jnp.concatenate([win_t, new_win_t], axis=2)
    band_pad = jnp.pad(band_t, ((0, 0), (0, 0), (0, WIN_BAND - band_t.shape[2]))).astype(BF16)
    ob_s = _nsa_attention(s_qb.reshape(dec_batch, dec_seq, 512), kct_s, vct_s, kst_s, vst_s,
                          (kst_tail, vst_tail), band_pad[:, 0:128],
                          _with_ones(band_pad[:, 128:256].reshape(dec_batch, NSA_G, HEAD_DIM, WIN_BAND), 2),
                          s_gb.reshape(dec_batch, dec_seq, 128), tables_s, dec_seq, tile_base)

    g1 = g_norm[0, 1].reshape(1, D_MODEL)
    wf, wn, wo = w_out_fox[0].astype(BF16), w_out_nsa[0].astype(BF16), w_out[0].astype(BF16)
    wu, wd = w_up[0].astype(BF16), w_down[0].astype(BF16)
    y_p = _out_mlp(xp, oa_p, ob_p, p_gm, mod_p[2:6], g1, wf, wn, wo, wu, wd, 512)
    y_s = _out_mlp(xs, oa_s.reshape(dec_batch * dec_seq, 512), ob_s.reshape(dec_batch * dec_seq, 512),
                   s_gm, mod_s[2:6], g1, wf, wn, wo, wu, wd, dec_batch * dec_seq)

    win_keep = min(WINDOW, seq)
    new_win_sample = jnp.transpose(band_t[:, :, dec_seq:].reshape(dec_batch, 2, NSA_G, HEAD_DIM, WINDOW),
                                   (0, 4, 1, 2, 3))[None]
    return (
        y_p.reshape(1, seq, D_MODEL),
        y_s.reshape(dec_batch, dec_seq, D_MODEL),
        p_foxkv.reshape(1, 1, seq, 2, FOX_HEADS, HEAD_DIM),
        s_foxkv.reshape(1, dec_batch, dec_seq, 2, FOX_HEADS, HEAD_DIM),
        jnp.transpose(p_lft).reshape(1, 1, seq, FOX_HEADS),
        jnp.transpose(s_lft).reshape(1, dec_batch, dec_seq, FOX_HEADS),
        p_nsakv.reshape(1, 1, seq, 4, NSA_G, HEAD_DIM),
        s_nsakv.reshape(1, dec_batch, dec_seq, 4, NSA_G, HEAD_DIM),
        p_kvwin[seq - win_keep:].reshape(1, 1, win_keep, 2, NSA_G, HEAD_DIM),
        new_win_sample,
    )
```

```python
import functools
import math

import numpy as np
import jax
import jax.numpy as jnp
from jax import lax
from jax.experimental import pallas as pl
from jax.experimental.pallas import tpu as pltpu

F32 = jnp.float32
BF16 = jnp.bfloat16

D_MODEL = 1024
HEAD_DIM = 64
FOX_HEADS = 8
NSA_HEADS = 8
NSA_G = 2
NSA_HPG = 4
CMP_BLOCK = 32
CMP_STRIDE = 16
CMP_HIDDEN = 128
SEL_BLOCK = 64
N_SEL = 16
N_FORCED = 3
WINDOW = 512
REL_BUCKETS = 32
REL_MAX_DIST = 128
PAGE = 128
EPS = 1e-6
D_FF = 4 * D_MODEL

LANE = 128
LOG2E = 1.4426950408889634
QSCALE = LOG2E / math.sqrt(HEAD_DIM)
NEG = -float(2 ** 30)
M_INIT = -1e30
VMEM_LIMIT = 58 * 1024 * 1024

NSA_TQ = 256
NSA_TK = 256
NSA_FAR = 1024
WIN_BAND = WINDOW + NSA_TQ
SEL_NEAR = 2 * NSA_TK
CMP_NEAR = 32
SEL_SLAB = LANE * SEL_BLOCK
FOX_TQ = 1024
FOX_FAR = 1024
FOX_STRIPS = 4
PREP_ROWS = 2048
FOX_DEC_PAGES = 16
NSA_DEC_PAGES = PREP_ROWS // PAGE


ATTN_FLAGS = None


def _cparams(sem, flags=None):
    return pltpu.CompilerParams(dimension_semantics=sem, vmem_limit_bytes=VMEM_LIMIT, flags=flags)


def _split2(x):
    hi = x.astype(BF16)
    lo = (x - hi.astype(F32)).astype(BF16)
    return hi, lo


def _split3(x):
    hi = x.astype(BF16)
    r = x - hi.astype(F32)
    mid = r.astype(BF16)
    lo = (r - mid.astype(F32)).astype(BF16)
    return hi, mid, lo


def _dot(a, b):
    return jnp.dot(a, b, preferred_element_type=F32)


def _dot_nt(a, b):
    return lax.dot_general(a, b, (((1,), (1,)), ((), ())), preferred_element_type=F32)


def _ada_kernel(c_ref, w_ref, b_ref, o_ref):
    c = c_ref[...]
    s = c * jax.nn.sigmoid(c)
    o_ref[...] = jnp.dot(s, w_ref[...], precision=lax.Precision.HIGHEST,
                         preferred_element_type=F32) + b_ref[...]


def _ada(c_all, w_ada, b_ada):
    rows = c_all.shape[0]
    n = w_ada.shape[1]
    tn = 1536
    return pl.pallas_call(
        _ada_kernel,
        grid=(n // tn,),
        in_specs=[pl.BlockSpec((rows, D_MODEL), lambda j: (0, 0)),
                  pl.BlockSpec((D_MODEL, tn), lambda j: (0, j)),
                  pl.BlockSpec((1, tn), lambda j: (0, j))],
        out_specs=pl.BlockSpec((rows, tn), lambda j: (0, j)),
        out_shape=jax.ShapeDtypeStruct((rows, n), F32),
        compiler_params=_cparams(("arbitrary",)),
        name="ada",
    )(c_all, w_ada, b_ada.reshape(1, n))


_C_QA, _C_KA, _C_VA, _C_QB = 0, 512, 1024, 1536
_C_KV = 2048
_C_F = 2816
_C_G = 2944
_C_M = 3072
_C_END = 5120


def _inproj_kernel(x_ref, sh_ref, sc_ref, g0_ref, w_ref, bd_ref, gq_ref, bf_ref,
                   foxkv_ref, nsakv_ref, kvwin_ref, lft_ref, qa_ref, kat_ref, va_ref,
                   qb_ref, kwt_ref, vwt_ref, gb_ref, gm_ref):
    x = x_ref[...]
    ms = jnp.mean(x * x, axis=-1, keepdims=True)
    y = x * lax.rsqrt(ms + EPS) * g0_ref[...]
    h = y * (1.0 + sc_ref[...]) + sh_ref[...]
    hb = h.astype(BF16)
    bd = bd_ref[...]

    def head_norm(z, gain):
        zz = (z * z).astype(BF16)
        w = z.shape[1]
        if w == LANE:
            msq = _dot(zz, bd[:LANE, :LANE])
        else:
            msq = jnp.concatenate([_dot(zz[:, k:k + 256], bd) for k in range(0, w, 256)], axis=1)
        return z * lax.rsqrt(msq + EPS) * gain

    za = _dot(hb, w_ref[:, _C_QA:_C_KV])
    qa = head_norm(za[:, _C_QA:_C_KA], gq_ref[:, 0:512])
    ka = head_norm(za[:, _C_KA:_C_VA], gq_ref[:, 512:1024])
    va = za[:, _C_VA:_C_QB]
    qb = head_norm(za[:, _C_QB:_C_KV], gq_ref[:, 1024:1536])
    foxkv_ref[:, 0:512] = ka
    foxkv_ref[:, 512:1024] = va
    qa_ref[...] = (qa * QSCALE).astype(BF16)
    kat_ref[...] = ka.T.astype(BF16)
    va_ref[...] = va.astype(BF16)
    qb_ref[...] = (qb * QSCALE).astype(BF16)

    zb = _dot(hb, w_ref[:, _C_KV:_C_M])
    ksel = head_norm(zb[:, 256:384], gq_ref[:, 1536:1664])
    kwin = head_norm(zb[:, 512:640], gq_ref[:, 1664:1792])
    vwin = zb[:, 640:768]
    nsakv_ref[:, 0:256] = zb[:, 0:256]
    nsakv_ref[:, 256:384] = ksel
    nsakv_ref[:, 384:512] = zb[:, 384:512]
    kvwin_ref[:, 0:128] = kwin
    kvwin_ref[:, 128:256] = vwin
    kwt_ref[...] = kwin.T.astype(BF16)
    vwt_ref[...] = vwin.T.astype(BF16)
    zf = zb[:, 768:896] + bf_ref[...]
    lf = jnp.minimum(zf, 0.0) - jnp.log(1.0 + jnp.exp(-jnp.abs(zf)))
    lft_ref[...] = lf.T[0:8, :]
    gb_ref[...] = jax.nn.sigmoid(zb[:, 896:1024])

    zm = _dot(hb, w_ref[:, _C_M:_C_END])
    gm_ref[...] = jax.nn.sigmoid(zm)


def _inproj(x, shift, scale, g0, w_cat, bd, gq, bf_pad):
    rows = x.shape[0]
    tr = 256
    mrows = shift.shape[0]
    mod_spec = (pl.BlockSpec((1, D_MODEL), lambda i: (0, 0)) if mrows == 1
                else pl.BlockSpec((tr, D_MODEL), lambda i: (i, 0)))
    const = lambda shape: pl.BlockSpec(shape, lambda i: tuple(0 for _ in shape))
    row_spec = lambda w: pl.BlockSpec((tr, w), lambda i: (i, 0))
    col_spec = lambda h: pl.BlockSpec((h, tr), lambda i: (0, i))
    out_shape = (
        jax.ShapeDtypeStruct((rows, 1024), F32),
        jax.ShapeDtypeStruct((rows, 512), F32),
        jax.ShapeDtypeStruct((rows, 256), F32),
        jax.ShapeDtypeStruct((8, rows), F32),
        jax.ShapeDtypeStruct((rows, 512), BF16),
        jax.ShapeDtypeStruct((512, rows), BF16),
        jax.ShapeDtypeStruct((rows, 512), BF16),
        jax.ShapeDtypeStruct((rows, 512), BF16),
        jax.ShapeDtypeStruct((128, rows), BF16),
        jax.ShapeDtypeStruct((128, rows), BF16),
        jax.ShapeDtypeStruct((rows, 128), F32),
        jax.ShapeDtypeStruct((rows, 2048), F32),
    )
    out_specs = (row_spec(1024), row_spec(512), row_spec(256), col_spec(8), row_spec(512),
                 col_spec(512), row_spec(512), row_spec(512), col_spec(128), col_spec(128),
                 row_spec(128), row_spec(2048))
    return pl.pallas_call(
        _inproj_kernel,
        grid=(rows // tr,),
        in_specs=[row_spec(D_MODEL), mod_spec, mod_spec, const((1, D_MODEL)),
                  const((D_MODEL, _C_END)), const((256, 256)), const((1, 1792)), const((1, 128))],
        out_specs=out_specs,
        out_shape=out_shape,
        compiler_params=_cparams(("arbitrary",)),
        name="inproj",
    )(x, shift, scale, g0, w_cat, bd, gq, bf_pad)


def _cumsum_kernel(x_ref, u_ref, pick_ref, bt_ref, o_ref):
    u = u_ref[...]
    c2 = sum(_dot(p, u) for p in _split3(x_ref[...]))
    pick = pick_ref[...]
    totb = sum(_dot(p, pick) for p in _split3(c2))
    bt = bt_ref[...]
    offs = sum(_dot(bt, p) for p in _split3(totb))
    o_ref[...] = (c2 + offs) * LOG2E


def _cumsum(lft):
    heads, s = lft.shape
    nblk = s // LANE
    rows = heads * nblk
    x = lft.reshape(rows, LANE)
    u = jnp.asarray(np.triu(np.ones((LANE, LANE), np.float32)), BF16)
    pick = np.zeros((LANE, LANE), np.float32)
    pick[LANE - 1, :] = 1.0
    r = np.arange(rows)
    bt = ((r[:, None] // nblk == r[None, :] // nblk) & (r[None, :] < r[:, None])).astype(np.float32)
    full = lambda shape: pl.BlockSpec(shape, lambda i: (0, 0))
    out = pl.pallas_call(
        _cumsum_kernel,
        grid=(1,),
        in_specs=[full((rows, LANE)), full((LANE, LANE)), full((LANE, LANE)), full((rows, rows))],
        out_specs=full((rows, LANE)),
        out_shape=jax.ShapeDtypeStruct((rows, LANE), F32),
        compiler_params=_cparams(("arbitrary",)),
        name="cumsum",
    )(x, u, jnp.asarray(pick, BF16), jnp.asarray(bt, BF16))
    return out.reshape(heads, nblk, LANE)


def _online_update(t, v1, m_ref, acc_ref, v_transposed=False):
    tk = t.shape[1]
    m_prev = m_ref[...]
    m_new = jnp.maximum(m_prev, jnp.max(t, axis=1, keepdims=True))
    alpha = jnp.exp2(m_prev - m_new)
    p = jnp.exp2(t - jnp.tile(m_new, (1, tk // LANE))).astype(BF16)
    pv = _dot_nt(p, v1) if v_transposed else _dot(p, v1)
    acc_ref[...] = acc_ref[...] * alpha + pv
    m_ref[...] = m_new


def _normalize(acc):
    return acc[:, 0:HEAD_DIM] / acc[:, HEAD_DIM:2 * HEAD_DIM]


def _with_ones(v, axis):
    shape = list(v.shape)
    shape[axis] = HEAD_DIM
    return jnp.concatenate([v, jnp.ones(shape, v.dtype)], axis=axis)


def _fox_kernel(q_ref, kt_ref, v_ref, ck_ref, o_ref, m_ref, acc_ref):
    i = pl.program_id(1)
    rs = FOX_TQ // FOX_STRIPS
    chains = [(hh, st) for hh in range(2) for st in range(FOX_STRIPS)]
    m_ref[...] = jnp.full(m_ref.shape, M_INIT, F32)
    acc_ref[...] = jnp.zeros(acc_ref.shape, F32)

    def tile(koff, width, masked):
        blk0 = koff // LANE
        widths = [min(width, (st + 1) * rs) if masked else width for _, st in chains]
        nck = [-jnp.concatenate([ck_ref[hh, blk0 + c] for c in range(width // LANE)], axis=1) for hh in range(2)]
        scores = [_dot(q_ref[st * rs:(st + 1) * rs, 64 * hh:64 * hh + 64], kt_ref[64 * hh:64 * hh + 64, pl.ds(koff, w)])
                  + jnp.tile(nck[hh][:, 0:w], (rs // 8, 1))
                  for (hh, st), w in zip(chains, widths)]
        for (hh, st), w, t in zip(chains, widths, scores):
            if masked:
                row = st * rs + lax.broadcasted_iota(jnp.int32, (rs, w), 0)
                col = lax.broadcasted_iota(jnp.int32, (rs, w), 1)
                t = jnp.where(row >= col, t, NEG)
            rows = pl.ds(st * rs, rs)
            _online_update(t, v_ref[pl.ds(koff, w), LANE * hh:LANE * (hh + 1)], m_ref.at[hh, rows],
                           acc_ref.at[hh, rows])

    def body(j, carry):
        tile(pl.multiple_of(j * FOX_FAR, FOX_FAR), FOX_FAR, False)
        return carry

    lax.fori_loop(0, i * (FOX_TQ // FOX_FAR), body, 0)
    tile(pl.multiple_of(i * FOX_TQ, FOX_TQ), FOX_TQ, True)
    o_ref[...] = jnp.concatenate([_normalize(acc_ref[hh]) for hh in range(2)], axis=1)


def _fox_prompt(qa, kat, va1, ck_rep):
    s = qa.shape[0]
    nblk = s // LANE
    return pl.pallas_call(
        _fox_kernel,
        grid=(FOX_HEADS // 2, s // FOX_TQ),
        in_specs=[pl.BlockSpec((FOX_TQ, LANE), lambda hp, i: (i, hp)),
                  pl.BlockSpec((LANE, s), lambda hp, i: (hp, 0)),
                  pl.BlockSpec((s, 2 * LANE), lambda hp, i: (0, hp)),
                  pl.BlockSpec((2, nblk, 8, LANE), lambda hp, i: (hp, 0, 0, 0))],
        out_specs=pl.BlockSpec((FOX_TQ, LANE), lambda hp, i: (i, hp)),
        out_shape=jax.ShapeDtypeStruct((s, 512), F32),
        scratch_shapes=[pltpu.VMEM((2, FOX_TQ, LANE), F32), pltpu.VMEM((2, FOX_TQ, LANE), F32)],
        compiler_params=_cparams(("arbitrary", "arbitrary"), ATTN_FLAGS),
        name="fox",
    )(qa, kat, va1, ck_rep)


def _gelu_tanh(x):
    return 0.5 * x * (1.0 + jnp.tanh(math.sqrt(2.0 / math.pi) * (x + 0.044715 * (x * x * x))))


def _nsaprep_kernel(*refs, n_parts, n_prefetch, feature_major):
    refs = refs[n_prefetch:]
    part_refs = refs[:n_parts]
    (look_ref, pe_ref, w1_ref, w2_ref, bd_ref, gkc_ref,
     kct_ref, vct_ref, kst_ref, vst_ref, xk_ref, xv_ref) = refs[n_parts:]
    xs_refs = (xk_ref, xv_ref)
    prow = PREP_ROWS // n_parts
    ntok = PREP_ROWS // CMP_STRIDE
    flat = CMP_STRIDE * LANE
    chunks = ([], [])
    for k in range(n_parts):
        lo, hi = k * prow, (k + 1) * prow
        if feature_major:
            blk = part_refs[k][...].reshape(4, LANE, prow)
            for kv in range(2):
                chunks[kv].append(blk[kv].T.reshape(prow // CMP_STRIDE, flat))
            kst_ref[0, :, lo:hi] = blk[2].astype(BF16)
            vsel_t = blk[3].astype(BF16)
        else:
            blk = part_refs[k][...]
            xk_ref[lo:hi, :] = blk[:, 0:128]
            xv_ref[lo:hi, :] = blk[:, 128:256]
            kst_ref[0, :, lo:hi] = blk[:, 256:384].T.astype(BF16)
            vsel_t = blk[:, 384:512].T.astype(BF16)
        for g in range(NSA_G):
            vst_ref[0, g, 0:HEAD_DIM, lo:hi] = vsel_t[HEAD_DIM * g:HEAD_DIM * (g + 1)]
            vst_ref[0, g, HEAD_DIM:2 * HEAD_DIM, lo:hi] = jnp.ones((HEAD_DIM, prow), BF16)
    outs = []
    if feature_major:
        look = look_ref[...].reshape(4, LANE, PAGE)
        for kv in range(2):
            chunks[kv].append(look[kv].T.reshape(PAGE // CMP_STRIDE, flat))
            xc = jnp.concatenate(chunks[kv], axis=0)
            first = _dot((xc[0:ntok] + pe_ref[kv, 0]).astype(BF16), w1_ref[kv, 0])
            second = _dot((xc + pe_ref[kv, 1]).astype(BF16), w1_ref[kv, 1])
            act = _gelu_tanh(first + second[1:ntok + 1]).astype(BF16)
            outs.append(_dot(act, w2_ref[kv]))
    else:
        look = look_ref[...]
        xk_ref[PREP_ROWS:PREP_ROWS + CMP_STRIDE, :] = look[:, 0:128]
        xv_ref[PREP_ROWS:PREP_ROWS + CMP_STRIDE, :] = look[:, 128:256]
        for kv in range(2):
            hid = jnp.zeros((ntok, NSA_G * CMP_HIDDEN), F32)
            for half in range(2):
                xcat = jnp.concatenate(
                    [xs_refs[kv][pl.ds(half * CMP_STRIDE + r, ntok, stride=CMP_STRIDE), :]
                     for r in range(CMP_STRIDE)], axis=1)
                xcat = (xcat + pe_ref[kv, half]).astype(BF16)
                hid = hid + _dot(xcat, w1_ref[kv, half])
            act = _gelu_tanh(hid).astype(BF16)
            outs.append(_dot(act, w2_ref[kv]))
    kc, vc = outs
    msq = _dot((kc * kc).astype(BF16), bd_ref[...])
    kc = kc * lax.rsqrt(msq + EPS) * gkc_ref[...]
    kct_ref[0] = kc.T.astype(BF16)
    vct_ref[0] = vc.T.astype(BF16)


def _nsaprep(parts_arrays, part_specs, look_array, look_spec, grid, batch, length, prefetch,
             pe, w1, w2, bd128, gkc, feature_major):
    n_parts = len(part_specs)
    ntok = PREP_ROWS // CMP_STRIDE
    ncp = length // CMP_STRIDE
    npf = len(prefetch)

    def cm(shape):
        return pl.BlockSpec(shape, lambda b, t, *pf: tuple(0 for _ in shape))

    in_specs = list(part_specs) + [look_spec, cm(pe.shape), cm(w1.shape), cm(w2.shape),
                                   cm(bd128.shape), cm(gkc.shape)]
    out_specs = (
        pl.BlockSpec((1, 128, ntok), lambda b, t, *pf: (b, 0, t)),
        pl.BlockSpec((1, 128, ntok), lambda b, t, *pf: (b, 0, t)),
        pl.BlockSpec((1, 128, PREP_ROWS), lambda b, t, *pf: (b, 0, t)),
        pl.BlockSpec((1, NSA_G, 128, PREP_ROWS), lambda b, t, *pf: (b, 0, 0, t)),
    )
    out_shape = (
        jax.ShapeDtypeStruct((batch, 128, ncp), BF16),
        jax.ShapeDtypeStruct((batch, 128, ncp), BF16),
        jax.ShapeDtypeStruct((batch, 128, length), BF16),
        jax.ShapeDtypeStruct((batch, NSA_G, 128, length), BF16),
    )
    grid_spec = pltpu.PrefetchScalarGridSpec(
        num_scalar_prefetch=npf, grid=grid, in_specs=in_specs, out_specs=out_specs,
        scratch_shapes=[pltpu.VMEM((PREP_ROWS + CMP_STRIDE, LANE), F32),
                        pltpu.VMEM((PREP_ROWS + CMP_STRIDE, LANE), F32)])
    return pl.pallas_call(
        functools.partial(_nsaprep_kernel, n_parts=n_parts, n_prefetch=npf, feature_major=feature_major),
        grid_spec=grid_spec,
        out_shape=out_shape,
        compiler_params=_cparams(("arbitrary", "arbitrary")),
        name="nsaprep",
    )(*prefetch, *parts_arrays, look_array, pe, w1, w2, bd128, gkc)


def _nsa_kernel(*refs, tq, tile_base, single_shot, ncp, nbp):
    if single_shot:
        (q_ref, kct_ref, vct_ref, kst_ref, vst_ref, kstt_ref, vstt_ref, kwt_ref, vwt_ref, g_ref,
         tw_ref, ts_ref, tc_ref, emain_ref, etail_ref, wcs_ref, eg_ref, o_ref) = refs
    else:
        (q_ref, kct_ref, vct_ref, kst_ref, vst_ref, kwt_ref, vwt_ref, g_ref,
         tw_ref, ts_ref, tc_ref, etab_ref, wcs_ref, eg_ref,
         o_ref, sel_ref, m_ref, acc_ref) = refs
    rows = NSA_HPG * tq
    i_abs = tile_base + pl.program_id(2)
    q0 = i_abs * NSA_TQ
    q4 = q_ref[0]
    qhs = [q4[:, 64 * hh:64 * hh + 64] for hh in range(NSA_HPG)]
    qs = jnp.concatenate(qhs, axis=0)

    n_iota = lax.broadcasted_iota(jnp.int32, (1, ncp), 1)
    base = jnp.where(n_iota >= 16 * i_abs + 16, NEG, 0.0)
    pr = lax.broadcasted_iota(jnp.int32, (2 * CMP_NEAR, ncp), 0) % CMP_NEAR
    pn = lax.broadcasted_iota(jnp.int32, (2 * CMP_NEAR, ncp), 1)
    place = jnp.where(pn == 16 * i_abs - 16 + pr, 1.0, 0.0).astype(BF16)
    kct = kct_ref[0]
    vct = vct_ref[0]
    psum = jnp.zeros((tq, ncp), F32)
    o_cmp = []
    cmp_logits = [_dot(qhs[hh], kct) + (_dot(tc_ref[0, hh * tq:(hh + 1) * tq, :], place) + base)
                  for hh in range(NSA_HPG)]
    for hh in range(NSA_HPG):
        s = cmp_logits[hh]
        m = jnp.max(s, axis=1, keepdims=True)
        p = jnp.exp2(s - m)
        l = jnp.sum(p, axis=1, keepdims=True)
        inv = jnp.where(m > 0.5 * NEG, 1.0 / l, 0.0)
        pn_ = p * inv
        psum = psum + pn_
        o_cmp.append(_dot_nt(pn_.astype(BF16), vct))
    wcs = wcs_ref[...]

    ax = 1 if single_shot else 0
    if single_shot:
        imp = sum(_dot(part, wcs) for part in _split2(psum))
        posq = q0 + lax.broadcasted_iota(jnp.int32, (tq, 1), 0)
        blk = lax.broadcasted_iota(jnp.int32, (tq, nbp), 1)
    else:
        imp = sum(_dot_nt(wcs, part) for part in _split2(psum))
        posq = q0 + lax.broadcasted_iota(jnp.int32, (1, tq), 1)
        blk = lax.broadcasted_iota(jnp.int32, (nbp, tq), 0)
    blk_f = blk.astype(F32)
    cur = posq // SEL_BLOCK
    forced = (blk == 0) | (blk == cur) | (blk == cur - 1)
    sel = jnp.where(forced, 0.0, NEG)
    score = jnp.where(forced | (blk * SEL_BLOCK > posq), -jnp.inf, imp)
    for _ in range(N_SEL - N_FORCED):
        mx = jnp.max(score, axis=ax, keepdims=True)
        cand = jnp.where(score == mx, blk_f, float(4 * nbp))
        amin = jnp.min(cand, axis=ax, keepdims=True)
        hit = blk_f == amin
        sel = jnp.where(hit & (mx > -jnp.inf), 0.0, sel)
        score = jnp.where(hit, -jnp.inf, score)
    if not single_shot:
        sel = sel.T

    if single_shot:
        skv = kst_ref.shape[2]
        slab = min(SEL_SLAB, skv)
        selb = sel.astype(BF16)
        emain = emain_ref[...]
        mb = jnp.concatenate([_dot(selb[:, LANE * c:LANE * (c + 1)], emain) for c in range(skv // slab)], axis=1)
        ct = (skv // SEL_BLOCK) // LANE
        mb_tail = _dot(selb[:, LANE * ct:LANE * (ct + 1)], etail_ref[...])
        near = ts_ref[0]
        s_main = _dot(qs, kst_ref[0])
        t_main = (s_main.reshape(NSA_HPG, tq, skv) + mb[None]).reshape(rows, skv)
        t_main = jnp.concatenate([t_main[:, :skv - NSA_TK], t_main[:, skv - NSA_TK:] + near[:, 0:NSA_TK]], axis=1)
        s_tail = _dot(qs, kstt_ref[0])
        t_tail = (s_tail.reshape(NSA_HPG, tq, NSA_TK) + mb_tail[None]).reshape(rows, NSA_TK) + near[:, NSA_TK:SEL_NEAR]
        m = jnp.maximum(jnp.max(t_main, axis=1, keepdims=True), jnp.max(t_tail, axis=1, keepdims=True))
        p_main = jnp.exp2(t_main - m).astype(BF16)
        p_tail = jnp.exp2(t_tail - m).astype(BF16)
        o_sel = _normalize(_dot_nt(p_main, vst_ref[0, 0]) + _dot_nt(p_tail, vstt_ref[0, 0]))
    else:
        sel_ref[...] = sel
        m_ref[...] = jnp.full(m_ref.shape, M_INIT, F32)
        acc_ref[...] = jnp.zeros(acc_ref.shape, F32)

        def sel_tile(koff, width, near_tbl):
            kt = kst_ref[0, :, pl.ds(koff, width)]
            vt = vst_ref[0, 0, :, pl.ds(koff, width)]
            mbs = []
            for u in range(width // NSA_TK):
                j = koff // NSA_TK + u
                slab_off = pl.multiple_of((j // 32) * LANE, LANE)
                sel128 = sel_ref[:, pl.ds(slab_off, LANE)]
                mbs.append(_dot(sel128.astype(BF16), etab_ref[j % 32]))
            mb = mbs[0] if len(mbs) == 1 else jnp.concatenate(mbs, axis=1)
            logits = []
            for hh in range(NSA_HPG):
                add = mb if near_tbl is None else mb + near_tbl[hh * tq:(hh + 1) * tq]
                logits.append(_dot(qhs[hh], kt) + add)
            for hh in range(NSA_HPG):
                hrows = pl.ds(hh * tq, tq)
                _online_update(logits[hh], vt, m_ref.at[hrows], acc_ref.at[hrows], v_transposed=True)

        n_far_tiles = jnp.maximum(i_abs - 1, 0)
        n_big = n_far_tiles // (NSA_FAR // NSA_TK)
        rem = n_far_tiles % (NSA_FAR // NSA_TK)

        def far_body(j, carry):
            sel_tile(pl.multiple_of(j * NSA_FAR, NSA_FAR), NSA_FAR, None)
            return carry

        lax.fori_loop(0, n_big, far_body, 0)
        rem_off = n_big * NSA_FAR

        @pl.when(rem >= 2)
        def _():
            sel_tile(pl.multiple_of(rem_off, 2 * NSA_TK), 2 * NSA_TK, None)

        @pl.when(rem % 2 == 1)
        def _():
            sel_tile(pl.multiple_of(rem_off + (rem // 2) * 2 * NSA_TK, NSA_TK), NSA_TK, None)

        @pl.when(i_abs >= 1)
        def _():
            sel_tile(pl.multiple_of((i_abs - 1) * NSA_TK, NSA_TK), SEL_NEAR, ts_ref[0])

        @pl.when(i_abs == 0)
        def _():
            sel_tile(pl.multiple_of(i_abs * NSA_TK, NSA_TK), NSA_TK, ts_ref[0, :, NSA_TK:SEL_NEAR])
        o_sel = _normalize(acc_ref[...])

    band0 = pl.multiple_of((i_abs - tile_base if single_shot else i_abs) * NSA_TQ, NSA_TQ)
    ktw = kwt_ref[0, :, pl.ds(band0, WIN_BAND)]
    vtw = vwt_ref[0, 0, :, pl.ds(band0, WIN_BAND)]
    b_iota = lax.broadcasted_iota(jnp.int32, (1, WIN_BAND), 1)
    before_start = jnp.where(b_iota >= WINDOW - q0, 0.0, NEG)
    win_logits = [_dot(qhs[hh], ktw) + (tw_ref[0, hh * tq:(hh + 1) * tq, :] + before_start)
                  for hh in range(NSA_HPG)]
    o_win = []
    for hh in range(NSA_HPG):
        s = win_logits[hh]
        p = jnp.exp2(s - jnp.max(s, axis=1, keepdims=True)).astype(BF16)
        o_win.append(_normalize(_dot_nt(p, vtw)))

    def heads_to_lanes(o):
        return jnp.concatenate([o[hh * tq:(hh + 1) * tq] for hh in range(NSA_HPG)], axis=1)

    branches = (jnp.concatenate(o_cmp, axis=1), heads_to_lanes(o_sel), jnp.concatenate(o_win, axis=1))
    gparts = _split2(g_ref[0])
    out = jnp.zeros((tq, NSA_HPG * HEAD_DIM), F32)
    for br in range(3):
        gexp = sum(_dot(part, eg_ref[0, br]) for part in gparts)
        out = out + gexp * branches[br]
    o_ref[0] = out


def _nsa_attention(q, kct, vct, kst, vst, tail, kwt, vwt, gates, tables, tq, tile_base):
    batch, sq, _ = q.shape
    n_qt = sq // tq
    ncp = kct.shape[2]
    skv = kst.shape[2]
    lw = kwt.shape[2]
    rows = NSA_HPG * tq
    single_shot = tail is not None
    nbp = tables["wcs"].shape[1 if single_shot else 0]
    grp = lambda n: pl.BlockSpec((1, 64, n), lambda b, g, i: (b, g, 0))
    per_g = lambda w: pl.BlockSpec((1, rows, w), lambda b, g, i: (g, 0, 0))
    const = lambda a: pl.BlockSpec(a.shape, lambda b, g, i: tuple(0 for _ in a.shape))
    v1 = lambda n: pl.BlockSpec((1, 1, 2 * HEAD_DIM, n), lambda b, g, i: (b, g, 0, 0))
    in_specs = [pl.BlockSpec((1, tq, 256), lambda b, g, i: (b, i, g)), grp(ncp), grp(ncp), grp(skv), v1(skv)]
    args = [q, kct, vct, kst, vst]
    if single_shot:
        in_specs += [grp(NSA_TK), v1(NSA_TK)]
        args += list(tail)
    in_specs += [grp(lw), v1(lw), pl.BlockSpec((1, tq, 128), lambda b, g, i: (b, i, 0)),
                 per_g(WIN_BAND), per_g(SEL_NEAR), per_g(2 * CMP_NEAR)]
    args += [kwt, vwt, gates, tables["tw"], tables["ts"], tables["tc"]]
    if single_shot:
        in_specs += [const(tables["emain"]), const(tables["etail"])]
        args += [tables["emain"], tables["etail"]]
        scratch = []
    else:
        in_specs += [const(tables["etab"])]
        args += [tables["etab"]]
        scratch = [pltpu.VMEM((tq, nbp), F32), pltpu.VMEM((rows, LANE), F32), pltpu.VMEM((rows, LANE), F32)]
    in_specs += [const(tables["wcs"]), pl.BlockSpec((1, 3, 128, 256), lambda b, g, i: (g, 0, 0, 0))]
    args += [tables["wcs"], tables["eg"]]
    return pl.pallas_call(
        functools.partial(_nsa_kernel, tq=tq, tile_base=tile_base, single_shot=single_shot, ncp=ncp, nbp=nbp),
        grid=(batch, NSA_G, n_qt),
        in_specs=in_specs,
        out_specs=pl.BlockSpec((1, tq, 256), lambda b, g, i: (b, i, g)),
        out_shape=jax.ShapeDtypeStruct((batch, sq, 512), F32),
        scratch_shapes=scratch,
        compiler_params=_cparams(("arbitrary", "arbitrary", "arbitrary"), ATTN_FLAGS),
        name="nsa",
    )(*args)


def _t5_bucket(dist):
    d = jnp.maximum(dist, 0)
    exact = REL_BUCKETS // 2
    far = exact + (jnp.log(jnp.maximum(d, 1).astype(F32) / exact)
                   / math.log(REL_MAX_DIST / exact) * (REL_BUCKETS - exact)).astype(jnp.int32)
    return jnp.where(d < exact, d, jnp.minimum(far, REL_BUCKETS - 1))


def _nsa_tables(rel_bias, tq, ncp, nbp, skv=None):
    dmax = WINDOW + NSA_TQ + 1
    by_dist = (rel_bias[_t5_bucket(jnp.arange(dmax))] - rel_bias[REL_BUCKETS - 1][None, :]) * LOG2E

    def toeplitz(w, off, lo, hi):
        n = w + tq
        k = np.arange(n)
        k = np.where(k < w, k, k - n)
        d = off - k
        g = jnp.where(jnp.asarray((d >= lo) & (d <= hi))[:, None], by_dist[np.clip(d, 0, dmax - 1)], NEG)
        flat = jnp.tile(g.T, (1, tq))[:, :tq * (n - 1)]
        t = flat.reshape(NSA_HEADS, tq, n - 1)[:, :, :w]
        return t.reshape(NSA_G, NSA_HPG * tq, w)

    tw = toeplitz(WIN_BAND, WINDOW, 0, WINDOW)
    ts = toeplitz(SEL_NEAR, NSA_TK, 0, dmax)
    tc = toeplitz(CMP_STRIDE * CMP_NEAR, NSA_TQ - CMP_BLOCK + 1, 0, dmax)[:, :, ::CMP_STRIDE]
    tc_hi = tc.astype(BF16)
    tc_lo = (tc - tc_hi.astype(F32)).astype(BF16)
    tables = {"tw": tw, "ts": ts, "tc": jnp.concatenate([tc_hi, tc_lo], axis=2)}

    m = np.arange(LANE)[:, None]
    if skv is None:
        key = np.arange(NSA_TK)[None, None, :]
        r = np.arange(32)[:, None, None]
        tables["etab"] = jnp.asarray((m[None] == 4 * r + key // SEL_BLOCK).astype(np.float32), BF16)
    else:
        slab = min(SEL_SLAB, skv)
        tables["emain"] = jnp.asarray((m == np.arange(slab)[None, :] // SEL_BLOCK).astype(np.float32), BF16)
        o = (skv // SEL_BLOCK) % LANE
        tables["etail"] = jnp.asarray((m == o + np.arange(NSA_TK)[None, :] // SEL_BLOCK).astype(np.float32), BF16)

    c0 = np.arange(ncp)[:, None] * CMP_STRIDE
    s0 = np.arange(nbp)[None, :] * SEL_BLOCK
    shared = np.minimum(c0 + CMP_BLOCK, s0 + SEL_BLOCK) - np.maximum(c0, s0)
    wcs = np.maximum(shared, 0).astype(np.float32) / CMP_BLOCK
    tables["wcs"] = jnp.asarray(wcs.T if skv is None else wcs, BF16)

    eg = np.zeros((NSA_G, 3, 128, 256), np.float32)
    for g in range(NSA_G):
        for br in range(3):
            for hh in range(NSA_HPG):
                eg[g, br, 3 * (NSA_HPG * g + hh) + br, 64 * hh:64 * hh + 64] = 1.0
    tables["eg"] = jnp.asarray(eg, BF16)
    return tables


def _foxdec_kernel(pt_ref, qbd_ref, knew_ref, vnew_ref, lfnew_ref, u_ref, *refs):
    npg = FOX_DEC_PAGES
    kv_refs = refs[:npg]
    lf_refs = refs[npg:2 * npg]
    o_ref, m_ref, l_ref, acc_ref, carry_ref = refs[2 * npg:]
    c = pl.program_id(1)
    nrow = FOX_HEADS * 8

    @pl.when(c == 0)
    def _():
        m_ref[...] = jnp.full(m_ref.shape, M_INIT, F32)
        l_ref[...] = jnp.zeros(l_ref.shape, F32)
        acc_ref[...] = jnp.zeros(acc_ref.shape, F32)
        carry_ref[...] = jnp.zeros(carry_ref.shape, F32)

    qbd = qbd_ref[0]
    u = u_ref[...]

    def cum_blocks(lf_blocks):
        lf = lf_blocks[0] if len(lf_blocks) == 1 else jnp.concatenate(lf_blocks, axis=0)
        res = sum(_dot(part, u) for part in _split3(lf))
        run = carry_ref[...]
        out = []
        for k in range(len(lf_blocks)):
            out.append((res[8 * k:8 * k + 8, 0:LANE] + run) * LOG2E)
            run = run + res[8 * k:8 * k + 8, LANE:2 * LANE]
        carry_ref[...] = run
        return out

    def biased(s, ck):
        return (s.reshape(FOX_HEADS, 8, s.shape[1]) - ck[:, None, :]).reshape(nrow, s.shape[1])

    kt = jnp.concatenate([kv_refs[k][0, 0].reshape(512, PAGE) for k in range(npg)], axis=1).astype(BF16)
    vt = jnp.concatenate([kv_refs[k][0, 1].reshape(512, PAGE) for k in range(npg)], axis=1).astype(BF16)
    ck = jnp.concatenate(cum_blocks([lf_refs[k][0] for k in range(npg)]), axis=1)
    t = biased(_dot(qbd, kt), ck)
    tk = t.shape[1]
    m_prev = m_ref[...]
    m_new = jnp.maximum(m_prev, jnp.max(t, axis=1, keepdims=True))
    alpha = jnp.exp2(m_prev - m_new)
    p = jnp.exp2(t - jnp.tile(m_new, (1, tk // LANE)))
    l_ref[...] = alpha * l_ref[...] + jnp.sum(p, axis=1, keepdims=True)
    acc_ref[...] = acc_ref[...] * jnp.tile(alpha, (1, 4)) + _dot_nt(p.astype(BF16), vt)
    m_ref[...] = m_new

    @pl.when(c == pl.num_programs(1) - 1)
    def _():
        t2 = biased(_dot(qbd, knew_ref[0]), cum_blocks([lfnew_ref[0]])[0])
        row = lax.broadcasted_iota(jnp.int32, (nrow, LANE), 0) % 8
        col = lax.broadcasted_iota(jnp.int32, (nrow, LANE), 1)
        t2 = jnp.where(col <= row, t2, NEG)
        m_prev2 = m_ref[...]
        m_new2 = jnp.maximum(m_prev2, jnp.max(t2, axis=1, keepdims=True))
        alpha2 = jnp.exp2(m_prev2 - m_new2)
        p2 = jnp.exp2(t2 - m_new2)
        l2 = alpha2 * l_ref[...] + jnp.sum(p2, axis=1, keepdims=True)
        acc2 = acc_ref[...] * jnp.tile(alpha2, (1, 4)) + _dot_nt(p2.astype(BF16), vnew_ref[0])
        o = acc2 / jnp.tile(l2, (1, 4))
        for h in range(FOX_HEADS):
            o_ref[0, :, 64 * h:64 * h + 64] = o[8 * h:8 * h + 8, 64 * h:64 * h + 64]


def _fox_decode(page_table, qbd, knew_t, vnew_t, lfnew, kv_pool, lft_pool):
    batch, n_pages = page_table.shape
    npg = FOX_DEC_PAGES
    u = jnp.asarray(np.concatenate([np.triu(np.ones((LANE, LANE), np.float32)),
                                    np.ones((LANE, LANE), np.float32)], axis=1), BF16)

    def seq(shape):
        return pl.BlockSpec(shape, lambda b, c, pt: (b,) + tuple(0 for _ in shape[1:]))

    def page(shape, k):
        return pl.BlockSpec(shape, lambda b, c, pt, k=k: (pt[b, c * npg + k],) + tuple(0 for _ in shape[1:]))

    in_specs = ([seq((1, 64, 512)), seq((1, 512, LANE)), seq((1, 512, LANE)), seq((1, 8, LANE)),
                 pl.BlockSpec((LANE, 2 * LANE), lambda b, c, pt: (0, 0))]
                + [page((1, 2, FOX_HEADS, HEAD_DIM, PAGE), k) for k in range(npg)]
                + [page((1, 8, PAGE), k) for k in range(npg)])
    grid_spec = pltpu.PrefetchScalarGridSpec(
        num_scalar_prefetch=1, grid=(batch, n_pages // npg), in_specs=in_specs,
        out_specs=seq((1, 8, 512)),
        scratch_shapes=[pltpu.VMEM((64, LANE), F32), pltpu.VMEM((64, LANE), F32),
                        pltpu.VMEM((64, 512), F32), pltpu.VMEM((8, LANE), F32)])
    return pl.pallas_call(
        _foxdec_kernel,
        grid_spec=grid_spec,
        out_shape=jax.ShapeDtypeStruct((batch, 8, 512), F32),
        compiler_params=_cparams(("arbitrary", "arbitrary")),
        name="foxdec",
    )(page_table, qbd, knew_t, vnew_t, lfnew, u, *([kv_pool] * npg), *([lft_pool] * npg))


def _outmlp_kernel(x_ref, oa_ref, ob_ref, gm_ref, m2_ref, m3_ref, m4_ref, m5_ref, g1_ref,
                   wf_ref, wn_ref, wo_ref, wu_ref, wd_ref, y_ref, y1_ref, h2_ref, acc_ref):
    c = pl.program_id(1)

    @pl.when(c == 0)
    def _():
        ya = _dot(oa_ref[...].astype(BF16), wf_ref[...])
        yb = _dot(ob_ref[...].astype(BF16), wn_ref[...])
        u = gm_ref[:, 0:D_MODEL] * ya + gm_ref[:, D_MODEL:2 * D_MODEL] * yb
        att = _dot(u.astype(BF16), wo_ref[...])
        y1 = x_ref[...] + m2_ref[...] * att
        y1_ref[...] = y1
        ms = jnp.mean(y1 * y1, axis=-1, keepdims=True)
        hn = y1 * lax.rsqrt(ms + EPS) * g1_ref[...]
        h2_ref[...] = (hn * (1.0 + m4_ref[...]) + m3_ref[...]).astype(BF16)
        acc_ref[...] = jnp.zeros(acc_ref.shape, F32)

    up = _dot(h2_ref[...], wu_ref[...])
    act = jnp.square(jnp.maximum(up, 0.0)).astype(BF16)
    acc_ref[...] += _dot(act, wd_ref[...])

    @pl.when(c == pl.num_programs(1) - 1)
    def _():
        y_ref[...] = y1_ref[...] + m5_ref[...] * acc_ref[...]


def _out_mlp(x, oa, ob, gm, mods, g1, wf, wn, wo, wu, wd, tr):
    rows = x.shape[0]
    ffc = 1024
    mrows = mods[0].shape[0]
    mod_spec = (pl.BlockSpec((1, D_MODEL), lambda i, c: (0, 0)) if mrows == 1
                else pl.BlockSpec((tr, D_MODEL), lambda i, c: (i, 0)))
    row = lambda w: pl.BlockSpec((tr, w), lambda i, c: (i, 0))
    const = lambda shape: pl.BlockSpec(shape, lambda i, c: (0, 0))
    return pl.pallas_call(
        _outmlp_kernel,
        grid=(rows // tr, D_FF // ffc),
        in_specs=[row(D_MODEL), row(512), row(512), row(2048), mod_spec, mod_spec, mod_spec, mod_spec,
                  const((1, D_MODEL)), const((512, D_MODEL)), const((512, D_MODEL)),
                  const((D_MODEL, D_MODEL)),
                  pl.BlockSpec((D_MODEL, ffc), lambda i, c: (0, c)),
                  pl.BlockSpec((ffc, D_MODEL), lambda i, c: (c, 0))],
        out_specs=row(D_MODEL),
        out_shape=jax.ShapeDtypeStruct((rows, D_MODEL), F32),
        scratch_shapes=[pltpu.VMEM((tr, D_MODEL), F32), pltpu.VMEM((tr, D_MODEL), BF16),
                        pltpu.VMEM((tr, D_MODEL), F32)],
        compiler_params=_cparams(("arbitrary", "arbitrary")),
        name="outmlp",
    )(x, oa, ob, gm, *mods, g1, wf, wn, wo, wu, wd)


def _prep_w_in(w_in):
    qa, ka, va, zf, qb, zkv, zg, zm = jnp.split(
        w_in, np.cumsum([512, 512, 512, 8, 512, 768, 24, 2048])[:-1].tolist(), axis=1)
    pad = lambda w: jnp.pad(w, ((0, 0), (0, 128 - w.shape[1])))
    return jnp.concatenate([qa, ka, va, qb, zkv, pad(zf), pad(zg), zm], axis=1).astype(BF16)


def _prep_cmp_weights(pe_cmp, w_cmp1, w_cmp2):
    eye = jnp.eye(NSA_G, dtype=F32)
    w1 = w_cmp1.reshape(2, 2, CMP_STRIDE, HEAD_DIM, CMP_HIDDEN)
    w1big = jnp.einsum('khrdj,gq->khrgdqj', w1, eye).reshape(
        2, 2, CMP_STRIDE * NSA_G * HEAD_DIM, NSA_G * CMP_HIDDEN)
    w2big = jnp.einsum('kjd,gq->kgjqd', w_cmp2, eye).reshape(2, NSA_G * CMP_HIDDEN, NSA_G * HEAD_DIM)
    pe = pe_cmp.reshape(2, 2, CMP_STRIDE, 1, HEAD_DIM)
    pe = jnp.broadcast_to(pe, (2, 2, CMP_STRIDE, NSA_G, HEAD_DIM)).reshape(2, 2, 1, CMP_STRIDE * NSA_G * HEAD_DIM)
    return pe, w1big.astype(BF16), w2big.astype(BF16)


def _block_diag_mean(n):
    r = np.arange(n)
    return jnp.asarray((r[:, None] // HEAD_DIM == r[None, :] // HEAD_DIM).astype(np.float32) / HEAD_DIM, BF16)


def kernel(x_prompt, x_sample, cache_fox_kv, cache_fox_logf, cache_nsa_kv, state_nsa_win, page_table,
           c_prompt, c_sample, w_ada, b_ada, g_norm, w_in, b_forget, g_qk_fox, g_qk_nsa,
           pe_cmp, w_cmp1, w_cmp2, rel_bias, w_out_fox, w_out_nsa, w_out, w_up, w_down):
    n_batch, seq, _ = x_prompt.shape
    dec_batch, dec_seq, _ = x_sample.shape
    n_pages = page_table.shape[1]
    past = n_pages * PAGE
    assert n_batch == 1 and dec_seq == 8 and w_ada.shape[0] == 1
    assert seq % PREP_ROWS == 0 and past % PREP_ROWS == 0 and state_nsa_win.shape[2] == WINDOW

    n_c = 1 + dec_batch
    c_rows = -(-n_c // 8) * 8
    c_all = jnp.pad(jnp.concatenate([c_prompt, c_sample], axis=0), ((0, c_rows - n_c), (0, 0)))
    mods = _ada(c_all, w_ada[0], b_ada[0])
    mod_p = [mods[0:1, k * D_MODEL:(k + 1) * D_MODEL] for k in range(6)]
    mod_s = [jnp.repeat(mods[1:n_c, k * D_MODEL:(k + 1) * D_MODEL], dec_seq, axis=0) for k in range(6)]

    w_cat = _prep_w_in(w_in[0])
    bd256 = _block_diag_mean(256)
    tile8 = lambda g: jnp.tile(g, 8)
    gq = jnp.concatenate([tile8(g_qk_fox[0, 0]), tile8(g_qk_fox[0, 1]), tile8(g_qk_nsa[0, 0]),
                          jnp.tile(g_qk_nsa[0, 2], 2), jnp.tile(g_qk_nsa[0, 3], 2)]).reshape(1, 1792)
    bf_pad = jnp.pad(b_forget[0], (0, 120)).reshape(1, 128)
    g0 = g_norm[0, 0].reshape(1, D_MODEL)
    xp = x_prompt.reshape(seq, D_MODEL)
    xs = x_sample.reshape(dec_batch * dec_seq, D_MODEL)
    (p_foxkv, p_nsakv, p_kvwin, p_lft, p_qa, p_kat, p_va, p_qb, p_kwt, p_vwt, p_gb, p_gm) = _inproj(
        xp, mod_p[0], mod_p[1], g0, w_cat, bd256, gq, bf_pad)
    (s_foxkv, s_nsakv, s_kvwin, s_lft, s_qa, s_kat, s_va, s_qb, s_kwt, s_vwt, s_gb, s_gm) = _inproj(
        xs, mod_s[0], mod_s[1], g0, w_cat, bd256, gq, bf_pad)

    ck = _cumsum(p_lft)
    ck_rep = jnp.broadcast_to(ck[:, :, None, :], (FOX_HEADS, seq // LANE, 8, LANE))
    va1 = _with_ones(p_va.reshape(seq, FOX_HEADS, HEAD_DIM), 2).reshape(seq, FOX_HEADS * 2 * HEAD_DIM)
    oa_p = _fox_prompt(p_qa, p_kat, va1, ck_rep)

    pe, w1big, w2big = _prep_cmp_weights(pe_cmp[0], w_cmp1[0], w_cmp2[0])
    bd128 = _block_diag_mean(128)
    gkc = jnp.tile(g_qk_nsa[0, 1], 2).reshape(1, 128)
    look_blocks = PREP_ROWS // CMP_STRIDE
    last_look_p = seq // CMP_STRIDE - 1
    kct_p, vct_p, kst_p, vst_p = _nsaprep(
        [p_nsakv], [pl.BlockSpec((PREP_ROWS, 512), lambda b, t: (t, 0))],
        p_nsakv, pl.BlockSpec((CMP_STRIDE, 512), lambda b, t: (jnp.minimum((t + 1) * look_blocks, last_look_p), 0)),
        (1, seq // PREP_ROWS), 1, seq, (), pe, w1big, w2big, bd128, gkc, False)
    tables_p = _nsa_tables(rel_bias, NSA_TQ, seq // CMP_STRIDE, max(seq // SEL_BLOCK, LANE))
    kwt_pad = jnp.pad(p_kwt, ((0, 0), (WINDOW, 0)))[None]
    vwt_pad = jnp.pad(_with_ones(p_vwt.reshape(NSA_G, HEAD_DIM, seq), 1), ((0, 0), (0, 0), (WINDOW, 0)))[None]
    ob_p = _nsa_attention(p_qb[None], kct_p, vct_p, kst_p, vst_p, None, kwt_pad, vwt_pad, p_gb[None],
                          tables_p, NSA_TQ, 0)[0]

    kv_pool = jnp.transpose(cache_fox_kv[0], (0, 2, 3, 4, 1))
    lft_pool = jnp.transpose(cache_fox_logf[0], (0, 2, 1))
    eye_h = jnp.eye(FOX_HEADS, dtype=BF16)
    qa_s = s_qa.reshape(dec_batch, dec_seq, FOX_HEADS, HEAD_DIM)
    qbd = jnp.einsum('bjhd,hk->bhjkd', qa_s, eye_h).reshape(dec_batch, 64, 512)

    def new_cols(a_t):
        a = jnp.transpose(a_t.reshape(a_t.shape[0], dec_batch, dec_seq), (1, 0, 2))
        return jnp.pad(a, ((0, 0), (0, 0), (0, LANE - dec_seq)))

    knew_t = new_cols(s_kat)
    vnew_t = new_cols(jnp.transpose(s_va))
    lfnew = new_cols(s_lft)
    oa_s = _fox_decode(page_table, qbd, knew_t, vnew_t, lfnew, kv_pool, lft_pool)

    nsa_pool = jnp.transpose(cache_nsa_kv[0], (0, 2, 3, 4, 1))
    npg = NSA_DEC_PAGES
    page_shape = (1, 4, NSA_G, HEAD_DIM, PAGE)
    part_specs = [pl.BlockSpec(page_shape, lambda b, t, pt, k=k: (pt[b, t * npg + k], 0, 0, 0, 0))
                  for k in range(npg)]
    look_spec = pl.BlockSpec(
        page_shape, lambda b, t, pt: (pt[b, jnp.minimum((t + 1) * npg, n_pages - 1)], 0, 0, 0, 0))
    kct_s, vct_s, kst_s, vst_s = _nsaprep(
        [nsa_pool] * npg, part_specs, nsa_pool, look_spec,
        (dec_batch, past // PREP_ROWS), dec_batch, past, (page_table,), pe, w1big, w2big, bd128, gkc, True)
    tile_base = past // NSA_TQ
    nb_dec = past // SEL_BLOCK + NSA_TK // SEL_BLOCK
    tables_s = _nsa_tables(rel_bias, dec_seq, past // CMP_STRIDE, -(-nb_dec // LANE) * LANE, skv=past)
    new_nsa_t = jnp.transpose(s_nsakv.reshape(dec_batch, dec_seq, 512), (0, 2, 1))
    pad_tail = lambda a: jnp.pad(a, ((0, 0), (0, 0), (0, NSA_TK - dec_seq))).astype(BF16)
    kst_tail = pad_tail(new_nsa_t[:, 256:384])
    vst_tail = _with_ones(pad_tail(new_nsa_t[:, 384:512]).reshape(dec_batch, NSA_G, HEAD_DIM, NSA_TK), 2)
    win_t = jnp.transpose(state_nsa_win[0], (0, 2, 3, 4, 1)).reshape(dec_batch, 256, WINDOW)
    new_win_t = jnp.transpose(s_kvwin.reshape(dec_batch, dec_seq, 256), (0, 2, 1))
    band_t = jnp.concatenate([win_t, new_win_t], axis=2)
    band_pad = jnp.pad(band_t, ((0, 0), (0, 0), (0, WIN_BAND - band_t.shape[2]))).astype(BF16)
    ob_s = _nsa_attention(s_qb.reshape(dec_batch, dec_seq, 512), kct_s, vct_s, kst_s, vst_s,
                          (kst_tail, vst_tail), band_pad[:, 0:128],
                          _with_ones(band_pad[:, 128:256].reshape(dec_batch, NSA_G, HEAD_DIM, WIN_BAND), 2),
                          s_gb.reshape(dec_batch, dec_seq, 128), tables_s, dec_seq, tile_base)

    g1 = g_norm[0, 1].reshape(1, D_MODEL)
    wf, wn, wo = w_out_fox[0].astype(BF16), w_out_nsa[0].astype(BF16), w_out[0].astype(BF16)
    wu, wd = w_up[0].astype(BF16), w_down[0].astype(BF16)
    y_p = _out_mlp(xp, oa_p, ob_p, p_gm, mod_p[2:6], g1, wf, wn, wo, wu, wd, 512)
    y_s = _out_mlp(xs, oa_s.reshape(dec_batch * dec_seq, 512), ob_s.reshape(dec_batch * dec_seq, 512),
                   s_gm, mod_s[2:6], g1, wf, wn, wo, wu, wd, dec_batch * dec_seq)

    win_keep = min(WINDOW, seq)
    new_win_sample = jnp.transpose(band_t[:, :, dec_seq:].reshape(dec_batch, 2, NSA_G, HEAD_DIM, WINDOW),
                                   (0, 4, 1, 2, 3))[None]
    return (
        y_p.reshape(1, seq, D_MODEL),
        y_s.reshape(dec_batch, dec_seq, D_MODEL),
        p_foxkv.reshape(1, 1, seq, 2, FOX_HEADS, HEAD_DIM),
        s_foxkv.reshape(1, dec_batch, dec_seq, 2, FOX_HEADS, HEAD_DIM),
        jnp.transpose(p_lft).reshape(1, 1, seq, FOX_HEADS),
        jnp.transpose(s_lft).reshape(1, dec_batch, dec_seq, FOX_HEADS),
        p_nsakv.reshape(1, 1, seq, 4, NSA_G, HEAD_DIM),
        s_nsakv.reshape(1, dec_batch, dec_seq, 4, NSA_G, HEAD_DIM),
        p_kvwin[seq - win_keep:].reshape(1, 1, win_keep, 2, NSA_G, HEAD_DIM),
        new_win_sample,
    )
```

```python
import functools
import math

import numpy as np
import jax
import jax.numpy as jnp
from jax import lax
from jax.experimental import pallas as pl
from jax.experimental.pallas import tpu as pltpu

F32 = jnp.float32
BF16 = jnp.bfloat16

D_MODEL = 1024
HEAD_DIM = 64
FOX_HEADS = 8
NSA_HEADS = 8
NSA_G = 2
NSA_HPG = 4
CMP_BLOCK = 32
CMP_STRIDE = 16
CMP_HIDDEN = 128
SEL_BLOCK = 64
N_SEL = 16
N_FORCED = 3
WINDOW = 512
REL_BUCKETS = 32
REL_MAX_DIST = 128
PAGE = 128
EPS = 1e-6
D_FF = 4 * D_MODEL

LANE = 128
LOG2E = 1.4426950408889634
QSCALE = LOG2E / math.sqrt(HEAD_DIM)
NEG = -float(2 ** 100)
M_INIT = -1e30
VMEM_LIMIT = 58 * 1024 * 1024

NSA_TQ = 256
NSA_TK = 256
NSA_FAR = 1024
WIN_BAND = WINDOW + NSA_TQ
SEL_NEAR = 2 * NSA_TK
CMP_NEAR = 32
SEL_SLAB = LANE * SEL_BLOCK
FOX_TQ = 1024
FOX_FAR = 1024
FOX_STRIPS = 4
PREP_ROWS = 2048
FOX_DEC_PAGES = 32
NSA_DEC_PAGES = PREP_ROWS // PAGE


def _cparams(sem):
    return pltpu.CompilerParams(dimension_semantics=sem, vmem_limit_bytes=VMEM_LIMIT)


def _split2(x):
    hi = x.astype(BF16)
    lo = (x - hi.astype(F32)).astype(BF16)
    return hi, lo


def _split3(x):
    hi = x.astype(BF16)
    r = x - hi.astype(F32)
    mid = r.astype(BF16)
    lo = (r - mid.astype(F32)).astype(BF16)
    return hi, mid, lo


def _dot(a, b):
    return jnp.dot(a, b, preferred_element_type=F32)


def _dot_nt(a, b):
    return lax.dot_general(a, b, (((1,), (1,)), ((), ())), preferred_element_type=F32)


def _ada_kernel(c_ref, w_ref, b_ref, o_ref):
    c = c_ref[...]
    s = c * jax.nn.sigmoid(c)
    o_ref[...] = jnp.dot(s, w_ref[...], precision=lax.Precision.HIGHEST,
                         preferred_element_type=F32) + b_ref[...]


def _ada(c_all, w_ada, b_ada):
    rows = c_all.shape[0]
    n = w_ada.shape[1]
    tn = 1536
    return pl.pallas_call(
        _ada_kernel,
        grid=(n // tn,),
        in_specs=[pl.BlockSpec((rows, D_MODEL), lambda j: (0, 0)),
                  pl.BlockSpec((D_MODEL, tn), lambda j: (0, j)),
                  pl.BlockSpec((1, tn), lambda j: (0, j))],
        out_specs=pl.BlockSpec((rows, tn), lambda j: (0, j)),
        out_shape=jax.ShapeDtypeStruct((rows, n), F32),
        compiler_params=_cparams(("arbitrary",)),
        name="ada",
    )(c_all, w_ada, b_ada.reshape(1, n))


_C_QA, _C_KA, _C_VA, _C_QB = 0, 512, 1024, 1536
_C_KV = 2048
_C_F = 2816
_C_G = 2944
_C_M = 3072
_C_END = 5120


def _inproj_kernel(x_ref, sh_ref, sc_ref, g0_ref, w_ref, bd_ref, gq_ref, bf_ref,
                   foxkv_ref, nsakv_ref, kvwin_ref, lft_ref, qa_ref, kat_ref, va_ref,
                   qb_ref, kwt_ref, vwt_ref, gb_ref, gm_ref):
    x = x_ref[...]
    ms = jnp.mean(x * x, axis=-1, keepdims=True)
    y = x * lax.rsqrt(ms + EPS) * g0_ref[...]
    h = y * (1.0 + sc_ref[...]) + sh_ref[...]
    hb = h.astype(BF16)
    bd = bd_ref[...]

    def head_norm(z, gain):
        zz = (z * z).astype(BF16)
        w = z.shape[1]
        if w == LANE:
            msq = _dot(zz, bd[:LANE, :LANE])
        else:
            msq = jnp.concatenate([_dot(zz[:, k:k + 256], bd) for k in range(0, w, 256)], axis=1)
        return z * lax.rsqrt(msq + EPS) * gain

    za = _dot(hb, w_ref[:, _C_QA:_C_KV])
    qa = head_norm(za[:, _C_QA:_C_KA], gq_ref[:, 0:512])
    ka = head_norm(za[:, _C_KA:_C_VA], gq_ref[:, 512:1024])
    va = za[:, _C_VA:_C_QB]
    qb = head_norm(za[:, _C_QB:_C_KV], gq_ref[:, 1024:1536])
    foxkv_ref[:, 0:512] = ka
    foxkv_ref[:, 512:1024] = va
    qa_ref[...] = (qa * QSCALE).astype(BF16)
    kat_ref[...] = ka.T.astype(BF16)
    va_ref[...] = va.astype(BF16)
    qb_ref[...] = (qb * QSCALE).astype(BF16)

    zb = _dot(hb, w_ref[:, _C_KV:_C_M])
    ksel = head_norm(zb[:, 256:384], gq_ref[:, 1536:1664])
    kwin = head_norm(zb[:, 512:640], gq_ref[:, 1664:1792])
    vwin = zb[:, 640:768]
    nsakv_ref[:, 0:256] = zb[:, 0:256]
    nsakv_ref[:, 256:384] = ksel
    nsakv_ref[:, 384:512] = zb[:, 384:512]
    kvwin_ref[:, 0:128] = kwin
    kvwin_ref[:, 128:256] = vwin
    kwt_ref[...] = kwin.T.astype(BF16)
    vwt_ref[...] = vwin.T.astype(BF16)
    zf = zb[:, 768:896] + bf_ref[...]
    lf = jnp.minimum(zf, 0.0) - jnp.log(1.0 + jnp.exp(-jnp.abs(zf)))
    lft_ref[...] = lf.T[0:8, :]
    gb_ref[...] = jax.nn.sigmoid(zb[:, 896:1024])

    zm = _dot(hb, w_ref[:, _C_M:_C_END])
    gm_ref[...] = jax.nn.sigmoid(zm)


def _inproj(x, shift, scale, g0, w_cat, bd, gq, bf_pad):
    rows = x.shape[0]
    tr = 256
    mrows = shift.shape[0]
    mod_spec = (pl.BlockSpec((1, D_MODEL), lambda i: (0, 0)) if mrows == 1
                else pl.BlockSpec((tr, D_MODEL), lambda i: (i, 0)))
    const = lambda shape: pl.BlockSpec(shape, lambda i: tuple(0 for _ in shape))
    row_spec = lambda w: pl.BlockSpec((tr, w), lambda i: (i, 0))
    col_spec = lambda h: pl.BlockSpec((h, tr), lambda i: (0, i))
    out_shape = (
        jax.ShapeDtypeStruct((rows, 1024), F32),
        jax.ShapeDtypeStruct((rows, 512), F32),
        jax.ShapeDtypeStruct((rows, 256), F32),
        jax.ShapeDtypeStruct((8, rows), F32),
        jax.ShapeDtypeStruct((rows, 512), BF16),
        jax.ShapeDtypeStruct((512, rows), BF16),
        jax.ShapeDtypeStruct((rows, 512), BF16),
        jax.ShapeDtypeStruct((rows, 512), BF16),
        jax.ShapeDtypeStruct((128, rows), BF16),
        jax.ShapeDtypeStruct((128, rows), BF16),
        jax.ShapeDtypeStruct((rows, 128), F32),
        jax.ShapeDtypeStruct((rows, 2048), F32),
    )
    out_specs = (row_spec(1024), row_spec(512), row_spec(256), col_spec(8), row_spec(512),
                 col_spec(512), row_spec(512), row_spec(512), col_spec(128), col_spec(128),
                 row_spec(128), row_spec(2048))
    return pl.pallas_call(
        _inproj_kernel,
        grid=(rows // tr,),
        in_specs=[row_spec(D_MODEL), mod_spec, mod_spec, const((1, D_MODEL)),
                  const((D_MODEL, _C_END)), const((256, 256)), const((1, 1792)), const((1, 128))],
        out_specs=out_specs,
        out_shape=out_shape,
        compiler_params=_cparams(("arbitrary",)),
        name="inproj",
    )(x, shift, scale, g0, w_cat, bd, gq, bf_pad)


def _cumsum_kernel(x_ref, u_ref, pick_ref, bt_ref, o_ref):
    u = u_ref[...]
    c2 = sum(_dot(p, u) for p in _split3(x_ref[...]))
    pick = pick_ref[...]
    totb = sum(_dot(p, pick) for p in _split3(c2))
    bt = bt_ref[...]
    offs = sum(_dot(bt, p) for p in _split3(totb))
    o_ref[...] = (c2 + offs) * LOG2E


def _cumsum(lft):
    heads, s = lft.shape
    nblk = s // LANE
    rows = heads * nblk
    x = lft.reshape(rows, LANE)
    u = jnp.asarray(np.triu(np.ones((LANE, LANE), np.float32)), BF16)
    pick = np.zeros((LANE, LANE), np.float32)
    pick[LANE - 1, :] = 1.0
    r = np.arange(rows)
    bt = ((r[:, None] // nblk == r[None, :] // nblk) & (r[None, :] < r[:, None])).astype(np.float32)
    full = lambda shape: pl.BlockSpec(shape, lambda i: (0, 0))
    out = pl.pallas_call(
        _cumsum_kernel,
        grid=(1,),
        in_specs=[full((rows, LANE)), full((LANE, LANE)), full((LANE, LANE)), full((rows, rows))],
        out_specs=full((rows, LANE)),
        out_shape=jax.ShapeDtypeStruct((rows, LANE), F32),
        compiler_params=_cparams(("arbitrary",)),
        name="cumsum",
    )(x, u, jnp.asarray(pick, BF16), jnp.asarray(bt, BF16))
    return out.reshape(heads, nblk, LANE)


def _online_update(t, v1, m_ref, acc_ref, v_transposed=False):
    tk = t.shape[1]
    m_prev = m_ref[...]
    m_new = jnp.maximum(m_prev, jnp.max(t, axis=1, keepdims=True))
    alpha = jnp.exp2(m_prev - m_new)
    p = jnp.exp2(t - jnp.tile(m_new, (1, tk // LANE))).astype(BF16)
    pv = _dot_nt(p, v1) if v_transposed else _dot(p, v1)
    acc_ref[...] = acc_ref[...] * alpha + pv
    m_ref[...] = m_new


def _normalize(acc):
    return acc[:, 0:HEAD_DIM] / acc[:, HEAD_DIM:2 * HEAD_DIM]


def _with_ones(v, axis):
    shape = list(v.shape)
    shape[axis] = HEAD_DIM
    return jnp.concatenate([v, jnp.ones(shape, v.dtype)], axis=axis)


def _fox_kernel(q_ref, kt_ref, v_ref, ck_ref, o_ref, m_ref, acc_ref):
    i = pl.program_id(1)
    rs = FOX_TQ // FOX_STRIPS
    chains = [(hh, st) for hh in range(2) for st in range(FOX_STRIPS)]
    m_ref[...] = jnp.full(m_ref.shape, M_INIT, F32)
    acc_ref[...] = jnp.zeros(acc_ref.shape, F32)

    def tile(koff, width, masked):
        blk0 = koff // LANE
        widths = [min(width, (st + 1) * rs) if masked else width for _, st in chains]
        nck = [-jnp.concatenate([ck_ref[hh, blk0 + c] for c in range(width // LANE)], axis=1) for hh in range(2)]
        scores = [_dot(q_ref[st * rs:(st + 1) * rs, 64 * hh:64 * hh + 64], kt_ref[64 * hh:64 * hh + 64, pl.ds(koff, w)])
                  + jnp.tile(nck[hh][:, 0:w], (rs // 8, 1))
                  for (hh, st), w in zip(chains, widths)]
        for (hh, st), w, t in zip(chains, widths, scores):
            if masked:
                row = st * rs + lax.broadcasted_iota(jnp.int32, (rs, w), 0)
                col = lax.broadcasted_iota(jnp.int32, (rs, w), 1)
                t = jnp.where(row >= col, t, NEG)
            rows = pl.ds(st * rs, rs)
            _online_update(t, v_ref[pl.ds(koff, w), LANE * hh:LANE * (hh + 1)], m_ref.at[hh, rows],
                           acc_ref.at[hh, rows])

    def body(j, carry):
        tile(pl.multiple_of(j * FOX_FAR, FOX_FAR), FOX_FAR, False)
        return carry

    lax.fori_loop(0, i * (FOX_TQ // FOX_FAR), body, 0)
    tile(pl.multiple_of(i * FOX_TQ, FOX_TQ), FOX_TQ, True)
    o_ref[...] = jnp.concatenate([_normalize(acc_ref[hh]) for hh in range(2)], axis=1)


def _fox_prompt(qa, kat, va1, ck_rep):
    s = qa.shape[0]
    nblk = s // LANE
    return pl.pallas_call(
        _fox_kernel,
        grid=(FOX_HEADS // 2, s // FOX_TQ),
        in_specs=[pl.BlockSpec((FOX_TQ, LANE), lambda hp, i: (i, hp)),
                  pl.BlockSpec((LANE, s), lambda hp, i: (hp, 0)),
                  pl.BlockSpec((s, 2 * LANE), lambda hp, i: (0, hp)),
                  pl.BlockSpec((2, nblk, 8, LANE), lambda hp, i: (hp, 0, 0, 0))],
        out_specs=pl.BlockSpec((FOX_TQ, LANE), lambda hp, i: (i, hp)),
        out_shape=jax.ShapeDtypeStruct((s, 512), F32),
        scratch_shapes=[pltpu.VMEM((2, FOX_TQ, LANE), F32), pltpu.VMEM((2, FOX_TQ, LANE), F32)],
        compiler_params=_cparams(("arbitrary", "arbitrary")),
        name="fox",
    )(qa, kat, va1, ck_rep)


def _gelu_tanh(x):
    return 0.5 * x * (1.0 + jnp.tanh(math.sqrt(2.0 / math.pi) * (x + 0.044715 * (x * x * x))))


def _nsaprep_kernel(*refs, n_parts, n_prefetch, feature_major):
    refs = refs[n_prefetch:]
    part_refs = refs[:n_parts]
    (look_ref, pe_ref, w1_ref, w2_ref, bd_ref, gkc_ref,
     kct_ref, vct_ref, kst_ref, vst_ref, xk_ref, xv_ref) = refs[n_parts:]
    xs_refs = (xk_ref, xv_ref)
    prow = PREP_ROWS // n_parts
    ntok = PREP_ROWS // CMP_STRIDE
    flat = CMP_STRIDE * LANE
    chunks = ([], [])
    for k in range(n_parts):
        lo, hi = k * prow, (k + 1) * prow
        if feature_major:
            blk = part_refs[k][...].reshape(4, LANE, prow)
            for kv in range(2):
                chunks[kv].append(blk[kv].T.reshape(prow // CMP_STRIDE, flat))
            kst_ref[0, :, lo:hi] = blk[2].astype(BF16)
            vsel_t = blk[3].astype(BF16)
        else:
            blk = part_refs[k][...]
            xk_ref[lo:hi, :] = blk[:, 0:128]
            xv_ref[lo:hi, :] = blk[:, 128:256]
            kst_ref[0, :, lo:hi] = blk[:, 256:384].T.astype(BF16)
            vsel_t = blk[:, 384:512].T.astype(BF16)
        for g in range(NSA_G):
            vst_ref[0, g, 0:HEAD_DIM, lo:hi] = vsel_t[HEAD_DIM * g:HEAD_DIM * (g + 1)]
            vst_ref[0, g, HEAD_DIM:2 * HEAD_DIM, lo:hi] = jnp.ones((HEAD_DIM, prow), BF16)
    outs = []
    if feature_major:
        look = look_ref[...].reshape(4, LANE, PAGE)
        for kv in range(2):
            chunks[kv].append(look[kv].T.reshape(PAGE // CMP_STRIDE, flat))
            xc = jnp.concatenate(chunks[kv], axis=0)
            first = _dot((xc[0:ntok] + pe_ref[kv, 0]).astype(BF16), w1_ref[kv, 0])
            second = _dot((xc + pe_ref[kv, 1]).astype(BF16), w1_ref[kv, 1])
            act = _gelu_tanh(first + second[1:ntok + 1]).astype(BF16)
            outs.append(_dot(act, w2_ref[kv]))
    else:
        look = look_ref[...]
        xk_ref[PREP_ROWS:PREP_ROWS + CMP_STRIDE, :] = look[:, 0:128]
        xv_ref[PREP_ROWS:PREP_ROWS + CMP_STRIDE, :] = look[:, 128:256]
        for kv in range(2):
            hid = jnp.zeros((ntok, NSA_G * CMP_HIDDEN), F32)
            for half in range(2):
                xcat = jnp.concatenate(
                    [xs_refs[kv][pl.ds(half * CMP_STRIDE + r, ntok, stride=CMP_STRIDE), :]
                     for r in range(CMP_STRIDE)], axis=1)
                xcat = (xcat + pe_ref[kv, half]).astype(BF16)
                hid = hid + _dot(xcat, w1_ref[kv, half])
            act = _gelu_tanh(hid).astype(BF16)
            outs.append(_dot(act, w2_ref[kv]))
    kc, vc = outs
    msq = _dot((kc * kc).astype(BF16), bd_ref[...])
    kc = kc * lax.rsqrt(msq + EPS) * gkc_ref[...]
    kct_ref[0] = kc.T.astype(BF16)
    vct_ref[0] = vc.T.astype(BF16)


def _nsaprep(parts_arrays, part_specs, look_array, look_spec, grid, batch, length, prefetch,
             pe, w1, w2, bd128, gkc, feature_major):
    n_parts = len(part_specs)
    ntok = PREP_ROWS // CMP_STRIDE
    ncp = length // CMP_STRIDE
    npf = len(prefetch)

    def cm(shape):
        return pl.BlockSpec(shape, lambda b, t, *pf: tuple(0 for _ in shape))

    in_specs = list(part_specs) + [look_spec, cm(pe.shape), cm(w1.shape), cm(w2.shape),
                                   cm(bd128.shape), cm(gkc.shape)]
    out_specs = (
        pl.BlockSpec((1, 128, ntok), lambda b, t, *pf: (b, 0, t)),
        pl.BlockSpec((1, 128, ntok), lambda b, t, *pf: (b, 0, t)),
        pl.BlockSpec((1, 128, PREP_ROWS), lambda b, t, *pf: (b, 0, t)),
        pl.BlockSpec((1, NSA_G, 128, PREP_ROWS), lambda b, t, *pf: (b, 0, 0, t)),
    )
    out_shape = (
        jax.ShapeDtypeStruct((batch, 128, ncp), BF16),
        jax.ShapeDtypeStruct((batch, 128, ncp), BF16),
        jax.ShapeDtypeStruct((batch, 128, length), BF16),
        jax.ShapeDtypeStruct((batch, NSA_G, 128, length), BF16),
    )
    grid_spec = pltpu.PrefetchScalarGridSpec(
        num_scalar_prefetch=npf, grid=grid, in_specs=in_specs, out_specs=out_specs,
        scratch_shapes=[pltpu.VMEM((PREP_ROWS + CMP_STRIDE, LANE), F32),
                        pltpu.VMEM((PREP_ROWS + CMP_STRIDE, LANE), F32)])
    return pl.pallas_call(
        functools.partial(_nsaprep_kernel, n_parts=n_parts, n_prefetch=npf, feature_major=feature_major),
        grid_spec=grid_spec,
        out_shape=out_shape,
        compiler_params=_cparams(("arbitrary", "arbitrary")),
        name="nsaprep",
    )(*prefetch, *parts_arrays, look_array, pe, w1, w2, bd128, gkc)


def _nsa_kernel(*refs, tq, tile_base, single_shot, ncp, nbp):
    if single_shot:
        (q_ref, kct_ref, vct_ref, kst_ref, vst_ref, kstt_ref, vstt_ref, kwt_ref, vwt_ref, g_ref,
         tw_ref, ts_ref, tc_ref, emain_ref, etail_ref, wcs_ref, eg_ref, o_ref) = refs
    else:
        (q_ref, kct_ref, vct_ref, kst_ref, vst_ref, kwt_ref, vwt_ref, g_ref,
         tw_ref, ts_ref, tc_ref, etab_ref, wcs_ref, eg_ref,
         o_ref, sel_ref, m_ref, acc_ref) = refs
    rows = NSA_HPG * tq
    i_abs = tile_base + pl.program_id(2)
    q0 = i_abs * NSA_TQ
    q4 = q_ref[0]
    qhs = [q4[:, 64 * hh:64 * hh + 64] for hh in range(NSA_HPG)]
    qs = jnp.concatenate(qhs, axis=0)

    n_iota = lax.broadcasted_iota(jnp.int32, (1, ncp), 1)
    base = jnp.where(n_iota >= 16 * i_abs + 16, NEG, 0.0)
    pr = lax.broadcasted_iota(jnp.int32, (2 * CMP_NEAR, ncp), 0) % CMP_NEAR
    pn = lax.broadcasted_iota(jnp.int32, (2 * CMP_NEAR, ncp), 1)
    place = jnp.where(pn == 16 * i_abs - 16 + pr, 1.0, 0.0).astype(BF16)
    kct = kct_ref[0]
    vct = vct_ref[0]
    psum = jnp.zeros((tq, ncp), F32)
    o_cmp = []
    cmp_logits = [_dot(qhs[hh], kct) + (_dot(tc_ref[0, hh * tq:(hh + 1) * tq, :], place) + base)
                  for hh in range(NSA_HPG)]
    for hh in range(NSA_HPG):
        s = cmp_logits[hh]
        m = jnp.max(s, axis=1, keepdims=True)
        p = jnp.exp2(s - m)
        l = jnp.sum(p, axis=1, keepdims=True)
        inv = jnp.where(m > 0.5 * NEG, 1.0 / l, 0.0)
        pn_ = p * inv
        psum = psum + pn_
        o_cmp.append(_dot_nt(pn_.astype(BF16), vct))
    wcs = wcs_ref[...]

    ax = 1 if single_shot else 0
    if single_shot:
        imp = sum(_dot(part, wcs) for part in _split2(psum))
        posq = q0 + lax.broadcasted_iota(jnp.int32, (tq, 1), 0)
        blk = lax.broadcasted_iota(jnp.int32, (tq, nbp), 1)
    else:
        imp = sum(_dot_nt(wcs, part) for part in _split2(psum))
        posq = q0 + lax.broadcasted_iota(jnp.int32, (1, tq), 1)
        blk = lax.broadcasted_iota(jnp.int32, (nbp, tq), 0)
    blk_f = blk.astype(F32)
    cur = posq // SEL_BLOCK
    forced = (blk == 0) | (blk == cur) | (blk == cur - 1)
    sel = jnp.where(forced, 0.0, NEG)
    score = jnp.where(forced | (blk * SEL_BLOCK > posq), -jnp.inf, imp)
    for _ in range(N_SEL - N_FORCED):
        mx = jnp.max(score, axis=ax, keepdims=True)
        cand = jnp.where(score == mx, blk_f, float(4 * nbp))
        amin = jnp.min(cand, axis=ax, keepdims=True)
        hit = blk_f == amin
        sel = jnp.where(hit & (mx > -jnp.inf), 0.0, sel)
        score = jnp.where(hit, -jnp.inf, score)
    if not single_shot:
        sel = sel.T

    if single_shot:
        skv = kst_ref.shape[2]
        slab = min(SEL_SLAB, skv)
        selb = sel.astype(BF16)
        emain = emain_ref[...]
        mb = jnp.concatenate([_dot(selb[:, LANE * c:LANE * (c + 1)], emain) for c in range(skv // slab)], axis=1)
        ct = (skv // SEL_BLOCK) // LANE
        mb_tail = _dot(selb[:, LANE * ct:LANE * (ct + 1)], etail_ref[...])
        near = ts_ref[0]
        s_main = _dot(qs, kst_ref[0])
        t_main = (s_main.reshape(NSA_HPG, tq, skv) + mb[None]).reshape(rows, skv)
        t_main = jnp.concatenate([t_main[:, :skv - NSA_TK], t_main[:, skv - NSA_TK:] + near[:, 0:NSA_TK]], axis=1)
        s_tail = _dot(qs, kstt_ref[0])
        t_tail = (s_tail.reshape(NSA_HPG, tq, NSA_TK) + mb_tail[None]).reshape(rows, NSA_TK) + near[:, NSA_TK:SEL_NEAR]
        m = jnp.maximum(jnp.max(t_main, axis=1, keepdims=True), jnp.max(t_tail, axis=1, keepdims=True))
        p_main = jnp.exp2(t_main - m).astype(BF16)
        p_tail = jnp.exp2(t_tail - m).astype(BF16)
        o_sel = _normalize(_dot_nt(p_main, vst_ref[0, 0]) + _dot_nt(p_tail, vstt_ref[0, 0]))
    else:
        sel_ref[...] = sel
        m_ref[...] = jnp.full(m_ref.shape, M_INIT, F32)
        acc_ref[...] = jnp.zeros(acc_ref.shape, F32)

        def sel_tile(koff, width, near_tbl):
            kt = kst_ref[0, :, pl.ds(koff, width)]
            vt = vst_ref[0, 0, :, pl.ds(koff, width)]
            mbs = []
            for u in range(width // NSA_TK):
                j = koff // NSA_TK + u
                slab_off = pl.multiple_of((j // 32) * LANE, LANE)
                sel128 = sel_ref[:, pl.ds(slab_off, LANE)]
                mbs.append(_dot(sel128.astype(BF16), etab_ref[j % 32]))
            mb = mbs[0] if len(mbs) == 1 else jnp.concatenate(mbs, axis=1)
            logits = []
            for hh in range(NSA_HPG):
                add = mb if near_tbl is None else mb + near_tbl[hh * tq:(hh + 1) * tq]
                logits.append(_dot(qhs[hh], kt) + add)
            for hh in range(NSA_HPG):
                hrows = pl.ds(hh * tq, tq)
                _online_update(logits[hh], vt, m_ref.at[hrows], acc_ref.at[hrows], v_transposed=True)

        n_far_tiles = jnp.maximum(i_abs - 1, 0)
        n_big = n_far_tiles // (NSA_FAR // NSA_TK)
        rem = n_far_tiles % (NSA_FAR // NSA_TK)

        def far_body(j, carry):
            sel_tile(pl.multiple_of(j * NSA_FAR, NSA_FAR), NSA_FAR, None)
            return carry

        lax.fori_loop(0, n_big, far_body, 0)
        rem_off = n_big * NSA_FAR

        @pl.when(rem >= 2)
        def _():
            sel_tile(pl.multiple_of(rem_off, 2 * NSA_TK), 2 * NSA_TK, None)

        @pl.when(rem % 2 == 1)
        def _():
            sel_tile(pl.multiple_of(rem_off + (rem // 2) * 2 * NSA_TK, NSA_TK), NSA_TK, None)

        @pl.when(i_abs >= 1)
        def _():
            sel_tile(pl.multiple_of((i_abs - 1) * NSA_TK, NSA_TK), SEL_NEAR, ts_ref[0])

        @pl.when(i_abs == 0)
        def _():
            sel_tile(pl.multiple_of(i_abs * NSA_TK, NSA_TK), NSA_TK, ts_ref[0, :, NSA_TK:SEL_NEAR])
        o_sel = _normalize(acc_ref[...])

    band0 = pl.multiple_of((i_abs - tile_base if single_shot else i_abs) * NSA_TQ, NSA_TQ)
    ktw = kwt_ref[0, :, pl.ds(band0, WIN_BAND)]
    vtw = vwt_ref[0, 0, :, pl.ds(band0, WIN_BAND)]
    b_iota = lax.broadcasted_iota(jnp.int32, (1, WIN_BAND), 1)
    before_start = jnp.where(b_iota >= WINDOW - q0, 0.0, NEG)
    win_logits = [_dot(qhs[hh], ktw) + (tw_ref[0, hh * tq:(hh + 1) * tq, :] + before_start)
                  for hh in range(NSA_HPG)]
    o_win = []
    for hh in range(NSA_HPG):
        s = win_logits[hh]
        p = jnp.exp2(s - jnp.max(s, axis=1, keepdims=True)).astype(BF16)
        o_win.append(_normalize(_dot_nt(p, vtw)))

    def heads_to_lanes(o):
        return jnp.concatenate([o[hh * tq:(hh + 1) * tq] for hh in range(NSA_HPG)], axis=1)

    branches = (jnp.concatenate(o_cmp, axis=1), heads_to_lanes(o_sel), jnp.concatenate(o_win, axis=1))
    gparts = _split2(g_ref[0])
    out = jnp.zeros((tq, NSA_HPG * HEAD_DIM), F32)
    for br in range(3):
        gexp = sum(_dot(part, eg_ref[0, br]) for part in gparts)
        out = out + gexp * branches[br]
    o_ref[0] = out


def _nsa_attention(q, kct, vct, kst, vst, tail, kwt, vwt, gates, tables, tq, tile_base):
    batch, sq, _ = q.shape
    n_qt = sq // tq
    ncp = kct.shape[2]
    skv = kst.shape[2]
    lw = kwt.shape[2]
    rows = NSA_HPG * tq
    single_shot = tail is not None
    nbp = tables["wcs"].shape[1 if single_shot else 0]
    grp = lambda n: pl.BlockSpec((1, 64, n), lambda b, g, i: (b, g, 0))
    per_g = lambda w: pl.BlockSpec((1, rows, w), lambda b, g, i: (g, 0, 0))
    const = lambda a: pl.BlockSpec(a.shape, lambda b, g, i: tuple(0 for _ in a.shape))
    v1 = lambda n: pl.BlockSpec((1, 1, 2 * HEAD_DIM, n), lambda b, g, i: (b, g, 0, 0))
    in_specs = [pl.BlockSpec((1, tq, 256), lambda b, g, i: (b, i, g)), grp(ncp), grp(ncp), grp(skv), v1(skv)]
    args = [q, kct, vct, kst, vst]
    if single_shot:
        in_specs += [grp(NSA_TK), v1(NSA_TK)]
        args += list(tail)
    in_specs += [grp(lw), v1(lw), pl.BlockSpec((1, tq, 128), lambda b, g, i: (b, i, 0)),
                 per_g(WIN_BAND), per_g(SEL_NEAR), per_g(2 * CMP_NEAR)]
    args += [kwt, vwt, gates, tables["tw"], tables["ts"], tables["tc"]]
    if single_shot:
        in_specs += [const(tables["emain"]), const(tables["etail"])]
        args += [tables["emain"], tables["etail"]]
        scratch = []
    else:
        in_specs += [const(tables["etab"])]
        args += [tables["etab"]]
        scratch = [pltpu.VMEM((tq, nbp), F32), pltpu.VMEM((rows, LANE), F32), pltpu.VMEM((rows, LANE), F32)]
    in_specs += [const(tables["wcs"]), pl.BlockSpec((1, 3, 128, 256), lambda b, g, i: (g, 0, 0, 0))]
    args += [tables["wcs"], tables["eg"]]
    return pl.pallas_call(
        functools.partial(_nsa_kernel, tq=tq, tile_base=tile_base, single_shot=single_shot, ncp=ncp, nbp=nbp),
        grid=(batch, NSA_G, n_qt),
        in_specs=in_specs,
        out_specs=pl.BlockSpec((1, tq, 256), lambda b, g, i: (b, i, g)),
        out_shape=jax.ShapeDtypeStruct((batch, sq, 512), F32),
        scratch_shapes=scratch,
        compiler_params=_cparams(("arbitrary", "arbitrary", "arbitrary")),
        name="nsa",
    )(*args)


def _t5_bucket(dist):
    d = jnp.maximum(dist, 0)
    exact = REL_BUCKETS // 2
    far = exact + (jnp.log(jnp.maximum(d, 1).astype(F32) / exact)
                   / math.log(REL_MAX_DIST / exact) * (REL_BUCKETS - exact)).astype(jnp.int32)
    return jnp.where(d < exact, d, jnp.minimum(far, REL_BUCKETS - 1))


def _nsa_tables(rel_bias, tq, ncp, nbp, skv=None):
    dmax = WINDOW + NSA_TQ + 1
    by_dist = (rel_bias[_t5_bucket(jnp.arange(dmax))] - rel_bias[REL_BUCKETS - 1][None, :]) * LOG2E

    def toeplitz(w, off, lo, hi):
        n = w + tq
        k = np.arange(n)
        k = np.where(k < w, k, k - n)
        d = off - k
        g = jnp.where(jnp.asarray((d >= lo) & (d <= hi))[:, None], by_dist[np.clip(d, 0, dmax - 1)], NEG)
        flat = jnp.tile(g.T, (1, tq))[:, :tq * (n - 1)]
        t = flat.reshape(NSA_HEADS, tq, n - 1)[:, :, :w]
        return t.reshape(NSA_G, NSA_HPG * tq, w)

    tw = toeplitz(WIN_BAND, WINDOW, 0, WINDOW)
    ts = toeplitz(SEL_NEAR, NSA_TK, 0, dmax)
    tc = toeplitz(CMP_STRIDE * CMP_NEAR, NSA_TQ - CMP_BLOCK + 1, 0, dmax)[:, :, ::CMP_STRIDE]
    tc_hi = tc.astype(BF16)
    tc_lo = (tc - tc_hi.astype(F32)).astype(BF16)
    tables = {"tw": tw, "ts": ts, "tc": jnp.concatenate([tc_hi, tc_lo], axis=2)}

    m = np.arange(LANE)[:, None]
    if skv is None:
        key = np.arange(NSA_TK)[None, None, :]
        r = np.arange(32)[:, None, None]
        tables["etab"] = jnp.asarray((m[None] == 4 * r + key // SEL_BLOCK).astype(np.float32), BF16)
    else:
        slab = min(SEL_SLAB, skv)
        tables["emain"] = jnp.asarray((m == np.arange(slab)[None, :] // SEL_BLOCK).astype(np.float32), BF16)
        o = (skv // SEL_BLOCK) % LANE
        tables["etail"] = jnp.asarray((m == o + np.arange(NSA_TK)[None, :] // SEL_BLOCK).astype(np.float32), BF16)

    c0 = np.arange(ncp)[:, None] * CMP_STRIDE
    s0 = np.arange(nbp)[None, :] * SEL_BLOCK
    shared = np.minimum(c0 + CMP_BLOCK, s0 + SEL_BLOCK) - np.maximum(c0, s0)
    wcs = np.maximum(shared, 0).astype(np.float32) / CMP_BLOCK
    tables["wcs"] = jnp.asarray(wcs.T if skv is None else wcs, BF16)

    eg = np.zeros((NSA_G, 3, 128, 256), np.float32)
    for g in range(NSA_G):
        for br in range(3):
            for hh in range(NSA_HPG):
                eg[g, br, 3 * (NSA_HPG * g + hh) + br, 64 * hh:64 * hh + 64] = 1.0
    tables["eg"] = jnp.asarray(eg, BF16)
    return tables


def _foxdec_kernel(pt_ref, qbd_ref, knew_ref, vnew_ref, lfnew_ref, u_ref, *refs):
    npg = FOX_DEC_PAGES
    kv_refs = refs[:npg]
    lf_refs = refs[npg:2 * npg]
    o_ref, m_ref, l_ref, acc_ref, carry_ref = refs[2 * npg:]
    c = pl.program_id(1)
    nrow = FOX_HEADS * 8

    @pl.when(c == 0)
    def _():
        m_ref[...] = jnp.full(m_ref.shape, M_INIT, F32)
        l_ref[...] = jnp.zeros(l_ref.shape, F32)
        acc_ref[...] = jnp.zeros(acc_ref.shape, F32)
        carry_ref[...] = jnp.zeros(carry_ref.shape, F32)

    qbd = qbd_ref[0]
    u = u_ref[...]

    def cum_blocks(lf_blocks):
        lf = lf_blocks[0] if len(lf_blocks) == 1 else jnp.concatenate(lf_blocks, axis=0)
        res = sum(_dot(part, u) for part in _split3(lf))
        run = carry_ref[...]
        out = []
        for k in range(len(lf_blocks)):
            out.append((res[8 * k:8 * k + 8, 0:LANE] + run) * LOG2E)
            run = run + res[8 * k:8 * k + 8, LANE:2 * LANE]
        carry_ref[...] = run
        return out

    def biased(s, ck):
        return (s.reshape(FOX_HEADS, 8, s.shape[1]) - ck[:, None, :]).reshape(nrow, s.shape[1])

    kt = jnp.concatenate([kv_refs[k][0, 0].reshape(512, PAGE) for k in range(npg)], axis=1).astype(BF16)
    vt = jnp.concatenate([kv_refs[k][0, 1].reshape(512, PAGE) for k in range(npg)], axis=1).astype(BF16)
    ck = jnp.concatenate(cum_blocks([lf_refs[k][0] for k in range(npg)]), axis=1)
    t = biased(_dot(qbd, kt), ck)
    tk = t.shape[1]
    m_prev = m_ref[...]
    m_new = jnp.maximum(m_prev, jnp.max(t, axis=1, keepdims=True))
    alpha = jnp.exp2(m_prev - m_new)
    p = jnp.exp2(t - jnp.tile(m_new, (1, tk // LANE)))
    l_ref[...] = alpha * l_ref[...] + jnp.sum(p, axis=1, keepdims=True)
    acc_ref[...] = acc_ref[...] * jnp.tile(alpha, (1, 4)) + _dot_nt(p.astype(BF16), vt)
    m_ref[...] = m_new

    @pl.when(c == pl.num_programs(1) - 1)
    def _():
        t2 = biased(_dot(qbd, knew_ref[0]), cum_blocks([lfnew_ref[0]])[0])
        row = lax.broadcasted_iota(jnp.int32, (nrow, LANE), 0) % 8
        col = lax.broadcasted_iota(jnp.int32, (nrow, LANE), 1)
        t2 = jnp.where(col <= row, t2, NEG)
        m_prev2 = m_ref[...]
        m_new2 = jnp.maximum(m_prev2, jnp.max(t2, axis=1, keepdims=True))
        alpha2 = jnp.exp2(m_prev2 - m_new2)
        p2 = jnp.exp2(t2 - m_new2)
        l2 = alpha2 * l_ref[...] + jnp.sum(p2, axis=1, keepdims=True)
        acc2 = acc_ref[...] * jnp.tile(alpha2, (1, 4)) + _dot_nt(p2.astype(BF16), vnew_ref[0])
        o = acc2 / jnp.tile(l2, (1, 4))
        for h in range(FOX_HEADS):
            o_ref[0, :, 64 * h:64 * h + 64] = o[8 * h:8 * h + 8, 64 * h:64 * h + 64]


def _fox_decode(page_table, qbd, knew_t, vnew_t, lfnew, kv_pool, lft_pool):
    batch, n_pages = page_table.shape
    npg = FOX_DEC_PAGES
    u = jnp.asarray(np.concatenate([np.triu(np.ones((LANE, LANE), np.float32)),
                                    np.ones((LANE, LANE), np.float32)], axis=1), BF16)

    def seq(shape):
        return pl.BlockSpec(shape, lambda b, c, pt: (b,) + tuple(0 for _ in shape[1:]))

    def page(shape, k):
        return pl.BlockSpec(shape, lambda b, c, pt, k=k: (pt[b, c * npg + k],) + tuple(0 for _ in shape[1:]))

    in_specs = ([seq((1, 64, 512)), seq((1, 512, LANE)), seq((1, 512, LANE)), seq((1, 8, LANE)),
                 pl.BlockSpec((LANE, 2 * LANE), lambda b, c, pt: (0, 0))]
                + [page((1, 2, FOX_HEADS, HEAD_DIM, PAGE), k) for k in range(npg)]
                + [page((1, 8, PAGE), k) for k in range(npg)])
    grid_spec = pltpu.PrefetchScalarGridSpec(
        num_scalar_prefetch=1, grid=(batch, n_pages // npg), in_specs=in_specs,
        out_specs=seq((1, 8, 512)),
        scratch_shapes=[pltpu.VMEM((64, LANE), F32), pltpu.VMEM((64, LANE), F32),
                        pltpu.VMEM((64, 512), F32), pltpu.VMEM((8, LANE), F32)])
    return pl.pallas_call(
        _foxdec_kernel,
        grid_spec=grid_spec,
        out_shape=jax.ShapeDtypeStruct((batch, 8, 512), F32),
        compiler_params=_cparams(("arbitrary", "arbitrary")),
        name="foxdec",
    )(page_table, qbd, knew_t, vnew_t, lfnew, u, *([kv_pool] * npg), *([lft_pool] * npg))


def _outmlp_kernel(x_ref, oa_ref, ob_ref, gm_ref, m2_ref, m3_ref, m4_ref, m5_ref, g1_ref,
                   wf_ref, wn_ref, wo_ref, wu_ref, wd_ref, y_ref, y1_ref, h2_ref, acc_ref):
    c = pl.program_id(1)

    @pl.when(c == 0)
    def _():
        ya = _dot(oa_ref[...].astype(BF16), wf_ref[...])
        yb = _dot(ob_ref[...].astype(BF16), wn_ref[...])
        u = gm_ref[:, 0:D_MODEL] * ya + gm_ref[:, D_MODEL:2 * D_MODEL] * yb
        att = _dot(u.astype(BF16), wo_ref[...])
        y1 = x_ref[...] + m2_ref[...] * att
        y1_ref[...] = y1
        ms = jnp.mean(y1 * y1, axis=-1, keepdims=True)
        hn = y1 * lax.rsqrt(ms + EPS) * g1_ref[...]
        h2_ref[...] = (hn * (1.0 + m4_ref[...]) + m3_ref[...]).astype(BF16)
        acc_ref[...] = jnp.zeros(acc_ref.shape, F32)

    up = _dot(h2_ref[...], wu_ref[...])
    act = jnp.square(jnp.maximum(up, 0.0)).astype(BF16)
    acc_ref[...] += _dot(act, wd_ref[...])

    @pl.when(c == pl.num_programs(1) - 1)
    def _():
        y_ref[...] = y1_ref[...] + m5_ref[...] * acc_ref[...]


def _out_mlp(x, oa, ob, gm, mods, g1, wf, wn, wo, wu, wd, tr):
    rows = x.shape[0]
    ffc = 1024
    mrows = mods[0].shape[0]
    mod_spec = (pl.BlockSpec((1, D_MODEL), lambda i, c: (0, 0)) if mrows == 1
                else pl.BlockSpec((tr, D_MODEL), lambda i, c: (i, 0)))
    row = lambda w: pl.BlockSpec((tr, w), lambda i, c: (i, 0))
    const = lambda shape: pl.BlockSpec(shape, lambda i, c: (0, 0))
    return pl.pallas_call(
        _outmlp_kernel,
        grid=(rows // tr, D_FF // ffc),
        in_specs=[row(D_MODEL), row(512), row(512), row(2048), mod_spec, mod_spec, mod_spec, mod_spec,
                  const((1, D_MODEL)), const((512, D_MODEL)), const((512, D_MODEL)),
                  const((D_MODEL, D_MODEL)),
                  pl.BlockSpec((D_MODEL, ffc), lambda i, c: (0, c)),
                  pl.BlockSpec((ffc, D_MODEL), lambda i, c: (c, 0))],
        out_specs=row(D_MODEL),
        out_shape=jax.ShapeDtypeStruct((rows, D_MODEL), F32),
        scratch_shapes=[pltpu.VMEM((tr, D_MODEL), F32), pltpu.VMEM((tr, D_MODEL), BF16),
                        pltpu.VMEM((tr, D_MODEL), F32)],
        compiler_params=_cparams(("arbitrary", "arbitrary")),
        name="outmlp",
    )(x, oa, ob, gm, *mods, g1, wf, wn, wo, wu, wd)


def _prep_w_in(w_in):
    qa, ka, va, zf, qb, zkv, zg, zm = jnp.split(
        w_in, np.cumsum([512, 512, 512, 8, 512, 768, 24, 2048])[:-1].tolist(), axis=1)
    pad = lambda w: jnp.pad(w, ((0, 0), (0, 128 - w.shape[1])))
    return jnp.concatenate([qa, ka, va, qb, zkv, pad(zf), pad(zg), zm], axis=1).astype(BF16)


def _prep_cmp_weights(pe_cmp, w_cmp1, w_cmp2):
    eye = jnp.eye(NSA_G, dtype=F32)
    w1 = w_cmp1.reshape(2, 2, CMP_STRIDE, HEAD_DIM, CMP_HIDDEN)
    w1big = jnp.einsum('khrdj,gq->khrgdqj', w1, eye).reshape(
        2, 2, CMP_STRIDE * NSA_G * HEAD_DIM, NSA_G * CMP_HIDDEN)
    w2big = jnp.einsum('kjd,gq->kgjqd', w_cmp2, eye).reshape(2, NSA_G * CMP_HIDDEN, NSA_G * HEAD_DIM)
    pe = pe_cmp.reshape(2, 2, CMP_STRIDE, 1, HEAD_DIM)
    pe = jnp.broadcast_to(pe, (2, 2, CMP_STRIDE, NSA_G, HEAD_DIM)).reshape(2, 2, 1, CMP_STRIDE * NSA_G * HEAD_DIM)
    return pe, w1big.astype(BF16), w2big.astype(BF16)


def _block_diag_mean(n):
    r = np.arange(n)
    return jnp.asarray((r[:, None] // HEAD_DIM == r[None, :] // HEAD_DIM).astype(np.float32) / HEAD_DIM, BF16)


def kernel(x_prompt, x_sample, cache_fox_kv, cache_fox_logf, cache_nsa_kv, state_nsa_win, page_table,
           c_prompt, c_sample, w_ada, b_ada, g_norm, w_in, b_forget, g_qk_fox, g_qk_nsa,
           pe_cmp, w_cmp1, w_cmp2, rel_bias, w_out_fox, w_out_nsa, w_out, w_up, w_down):
    n_batch, seq, _ = x_prompt.shape
    dec_batch, dec_seq, _ = x_sample.shape
    n_pages = page_table.shape[1]
    past = n_pages * PAGE
    assert n_batch == 1 and dec_seq == 8 and w_ada.shape[0] == 1
    assert seq % PREP_ROWS == 0 and past % PREP_ROWS == 0 and state_nsa_win.shape[2] == WINDOW

    n_c = 1 + dec_batch
    c_rows = -(-n_c // 8) * 8
    c_all = jnp.pad(jnp.concatenate([c_prompt, c_sample], axis=0), ((0, c_rows - n_c), (0, 0)))
    mods = _ada(c_all, w_ada[0], b_ada[0])
    mod_p = [mods[0:1, k * D_MODEL:(k + 1) * D_MODEL] for k in range(6)]
    mod_s = [jnp.repeat(mods[1:n_c, k * D_MODEL:(k + 1) * D_MODEL], dec_seq, axis=0) for k in range(6)]

    w_cat = _prep_w_in(w_in[0])
    bd256 = _block_diag_mean(256)
    tile8 = lambda g: jnp.tile(g, 8)
    gq = jnp.concatenate([tile8(g_qk_fox[0, 0]), tile8(g_qk_fox[0, 1]), tile8(g_qk_nsa[0, 0]),
                          jnp.tile(g_qk_nsa[0, 2], 2), jnp.tile(g_qk_nsa[0, 3], 2)]).reshape(1, 1792)
    bf_pad = jnp.pad(b_forget[0], (0, 120)).reshape(1, 128)
    g0 = g_norm[0, 0].reshape(1, D_MODEL)
    xp = x_prompt.reshape(seq, D_MODEL)
    xs = x_sample.reshape(dec_batch * dec_seq, D_MODEL)
    (p_foxkv, p_nsakv, p_kvwin, p_lft, p_qa, p_kat, p_va, p_qb, p_kwt, p_vwt, p_gb, p_gm) = _inproj(
        xp, mod_p[0], mod_p[1], g0, w_cat, bd256, gq, bf_pad)
    (s_foxkv, s_nsakv, s_kvwin, s_lft, s_qa, s_kat, s_va, s_qb, s_kwt, s_vwt, s_gb, s_gm) = _inproj(
        xs, mod_s[0], mod_s[1], g0, w_cat, bd256, gq, bf_pad)

    ck = _cumsum(p_lft)
    ck_rep = jnp.broadcast_to(ck[:, :, None, :], (FOX_HEADS, seq // LANE, 8, LANE))
    va1 = _with_ones(p_va.reshape(seq, FOX_HEADS, HEAD_DIM), 2).reshape(seq, FOX_HEADS * 2 * HEAD_DIM)
    oa_p = _fox_prompt(p_qa, p_kat, va1, ck_rep)

    pe, w1big, w2big = _prep_cmp_weights(pe_cmp[0], w_cmp1[0], w_cmp2[0])
    bd128 = _block_diag_mean(128)
    gkc = jnp.tile(g_qk_nsa[0, 1], 2).reshape(1, 128)
    look_blocks = PREP_ROWS // CMP_STRIDE
    last_look_p = seq // CMP_STRIDE - 1
    kct_p, vct_p, kst_p, vst_p = _nsaprep(
        [p_nsakv], [pl.BlockSpec((PREP_ROWS, 512), lambda b, t: (t, 0))],
        p_nsakv, pl.BlockSpec((CMP_STRIDE, 512), lambda b, t: (jnp.minimum((t + 1) * look_blocks, last_look_p), 0)),
        (1, seq // PREP_ROWS), 1, seq, (), pe, w1big, w2big, bd128, gkc, False)
    tables_p = _nsa_tables(rel_bias, NSA_TQ, seq // CMP_STRIDE, max(seq // SEL_BLOCK, LANE))
    kwt_pad = jnp.pad(p_kwt, ((0, 0), (WINDOW, 0)))[None]
    vwt_pad = jnp.pad(_with_ones(p_vwt.reshape(NSA_G, HEAD_DIM, seq), 1), ((0, 0), (0, 0), (WINDOW, 0)))[None]
    ob_p = _nsa_attention(p_qb[None], kct_p, vct_p, kst_p, vst_p, None, kwt_pad, vwt_pad, p_gb[None],
                          tables_p, NSA_TQ, 0)[0]

    kv_pool = jnp.transpose(cache_fox_kv[0], (0, 2, 3, 4, 1))
    lft_pool = jnp.transpose(cache_fox_logf[0], (0, 2, 1))
    eye_h = jnp.eye(FOX_HEADS, dtype=BF16)
    qa_s = s_qa.reshape(dec_batch, dec_seq, FOX_HEADS, HEAD_DIM)
    qbd = jnp.einsum('bjhd,hk->bhjkd', qa_s, eye_h).reshape(dec_batch, 64, 512)

    def new_cols(a_t):
        a = jnp.transpose(a_t.reshape(a_t.shape[0], dec_batch, dec_seq), (1, 0, 2))
        return jnp.pad(a, ((0, 0), (0, 0), (0, LANE - dec_seq)))

    knew_t = new_cols(s_kat)
    vnew_t = new_cols(jnp.transpose(s_va))
    lfnew = new_cols(s_lft)
    oa_s = _fox_decode(page_table, qbd, knew_t, vnew_t, lfnew, kv_pool, lft_pool)

    nsa_pool = jnp.transpose(cache_nsa_kv[0], (0, 2, 3, 4, 1))
    npg = NSA_DEC_PAGES
    page_shape = (1, 4, NSA_G, HEAD_DIM, PAGE)
    part_specs = [pl.BlockSpec(page_shape, lambda b, t, pt, k=k: (pt[b, t * npg + k], 0, 0, 0, 0))
                  for k in range(npg)]
    look_spec = pl.BlockSpec(
        page_shape, lambda b, t, pt: (pt[b, jnp.minimum((t + 1) * npg, n_pages - 1)], 0, 0, 0, 0))
    kct_s, vct_s, kst_s, vst_s = _nsaprep(
        [nsa_pool] * npg, part_specs, nsa_pool, look_spec,
        (dec_batch, past // PREP_ROWS), dec_batch, past, (page_table,), pe, w1big, w2big, bd128, gkc, True)
    tile_base = past // NSA_TQ
    nb_dec = past // SEL_BLOCK + NSA_TK // SEL_BLOCK
    tables_s = _nsa_tables(rel_bias, dec_seq, past // CMP_STRIDE, -(-nb_dec // LANE) * LANE, skv=past)
    new_nsa_t = jnp.transpose(s_nsakv.reshape(dec_batch, dec_seq, 512), (0, 2, 1))
    pad_tail = lambda a: jnp.pad(a, ((0, 0), (0, 0), (0, NSA_TK - dec_seq))).astype(BF16)
    kst_tail = pad_tail(new_nsa_t[:, 256:384])
    vst_tail = _with_ones(pad_tail(new_nsa_t[:, 384:512]).reshape(dec_batch, NSA_G, HEAD_DIM, NSA_TK), 2)
    win_t = jnp.transpose(state_nsa_win[0], (0, 2, 3, 4, 1)).reshape(dec_batch, 256, WINDOW)
    new_win_t = jnp.transpose(s_kvwin.reshape(dec_batch, dec_seq, 256), (0, 2, 1))
    band_t = jnp.concatenate([win_t, new_win_t], axis=2)
    band_pad = jnp.pad(band_t, ((0, 0), (0, 0), (0, WIN_BAND - band_t.shape[2]))).astype(BF16)
    ob_s = _nsa_attention(s_qb.reshape(dec_batch, dec_seq, 512), kct_s, vct_s, kst_s, vst_s,
                          (kst_tail, vst_tail), band_pad[:, 0:128],
                          _with_ones(band_pad[:, 128:256].reshape(dec_batch, NSA_G, HEAD_DIM, WIN_BAND), 2),
                          s_gb.reshape(dec_batch, dec_seq, 128), tables_s, dec_seq, tile_base)

    g1 = g_norm[0, 1].reshape(1, D_MODEL)
    wf, wn, wo = w_out_fox[0].astype(BF16), w_out_nsa[0].astype(BF16), w_out[0].astype(BF16)
    wu, wd = w_up[0].astype(BF16), w_down[0].astype(BF16)
    y_p = _out_mlp(xp, oa_p, ob_p, p_gm, mod_p[2:6], g1, wf, wn, wo, wu, wd, 512)
    y_s = _out_mlp(xs, oa_s.reshape(dec_batch * dec_seq, 512), ob_s.reshape(dec_batch * dec_seq, 512),
                   s_gm, mod_s[2:6], g1, wf, wn, wo, wu, wd, dec_batch * dec_seq)

    win_keep = min(WINDOW, seq)
    new_win_sample = jnp.transpose(band_t[:, :, dec_seq:].reshape(dec_batch, 2, NSA_G, HEAD_DIM, WINDOW),
                                   (0, 4, 1, 2, 3))[None]
    return (
        y_p.reshape(1, seq, D_MODEL),
        y_s.reshape(dec_batch, dec_seq, D_MODEL),
        p_foxkv.reshape(1, 1, seq, 2, FOX_HEADS, HEAD_DIM),
        s_foxkv.reshape(1, dec_batch, dec_seq, 2, FOX_HEADS, HEAD_DIM),
        jnp.transpose(p_lft).reshape(1, 1, seq, FOX_HEADS),
        jnp.transpose(s_lft).reshape(1, dec_batch, dec_seq, FOX_HEADS),
        p_nsakv.reshape(1, 1, seq, 4, NSA_G, HEAD_DIM),
        s_nsakv.reshape(1, dec_batch, dec_seq, 4, NSA_G, HEAD_DIM),
        p_kvwin[seq - win_keep:].reshape(1, 1, win_keep, 2, NSA_G, HEAD_DIM),
        new_win_sample,
    )
```

```python
import functools
import math

import numpy as np
import jax
import jax.numpy as jnp
from jax import lax
from jax.experimental import pallas as pl
from jax.experimental.pallas import tpu as pltpu

F32 = jnp.float32
BF16 = jnp.bfloat16

D_MODEL = 1024
HEAD_DIM = 64
FOX_HEADS = 8
NSA_HEADS = 8
NSA_G = 2
NSA_HPG = 4
CMP_BLOCK = 32
CMP_STRIDE = 16
CMP_HIDDEN = 128
SEL_BLOCK = 64
N_SEL = 16
N_FORCED = 3
WINDOW = 512
REL_BUCKETS = 32
REL_MAX_DIST = 128
PAGE = 128
EPS = 1e-6
D_FF = 4 * D_MODEL

LANE = 128
LOG2E = 1.4426950408889634
QSCALE = LOG2E / math.sqrt(HEAD_DIM)
NEG = -float(2 ** 100)
M_INIT = -1e30
VMEM_LIMIT = 58 * 1024 * 1024

NSA_TQ = 256
NSA_TK = 256
NSA_FAR = 1024
WIN_BAND = WINDOW + NSA_TQ
SEL_NEAR = 2 * NSA_TK
CMP_NEAR = 32
SEL_SLAB = LANE * SEL_BLOCK
FOX_TQ = 1024
FOX_FAR = 1024
FOX_STRIPS = 4
PREP_ROWS = 2048
FOX_DEC_PAGES = 32
NSA_DEC_PAGES = PREP_ROWS // PAGE


def _cparams(sem):
    return pltpu.CompilerParams(dimension_semantics=sem, vmem_limit_bytes=VMEM_LIMIT)


def _split2(x):
    hi = x.astype(BF16)
    lo = (x - hi.astype(F32)).astype(BF16)
    return hi, lo


def _split3(x):
    hi = x.astype(BF16)
    r = x - hi.astype(F32)
    mid = r.astype(BF16)
    lo = (r - mid.astype(F32)).astype(BF16)
    return hi, mid, lo


def _dot(a, b):
    return jnp.dot(a, b, preferred_element_type=F32)


def _dot_nt(a, b):
    return lax.dot_general(a, b, (((1,), (1,)), ((), ())), preferred_element_type=F32)


def _ada_kernel(c_ref, w_ref, b_ref, o_ref):
    c = c_ref[...]
    s = c * jax.nn.sigmoid(c)
    o_ref[...] = jnp.dot(s, w_ref[...], precision=lax.Precision.HIGHEST,
                         preferred_element_type=F32) + b_ref[...]


def _ada(c_all, w_ada, b_ada):
    rows = c_all.shape[0]
    n = w_ada.shape[1]
    tn = 1536
    return pl.pallas_call(
        _ada_kernel,
        grid=(n // tn,),
        in_specs=[pl.BlockSpec((rows, D_MODEL), lambda j: (0, 0)),
                  pl.BlockSpec((D_MODEL, tn), lambda j: (0, j)),
                  pl.BlockSpec((1, tn), lambda j: (0, j))],
        out_specs=pl.BlockSpec((rows, tn), lambda j: (0, j)),
        out_shape=jax.ShapeDtypeStruct((rows, n), F32),
        compiler_params=_cparams(("arbitrary",)),
        name="ada",
    )(c_all, w_ada, b_ada.reshape(1, n))


_C_QA, _C_KA, _C_VA, _C_QB = 0, 512, 1024, 1536
_C_KV = 2048
_C_F = 2816
_C_G = 2944
_C_M = 3072
_C_END = 5120


def _inproj_kernel(x_ref, sh_ref, sc_ref, g0_ref, w_ref, bd_ref, gq_ref, bf_ref,
                   foxkv_ref, nsakv_ref, kvwin_ref, lft_ref, qa_ref, kat_ref, va_ref,
                   qb_ref, kwt_ref, vwt_ref, gb_ref, gm_ref):
    x = x_ref[...]
    ms = jnp.mean(x * x, axis=-1, keepdims=True)
    y = x * lax.rsqrt(ms + EPS) * g0_ref[...]
    h = y * (1.0 + sc_ref[...]) + sh_ref[...]
    hb = h.astype(BF16)
    bd = bd_ref[...]

    def head_norm(z, gain):
        zz = (z * z).astype(BF16)
        w = z.shape[1]
        if w == LANE:
            msq = _dot(zz, bd[:LANE, :LANE])
        else:
            msq = jnp.concatenate([_dot(zz[:, k:k + 256], bd) for k in range(0, w, 256)], axis=1)
        return z * lax.rsqrt(msq + EPS) * gain

    za = _dot(hb, w_ref[:, _C_QA:_C_KV])
    qa = head_norm(za[:, _C_QA:_C_KA], gq_ref[:, 0:512])
    ka = head_norm(za[:, _C_KA:_C_VA], gq_ref[:, 512:1024])
    va = za[:, _C_VA:_C_QB]
    qb = head_norm(za[:, _C_QB:_C_KV], gq_ref[:, 1024:1536])
    foxkv_ref[:, 0:512] = ka
    foxkv_ref[:, 512:1024] = va
    qa_ref[...] = (qa * QSCALE).astype(BF16)
    kat_ref[...] = ka.T.astype(BF16)
    va_ref[...] = va.astype(BF16)
    qb_ref[...] = (qb * QSCALE).astype(BF16)

    zb = _dot(hb, w_ref[:, _C_KV:_C_M])
    ksel = head_norm(zb[:, 256:384], gq_ref[:, 1536:1664])
    kwin = head_norm(zb[:, 512:640], gq_ref[:, 1664:1792])
    vwin = zb[:, 640:768]
    nsakv_ref[:, 0:256] = zb[:, 0:256]
    nsakv_ref[:, 256:384] = ksel
    nsakv_ref[:, 384:512] = zb[:, 384:512]
    kvwin_ref[:, 0:128] = kwin
    kvwin_ref[:, 128:256] = vwin
    kwt_ref[...] = kwin.T.astype(BF16)
    vwt_ref[...] = vwin.T.astype(BF16)
    zf = zb[:, 768:896] + bf_ref[...]
    lf = jnp.minimum(zf, 0.0) - jnp.log(1.0 + jnp.exp(-jnp.abs(zf)))
    lft_ref[...] = lf.T[0:8, :]
    gb_ref[...] = jax.nn.sigmoid(zb[:, 896:1024])

    zm = _dot(hb, w_ref[:, _C_M:_C_END])
    gm_ref[...] = jax.nn.sigmoid(zm)


def _inproj(x, shift, scale, g0, w_cat, bd, gq, bf_pad):
    rows = x.shape[0]
    tr = 256
    mrows = shift.shape[0]
    mod_spec = (pl.BlockSpec((1, D_MODEL), lambda i: (0, 0)) if mrows == 1
                else pl.BlockSpec((tr, D_MODEL), lambda i: (i, 0)))
    const = lambda shape: pl.BlockSpec(shape, lambda i: tuple(0 for _ in shape))
    row_spec = lambda w: pl.BlockSpec((tr, w), lambda i: (i, 0))
    col_spec = lambda h: pl.BlockSpec((h, tr), lambda i: (0, i))
    out_shape = (
        jax.ShapeDtypeStruct((rows, 1024), F32),
        jax.ShapeDtypeStruct((rows, 512), F32),
        jax.ShapeDtypeStruct((rows, 256), F32),
        jax.ShapeDtypeStruct((8, rows), F32),
        jax.ShapeDtypeStruct((rows, 512), BF16),
        jax.ShapeDtypeStruct((512, rows), BF16),
        jax.ShapeDtypeStruct((rows, 512), BF16),
        jax.ShapeDtypeStruct((rows, 512), BF16),
        jax.ShapeDtypeStruct((128, rows), BF16),
        jax.ShapeDtypeStruct((128, rows), BF16),
        jax.ShapeDtypeStruct((rows, 128), F32),
        jax.ShapeDtypeStruct((rows, 2048), F32),
    )
    out_specs = (row_spec(1024), row_spec(512), row_spec(256), col_spec(8), row_spec(512),
                 col_spec(512), row_spec(512), row_spec(512), col_spec(128), col_spec(128),
                 row_spec(128), row_spec(2048))
    return pl.pallas_call(
        _inproj_kernel,
        grid=(rows // tr,),
        in_specs=[row_spec(D_MODEL), mod_spec, mod_spec, const((1, D_MODEL)),
                  const((D_MODEL, _C_END)), const((256, 256)), const((1, 1792)), const((1, 128))],
        out_specs=out_specs,
        out_shape=out_shape,
        compiler_params=_cparams(("arbitrary",)),
        name="inproj",
    )(x, shift, scale, g0, w_cat, bd, gq, bf_pad)


def _cumsum_kernel(x_ref, u_ref, pick_ref, bt_ref, o_ref):
    u = u_ref[...]
    c2 = sum(_dot(p, u) for p in _split3(x_ref[...]))
    pick = pick_ref[...]
    totb = sum(_dot(p, pick) for p in _split3(c2))
    bt = bt_ref[...]
    offs = sum(_dot(bt, p) for p in _split3(totb))
    o_ref[...] = (c2 + offs) * LOG2E


def _cumsum(lft):
    heads, s = lft.shape
    nblk = s // LANE
    rows = heads * nblk
    x = lft.reshape(rows, LANE)
    u = jnp.asarray(np.triu(np.ones((LANE, LANE), np.float32)), BF16)
    pick = np.zeros((LANE, LANE), np.float32)
    pick[LANE - 1, :] = 1.0
    r = np.arange(rows)
    bt = ((r[:, None] // nblk == r[None, :] // nblk) & (r[None, :] < r[:, None])).astype(np.float32)
    full = lambda shape: pl.BlockSpec(shape, lambda i: (0, 0))
    out = pl.pallas_call(
        _cumsum_kernel,
        grid=(1,),
        in_specs=[full((rows, LANE)), full((LANE, LANE)), full((LANE, LANE)), full((rows, rows))],
        out_specs=full((rows, LANE)),
        out_shape=jax.ShapeDtypeStruct((rows, LANE), F32),
        compiler_params=_cparams(("arbitrary",)),
        name="cumsum",
    )(x, u, jnp.asarray(pick, BF16), jnp.asarray(bt, BF16))
    return out.reshape(heads, nblk, LANE)


def _online_update(t, v1, m_ref, acc_ref, v_transposed=False):
    tk = t.shape[1]
    m_prev = m_ref[...]
    m_new = jnp.maximum(m_prev, jnp.max(t, axis=1, keepdims=True))
    alpha = jnp.exp2(m_prev - m_new)
    p = jnp.exp2(t - jnp.tile(m_new, (1, tk // LANE))).astype(BF16)
    pv = _dot_nt(p, v1) if v_transposed else _dot(p, v1)
    acc_ref[...] = acc_ref[...] * alpha + pv
    m_ref[...] = m_new


def _normalize(acc):
    return acc[:, 0:HEAD_DIM] / acc[:, HEAD_DIM:2 * HEAD_DIM]


def _with_ones(v, axis):
    shape = list(v.shape)
    shape[axis] = HEAD_DIM
    return jnp.concatenate([v, jnp.ones(shape, v.dtype)], axis=axis)


def _fox_kernel(q_ref, kt_ref, v_ref, ck_ref, o_ref, m_ref, acc_ref):
    i = pl.program_id(1)
    rs = FOX_TQ // FOX_STRIPS
    chains = [(hh, st) for hh in range(2) for st in range(FOX_STRIPS)]
    m_ref[...] = jnp.full(m_ref.shape, M_INIT, F32)
    acc_ref[...] = jnp.zeros(acc_ref.shape, F32)

    def tile(koff, width, masked):
        blk0 = koff // LANE
        widths = [min(width, (st + 1) * rs) if masked else width for _, st in chains]
        nck = [-jnp.concatenate([ck_ref[hh, blk0 + c] for c in range(width // LANE)], axis=1) for hh in range(2)]
        scores = [_dot(q_ref[st * rs:(st + 1) * rs, 64 * hh:64 * hh + 64], kt_ref[64 * hh:64 * hh + 64, pl.ds(koff, w)])
                  + jnp.tile(nck[hh][:, 0:w], (rs // 8, 1))
                  for (hh, st), w in zip(chains, widths)]
        for (hh, st), w, t in zip(chains, widths, scores):
            if masked:
                row = st * rs + lax.broadcasted_iota(jnp.int32, (rs, w), 0)
                col = lax.broadcasted_iota(jnp.int32, (rs, w), 1)
                t = jnp.where(row >= col, t, NEG)
            rows = pl.ds(st * rs, rs)
            _online_update(t, v_ref[pl.ds(koff, w), LANE * hh:LANE * (hh + 1)], m_ref.at[hh, rows],
                           acc_ref.at[hh, rows])

    def body(j, carry):
        tile(pl.multiple_of(j * FOX_FAR, FOX_FAR), FOX_FAR, False)
        return carry

    lax.fori_loop(0, i * (FOX_TQ // FOX_FAR), body, 0)
    tile(pl.multiple_of(i * FOX_TQ, FOX_TQ), FOX_TQ, True)
    o_ref[...] = jnp.concatenate([_normalize(acc_ref[hh]) for hh in range(2)], axis=1)


def _fox_prompt(qa, kat, va1, ck_rep):
    s = qa.shape[0]
    nblk = s // LANE
    return pl.pallas_call(
        _fox_kernel,
        grid=(FOX_HEADS // 2, s // FOX_TQ),
        in_specs=[pl.BlockSpec((FOX_TQ, LANE), lambda hp, i: (i, hp)),
                  pl.BlockSpec((LANE, s), lambda hp, i: (hp, 0)),
                  pl.BlockSpec((s, 2 * LANE), lambda hp, i: (0, hp)),
                  pl.BlockSpec((2, nblk, 8, LANE), lambda hp, i: (hp, 0, 0, 0))],
        out_specs=pl.BlockSpec((FOX_TQ, LANE), lambda hp, i: (i, hp)),
        out_shape=jax.ShapeDtypeStruct((s, 512), F32),
        scratch_shapes=[pltpu.VMEM((2, FOX_TQ, LANE), F32), pltpu.VMEM((2, FOX_TQ, LANE), F32)],
        compiler_params=_cparams(("arbitrary", "arbitrary")),
        name="fox",
    )(qa, kat, va1, ck_rep)


def _gelu_tanh(x):
    return 0.5 * x * (1.0 + jnp.tanh(math.sqrt(2.0 / math.pi) * (x + 0.044715 * (x * x * x))))


def _nsaprep_kernel(*refs, n_parts, n_prefetch, feature_major):
    refs = refs[n_prefetch:]
    part_refs = refs[:n_parts]
    (look_ref, pe_ref, w1_ref, w2_ref, bd_ref, gkc_ref,
     kct_ref, vct_ref, kst_ref, vst_ref, xk_ref, xv_ref) = refs[n_parts:]
    xs_refs = (xk_ref, xv_ref)
    prow = PREP_ROWS // n_parts
    ntok = PREP_ROWS // CMP_STRIDE
    flat = CMP_STRIDE * LANE
    chunks = ([], [])
    for k in range(n_parts):
        lo, hi = k * prow, (k + 1) * prow
        if feature_major:
            blk = part_refs[k][...].reshape(4, LANE, prow)
            for kv in range(2):
                chunks[kv].append(blk[kv].T.reshape(prow // CMP_STRIDE, flat))
            kst_ref[0, :, lo:hi] = blk[2].astype(BF16)
            vsel_t = blk[3].astype(BF16)
        else:
            blk = part_refs[k][...]
            xk_ref[lo:hi, :] = blk[:, 0:128]
            xv_ref[lo:hi, :] = blk[:, 128:256]
            kst_ref[0, :, lo:hi] = blk[:, 256:384].T.astype(BF16)
            vsel_t = blk[:, 384:512].T.astype(BF16)
        for g in range(NSA_G):
            vst_ref[0, g, 0:HEAD_DIM, lo:hi] = vsel_t[HEAD_DIM * g:HEAD_DIM * (g + 1)]
            vst_ref[0, g, HEAD_DIM:2 * HEAD_DIM, lo:hi] = jnp.ones((HEAD_DIM, prow), BF16)
    outs = []
    if feature_major:
        look = look_ref[...].reshape(4, LANE, PAGE)
        for kv in range(2):
            chunks[kv].append(look[kv].T.reshape(PAGE // CMP_STRIDE, flat))
            xc = jnp.concatenate(chunks[kv], axis=0)
            first = _dot((xc[0:ntok] + pe_ref[kv, 0]).astype(BF16), w1_ref[kv, 0])
            second = _dot((xc + pe_ref[kv, 1]).astype(BF16), w1_ref[kv, 1])
            act = _gelu_tanh(first + second[1:ntok + 1]).astype(BF16)
            outs.append(_dot(act, w2_ref[kv]))
    else:
        look = look_ref[...]
        xk_ref[PREP_ROWS:PREP_ROWS + CMP_STRIDE, :] = look[:, 0:128]
        xv_ref[PREP_ROWS:PREP_ROWS + CMP_STRIDE, :] = look[:, 128:256]
        for kv in range(2):
            hid = jnp.zeros((ntok, NSA_G * CMP_HIDDEN), F32)
            for half in range(2):
                xcat = jnp.concatenate(
                    [xs_refs[kv][pl.ds(half * CMP_STRIDE + r, ntok, stride=CMP_STRIDE), :]
                     for r in range(CMP_STRIDE)], axis=1)
                xcat = (xcat + pe_ref[kv, half]).astype(BF16)
                hid = hid + _dot(xcat, w1_ref[kv, half])
            act = _gelu_tanh(hid).astype(BF16)
            outs.append(_dot(act, w2_ref[kv]))
    kc, vc = outs
    msq = _dot((kc * kc).astype(BF16), bd_ref[...])
    kc = kc * lax.rsqrt(msq + EPS) * gkc_ref[...]
    kct_ref[0] = kc.T.astype(BF16)
    vct_ref[0] = vc.T.astype(BF16)


def _nsaprep(parts_arrays, part_specs, look_array, look_spec, grid, batch, length, prefetch,
             pe, w1, w2, bd128, gkc, feature_major):
    n_parts = len(part_specs)
    ntok = PREP_ROWS // CMP_STRIDE
    ncp = length // CMP_STRIDE
    npf = len(prefetch)

    def cm(shape):
        return pl.BlockSpec(shape, lambda b, t, *pf: tuple(0 for _ in shape))

    in_specs = list(part_specs) + [look_spec, cm(pe.shape), cm(w1.shape), cm(w2.shape),
                                   cm(bd128.shape), cm(gkc.shape)]
    out_specs = (
        pl.BlockSpec((1, 128, ntok), lambda b, t, *pf: (b, 0, t)),
        pl.BlockSpec((1, 128, ntok), lambda b, t, *pf: (b, 0, t)),
        pl.BlockSpec((1, 128, PREP_ROWS), lambda b, t, *pf: (b, 0, t)),
        pl.BlockSpec((1, NSA_G, 128, PREP_ROWS), lambda b, t, *pf: (b, 0, 0, t)),
    )
    out_shape = (
        jax.ShapeDtypeStruct((batch, 128, ncp), BF16),
        jax.ShapeDtypeStruct((batch, 128, ncp), BF16),
        jax.ShapeDtypeStruct((batch, 128, length), BF16),
        jax.ShapeDtypeStruct((batch, NSA_G, 128, length), BF16),
    )
    grid_spec = pltpu.PrefetchScalarGridSpec(
        num_scalar_prefetch=npf, grid=grid, in_specs=in_specs, out_specs=out_specs,
        scratch_shapes=[pltpu.VMEM((PREP_ROWS + CMP_STRIDE, LANE), F32),
                        pltpu.VMEM((PREP_ROWS + CMP_STRIDE, LANE), F32)])
    return pl.pallas_call(
        functools.partial(_nsaprep_kernel, n_parts=n_parts, n_prefetch=npf, feature_major=feature_major),
        grid_spec=grid_spec,
        out_shape=out_shape,
        compiler_params=_cparams(("arbitrary", "arbitrary")),
        name="nsaprep",
    )(*prefetch, *parts_arrays, look_array, pe, w1, w2, bd128, gkc)


def _nsa_kernel(*refs, tq, tile_base, single_shot, ncp, nbp):
    if single_shot:
        (q_ref, kct_ref, vct_ref, kst_ref, vst_ref, kstt_ref, vstt_ref, kwt_ref, vwt_ref, g_ref,
         tw_ref, ts_ref, tc_ref, emain_ref, etail_ref, wcs_ref, eg_ref, o_ref) = refs
    else:
        (q_ref, kct_ref, vct_ref, kst_ref, vst_ref, kwt_ref, vwt_ref, g_ref,
         tw_ref, ts_ref, tc_ref, etab_ref, wcs_ref, eg_ref,
         o_ref, sel_ref, m_ref, acc_ref) = refs
    rows = NSA_HPG * tq
    i_abs = tile_base + pl.program_id(2)
    q0 = i_abs * NSA_TQ
    q4 = q_ref[0]
    qhs = [q4[:, 64 * hh:64 * hh + 64] for hh in range(NSA_HPG)]
    qs = jnp.concatenate(qhs, axis=0)

    n_iota = lax.broadcasted_iota(jnp.int32, (1, ncp), 1)
    base = jnp.where(n_iota >= 16 * i_abs + 16, NEG, 0.0)
    pr = lax.broadcasted_iota(jnp.int32, (2 * CMP_NEAR, ncp), 0) % CMP_NEAR
    pn = lax.broadcasted_iota(jnp.int32, (2 * CMP_NEAR, ncp), 1)
    place = jnp.where(pn == 16 * i_abs - 16 + pr, 1.0, 0.0).astype(BF16)
    kct = kct_ref[0]
    vct = vct_ref[0]
    psum = jnp.zeros((tq, ncp), F32)
    o_cmp = []
    cmp_logits = [_dot(qhs[hh], kct) + (_dot(tc_ref[0, hh * tq:(hh + 1) * tq, :], place) + base)
                  for hh in range(NSA_HPG)]
    for hh in range(NSA_HPG):
        s = cmp_logits[hh]
        m = jnp.max(s, axis=1, keepdims=True)
        p = jnp.exp2(s - m)
        l = jnp.sum(p, axis=1, keepdims=True)
        inv = jnp.where(m > 0.5 * NEG, 1.0 / l, 0.0)
        pn_ = p * inv
        psum = psum + pn_
        o_cmp.append(_dot_nt(pn_.astype(BF16), vct))
    wcs = wcs_ref[...]

    ax = 1 if single_shot else 0
    if single_shot:
        imp = sum(_dot(part, wcs) for part in _split2(psum))
        posq = q0 + lax.broadcasted_iota(jnp.int32, (tq, 1), 0)
        blk = lax.broadcasted_iota(jnp.int32, (tq, nbp), 1)
    else:
        imp = sum(_dot_nt(wcs, part) for part in _split2(psum))
        posq = q0 + lax.broadcasted_iota(jnp.int32, (1, tq), 1)
        blk = lax.broadcasted_iota(jnp.int32, (nbp, tq), 0)
    blk_f = blk.astype(F32)
    cur = posq // SEL_BLOCK
    forced = (blk == 0) | (blk == cur) | (blk == cur - 1)
    sel = jnp.where(forced, 0.0, NEG)
    score = jnp.where(forced | (blk * SEL_BLOCK > posq), -jnp.inf, imp)
    for _ in range(N_SEL - N_FORCED):
        mx = jnp.max(score, axis=ax, keepdims=True)
        cand = jnp.where(score == mx, blk_f, float(4 * nbp))
        amin = jnp.min(cand, axis=ax, keepdims=True)
        hit = blk_f == amin
        sel = jnp.where(hit & (mx > -jnp.inf), 0.0, sel)
        score = jnp.where(hit, -jnp.inf, score)
    if not single_shot:
        sel = sel.T

    if single_shot:
        skv = kst_ref.shape[2]
        slab = min(SEL_SLAB, skv)
        selb = sel.astype(BF16)
        emain = emain_ref[...]
        mb = jnp.concatenate([_dot(selb[:, LANE * c:LANE * (c + 1)], emain) for c in range(skv // slab)], axis=1)
        ct = (skv // SEL_BLOCK) // LANE
        mb_tail = _dot(selb[:, LANE * ct:LANE * (ct + 1)], etail_ref[...])
        near = ts_ref[0]
        s_main = _dot(qs, kst_ref[0])
        t_main = (s_main.reshape(NSA_HPG, tq, skv) + mb[None]).reshape(rows, skv)
        t_main = jnp.concatenate([t_main[:, :skv - NSA_TK], t_main[:, skv - NSA_TK:] + near[:, 0:NSA_TK]], axis=1)
        s_tail = _dot(qs, kstt_ref[0])
        t_tail = (s_tail.reshape(NSA_HPG, tq, NSA_TK) + mb_tail[None]).reshape(rows, NSA_TK) + near[:, NSA_TK:SEL_NEAR]
        m = jnp.maximum(jnp.max(t_main, axis=1, keepdims=True), jnp.max(t_tail, axis=1, keepdims=True))
        p_main = jnp.exp2(t_main - m).astype(BF16)
        p_tail = jnp.exp2(t_tail - m).astype(BF16)
        o_sel = _normalize(_dot_nt(p_main, vst_ref[0, 0]) + _dot_nt(p_tail, vstt_ref[0, 0]))
    else:
        sel_ref[...] = sel
        m_ref[...] = jnp.full(m_ref.shape, M_INIT, F32)
        acc_ref[...] = jnp.zeros(acc_ref.shape, F32)

        def sel_tile(koff, width, near_tbl):
            kt = kst_ref[0, :, pl.ds(koff, width)]
            vt = vst_ref[0, 0, :, pl.ds(koff, width)]
            mbs = []
            for u in range(width // NSA_TK):
                j = koff // NSA_TK + u
                slab_off = pl.multiple_of((j // 32) * LANE, LANE)
                sel128 = sel_ref[:, pl.ds(slab_off, LANE)]
                mbs.append(_dot(sel128.astype(BF16), etab_ref[j % 32]))
            mb = mbs[0] if len(mbs) == 1 else jnp.concatenate(mbs, axis=1)
            logits = []
            for hh in range(NSA_HPG):
                add = mb
                if near_tbl is not None:
                    wn = near_tbl.shape[1]
                    near_h = mb[:, width - wn:] + near_tbl[hh * tq:(hh + 1) * tq]
                    add = near_h if wn == width else jnp.concatenate([mb[:, :width - wn], near_h], axis=1)
                logits.append(_dot(qhs[hh], kt) + add)
            for hh in range(NSA_HPG):
                hrows = pl.ds(hh * tq, tq)
                _online_update(logits[hh], vt, m_ref.at[hrows], acc_ref.at[hrows], v_transposed=True)

        n_far_tiles = jnp.maximum(i_abs - 1, 0)
        n_big = n_far_tiles // (NSA_FAR // NSA_TK)
        rem = n_far_tiles % (NSA_FAR // NSA_TK)

        def far_body(j, carry):
            sel_tile(pl.multiple_of(j * NSA_FAR, NSA_FAR), NSA_FAR, None)
            return carry

        lax.fori_loop(0, n_big, far_body, 0)
        rem_off = n_big * NSA_FAR

        for r in range(NSA_FAR // NSA_TK):
            @pl.when((i_abs >= 1) & (rem == r))
            def _(r=r):
                sel_tile(pl.multiple_of(rem_off, NSA_FAR), r * NSA_TK + SEL_NEAR, ts_ref[0])

        @pl.when(i_abs == 0)
        def _():
            sel_tile(pl.multiple_of(i_abs * NSA_TK, NSA_TK), NSA_TK, ts_ref[0, :, NSA_TK:SEL_NEAR])
        o_sel = _normalize(acc_ref[...])

    band0 = pl.multiple_of((i_abs - tile_base if single_shot else i_abs) * NSA_TQ, NSA_TQ)
    ktw = kwt_ref[0, :, pl.ds(band0, WIN_BAND)]
    vtw = vwt_ref[0, 0, :, pl.ds(band0, WIN_BAND)]
    b_iota = lax.broadcasted_iota(jnp.int32, (1, WIN_BAND), 1)
    before_start = jnp.where(b_iota >= WINDOW - q0, 0.0, NEG)
    win_logits = [_dot(qhs[hh], ktw) + (tw_ref[0, hh * tq:(hh + 1) * tq, :] + before_start)
                  for hh in range(NSA_HPG)]
    o_win = []
    for hh in range(NSA_HPG):
        s = win_logits[hh]
        p = jnp.exp2(s - jnp.max(s, axis=1, keepdims=True)).astype(BF16)
        o_win.append(_normalize(_dot_nt(p, vtw)))

    def heads_to_lanes(o):
        return jnp.concatenate([o[hh * tq:(hh + 1) * tq] for hh in range(NSA_HPG)], axis=1)

    branches = (jnp.concatenate(o_cmp, axis=1), heads_to_lanes(o_sel), jnp.concatenate(o_win, axis=1))
    gparts = _split2(g_ref[0])
    out = jnp.zeros((tq, NSA_HPG * HEAD_DIM), F32)
    for br in range(3):
        gexp = sum(_dot(part, eg_ref[0, br]) for part in gparts)
        out = out + gexp * branches[br]
    o_ref[0] = out


def _nsa_attention(q, kct, vct, kst, vst, tail, kwt, vwt, gates, tables, tq, tile_base):
    batch, sq, _ = q.shape
    n_qt = sq // tq
    ncp = kct.shape[2]
    skv = kst.shape[2]
    lw = kwt.shape[2]
    rows = NSA_HPG * tq
    single_shot = tail is not None
    nbp = tables["wcs"].shape[1 if single_shot else 0]
    grp = lambda n: pl.BlockSpec((1, 64, n), lambda b, g, i: (b, g, 0))
    per_g = lambda w: pl.BlockSpec((1, rows, w), lambda b, g, i: (g, 0, 0))
    const = lambda a: pl.BlockSpec(a.shape, lambda b, g, i: tuple(0 for _ in a.shape))
    v1 = lambda n: pl.BlockSpec((1, 1, 2 * HEAD_DIM, n), lambda b, g, i: (b, g, 0, 0))
    in_specs = [pl.BlockSpec((1, tq, 256), lambda b, g, i: (b, i, g)), grp(ncp), grp(ncp), grp(skv), v1(skv)]
    args = [q, kct, vct, kst, vst]
    if single_shot:
        in_specs += [grp(NSA_TK), v1(NSA_TK)]
        args += list(tail)
    in_specs += [grp(lw), v1(lw), pl.BlockSpec((1, tq, 128), lambda b, g, i: (b, i, 0)),
                 per_g(WIN_BAND), per_g(SEL_NEAR), per_g(2 * CMP_NEAR)]
    args += [kwt, vwt, gates, tables["tw"], tables["ts"], tables["tc"]]
    if single_shot:
        in_specs += [const(tables["emain"]), const(tables["etail"])]
        args += [tables["emain"], tables["etail"]]
        scratch = []
    else:
        in_specs += [const(tables["etab"])]
        args += [tables["etab"]]
        scratch = [pltpu.VMEM((tq, nbp), F32), pltpu.VMEM((rows, LANE), F32), pltpu.VMEM((rows, LANE), F32)]
    in_specs += [const(tables["wcs"]), pl.BlockSpec((1, 3, 128, 256), lambda b, g, i: (g, 0, 0, 0))]
    args += [tables["wcs"], tables["eg"]]
    return pl.pallas_call(
        functools.partial(_nsa_kernel, tq=tq, tile_base=tile_base, single_shot=single_shot, ncp=ncp, nbp=nbp),
        grid=(batch, NSA_G, n_qt),
        in_specs=in_specs,
        out_specs=pl.BlockSpec((1, tq, 256), lambda b, g, i: (b, i, g)),
        out_shape=jax.ShapeDtypeStruct((batch, sq, 512), F32),
        scratch_shapes=scratch,
        compiler_params=_cparams(("arbitrary", "arbitrary", "arbitrary")),
        name="nsa",
    )(*args)


def _t5_bucket(dist):
    d = jnp.maximum(dist, 0)
    exact = REL_BUCKETS // 2
    far = exact + (jnp.log(jnp.maximum(d, 1).astype(F32) / exact)
                   / math.log(REL_MAX_DIST / exact) * (REL_BUCKETS - exact)).astype(jnp.int32)
    return jnp.where(d < exact, d, jnp.minimum(far, REL_BUCKETS - 1))


def _nsa_tables(rel_bias, tq, ncp, nbp, skv=None):
    dmax = WINDOW + NSA_TQ + 1
    by_dist = (rel_bias[_t5_bucket(jnp.arange(dmax))] - rel_bias[REL_BUCKETS - 1][None, :]) * LOG2E

    def toeplitz(w, off, lo, hi):
        n = w + tq
        k = np.arange(n)
        k = np.where(k < w, k, k - n)
        d = off - k
        g = jnp.where(jnp.asarray((d >= lo) & (d <= hi))[:, None], by_dist[np.clip(d, 0, dmax - 1)], NEG)
        flat = jnp.tile(g.T, (1, tq))[:, :tq * (n - 1)]
        t = flat.reshape(NSA_HEADS, tq, n - 1)[:, :, :w]
        return t.reshape(NSA_G, NSA_HPG * tq, w)

    tw = toeplitz(WIN_BAND, WINDOW, 0, WINDOW)
    ts = toeplitz(SEL_NEAR, NSA_TK, 0, dmax)
    tc = toeplitz(CMP_STRIDE * CMP_NEAR, NSA_TQ - CMP_BLOCK + 1, 0, dmax)[:, :, ::CMP_STRIDE]
    tc_hi = tc.astype(BF16)
    tc_lo = (tc - tc_hi.astype(F32)).astype(BF16)
    tables = {"tw": tw, "ts": ts, "tc": jnp.concatenate([tc_hi, tc_lo], axis=2)}

    m = np.arange(LANE)[:, None]
    if skv is None:
        key = np.arange(NSA_TK)[None, None, :]
        r = np.arange(32)[:, None, None]
        tables["etab"] = jnp.asarray((m[None] == 4 * r + key // SEL_BLOCK).astype(np.float32), BF16)
    else:
        slab = min(SEL_SLAB, skv)
        tables["emain"] = jnp.asarray((m == np.arange(slab)[None, :] // SEL_BLOCK).astype(np.float32), BF16)
        o = (skv // SEL_BLOCK) % LANE
        tables["etail"] = jnp.asarray((m == o + np.arange(NSA_TK)[None, :] // SEL_BLOCK).astype(np.float32), BF16)

    c0 = np.arange(ncp)[:, None] * CMP_STRIDE
    s0 = np.arange(nbp)[None, :] * SEL_BLOCK
    shared = np.minimum(c0 + CMP_BLOCK, s0 + SEL_BLOCK) - np.maximum(c0, s0)
    wcs = np.maximum(shared, 0).astype(np.float32) / CMP_BLOCK
    tables["wcs"] = jnp.asarray(wcs.T if skv is None else wcs, BF16)

    eg = np.zeros((NSA_G, 3, 128, 256), np.float32)
    for g in range(NSA_G):
        for br in range(3):
            for hh in range(NSA_HPG):
                eg[g, br, 3 * (NSA_HPG * g + hh) + br, 64 * hh:64 * hh + 64] = 1.0
    tables["eg"] = jnp.asarray(eg, BF16)
    return tables


def _foxdec_kernel(pt_ref, qbd_ref, knew_ref, vnew_ref, lfnew_ref, u_ref, *refs):
    npg = FOX_DEC_PAGES
    kv_refs = refs[:npg]
    lf_refs = refs[npg:2 * npg]
    o_ref, m_ref, l_ref, acc_ref, carry_ref = refs[2 * npg:]
    c = pl.program_id(1)
    nrow = FOX_HEADS * 8

    @pl.when(c == 0)
    def _():
        m_ref[...] = jnp.full(m_ref.shape, M_INIT, F32)
        l_ref[...] = jnp.zeros(l_ref.shape, F32)
        acc_ref[...] = jnp.zeros(acc_ref.shape, F32)
        carry_ref[...] = jnp.zeros(carry_ref.shape, F32)

    qbd = qbd_ref[0]
    u = u_ref[...]

    def cum_blocks(lf_blocks):
        lf = lf_blocks[0] if len(lf_blocks) == 1 else jnp.concatenate(lf_blocks, axis=0)
        res = sum(_dot(part, u) for part in _split3(lf))
        run = carry_ref[...]
        out = []
        for k in range(len(lf_blocks)):
            out.append((res[8 * k:8 * k + 8, 0:LANE] + run) * LOG2E)
            run = run + res[8 * k:8 * k + 8, LANE:2 * LANE]
        carry_ref[...] = run
        return out

    def biased(s, ck):
        return (s.reshape(FOX_HEADS, 8, s.shape[1]) - ck[:, None, :]).reshape(nrow, s.shape[1])

    kt = jnp.concatenate([kv_refs[k][0, 0].reshape(512, PAGE) for k in range(npg)], axis=1).astype(BF16)
    vt = jnp.concatenate([kv_refs[k][0, 1].reshape(512, PAGE) for k in range(npg)], axis=1).astype(BF16)
    ck = jnp.concatenate(cum_blocks([lf_refs[k][0] for k in range(npg)]), axis=1)
    t = biased(_dot(qbd, kt), ck)
    tk = t.shape[1]
    m_prev = m_ref[...]
    m_new = jnp.maximum(m_prev, jnp.max(t, axis=1, keepdims=True))
    alpha = jnp.exp2(m_prev - m_new)
    p = jnp.exp2(t - jnp.tile(m_new, (1, tk // LANE)))
    l_ref[...] = alpha * l_ref[...] + jnp.sum(p, axis=1, keepdims=True)
    acc_ref[...] = acc_ref[...] * jnp.tile(alpha, (1, 4)) + _dot_nt(p.astype(BF16), vt)
    m_ref[...] = m_new

    @pl.when(c == pl.num_programs(1) - 1)
    def _():
        t2 = biased(_dot(qbd, knew_ref[0]), cum_blocks([lfnew_ref[0]])[0])
        row = lax.broadcasted_iota(jnp.int32, (nrow, LANE), 0) % 8
        col = lax.broadcasted_iota(jnp.int32, (nrow, LANE), 1)
        t2 = jnp.where(col <= row, t2, NEG)
        m_prev2 = m_ref[...]
        m_new2 = jnp.maximum(m_prev2, jnp.max(t2, axis=1, keepdims=True))
        alpha2 = jnp.exp2(m_prev2 - m_new2)
        p2 = jnp.exp2(t2 - m_new2)
        l2 = alpha2 * l_ref[...] + jnp.sum(p2, axis=1, keepdims=True)
        acc2 = acc_ref[...] * jnp.tile(alpha2, (1, 4)) + _dot_nt(p2.astype(BF16), vnew_ref[0])
        o = acc2 / jnp.tile(l2, (1, 4))
        for h in range(FOX_HEADS):
            o_ref[0, :, 64 * h:64 * h + 64] = o[8 * h:8 * h + 8, 64 * h:64 * h + 64]


def _fox_decode(page_table, qbd, knew_t, vnew_t, lfnew, kv_pool, lft_pool):
    batch, n_pages = page_table.shape
    npg = FOX_DEC_PAGES
    u = jnp.asarray(np.concatenate([np.triu(np.ones((LANE, LANE), np.float32)),
                                    np.ones((LANE, LANE), np.float32)], axis=1), BF16)

    def seq(shape):
        return pl.BlockSpec(shape, lambda b, c, pt: (b,) + tuple(0 for _ in shape[1:]))

    def page(shape, k):
        return pl.BlockSpec(shape, lambda b, c, pt, k=k: (pt[b, c * npg + k],) + tuple(0 for _ in shape[1:]))

    in_specs = ([seq((1, 64, 512)), seq((1, 512, LANE)), seq((1, 512, LANE)), seq((1, 8, LANE)),
                 pl.BlockSpec((LANE, 2 * LANE), lambda b, c, pt: (0, 0))]
                + [page((1, 2, FOX_HEADS, HEAD_DIM, PAGE), k) for k in range(npg)]
                + [page((1, 8, PAGE), k) for k in range(npg)])
    grid_spec = pltpu.PrefetchScalarGridSpec(
        num_scalar_prefetch=1, grid=(batch, n_pages // npg), in_specs=in_specs,
        out_specs=seq((1, 8, 512)),
        scratch_shapes=[pltpu.VMEM((64, LANE), F32), pltpu.VMEM((64, LANE), F32),
                        pltpu.VMEM((64, 512), F32), pltpu.VMEM((8, LANE), F32)])
    return pl.pallas_call(
        _foxdec_kernel,
        grid_spec=grid_spec,
        out_shape=jax.ShapeDtypeStruct((batch, 8, 512), F32),
        compiler_params=_cparams(("arbitrary", "arbitrary")),
        name="foxdec",
    )(page_table, qbd, knew_t, vnew_t, lfnew, u, *([kv_pool] * npg), *([lft_pool] * npg))


def _outmlp_kernel(x_ref, oa_ref, ob_ref, gm_ref, m2_ref, m3_ref, m4_ref, m5_ref, g1_ref,
                   wf_ref, wn_ref, wo_ref, wu_ref, wd_ref, y_ref, y1_ref, h2_ref, acc_ref):
    c = pl.program_id(1)

    @pl.when(c == 0)
    def _():
        ya = _dot(oa_ref[...].astype(BF16), wf_ref[...])
        yb = _dot(ob_ref[...].astype(BF16), wn_ref[...])
        u = gm_ref[:, 0:D_MODEL] * ya + gm_ref[:, D_MODEL:2 * D_MODEL] * yb
        att = _dot(u.astype(BF16), wo_ref[...])
        y1 = x_ref[...] + m2_ref[...] * att
        y1_ref[...] = y1
        ms = jnp.mean(y1 * y1, axis=-1, keepdims=True)
        hn = y1 * lax.rsqrt(ms + EPS) * g1_ref[...]
        h2_ref[...] = (hn * (1.0 + m4_ref[...]) + m3_ref[...]).astype(BF16)
        acc_ref[...] = jnp.zeros(acc_ref.shape, F32)

    up = _dot(h2_ref[...], wu_ref[...])
    act = jnp.square(jnp.maximum(up, 0.0)).astype(BF16)
    acc_ref[...] += _dot(act, wd_ref[...])

    @pl.when(c == pl.num_programs(1) - 1)
    def _():
        y_ref[...] = y1_ref[...] + m5_ref[...] * acc_ref[...]


def _out_mlp(x, oa, ob, gm, mods, g1, wf, wn, wo, wu, wd, tr):
    rows = x.shape[0]
    ffc = 1024
    mrows = mods[0].shape[0]
    mod_spec = (pl.BlockSpec((1, D_MODEL), lambda i, c: (0, 0)) if mrows == 1
                else pl.BlockSpec((tr, D_MODEL), lambda i, c: (i, 0)))
    row = lambda w: pl.BlockSpec((tr, w), lambda i, c: (i, 0))
    const = lambda shape: pl.BlockSpec(shape, lambda i, c: (0, 0))
    return pl.pallas_call(
        _outmlp_kernel,
        grid=(rows // tr, D_FF // ffc),
        in_specs=[row(D_MODEL), row(512), row(512), row(2048), mod_spec, mod_spec, mod_spec, mod_spec,
                  const((1, D_MODEL)), const((512, D_MODEL)), const((512, D_MODEL)),
                  const((D_MODEL, D_MODEL)),
                  pl.BlockSpec((D_MODEL, ffc), lambda i, c: (0, c)),
                  pl.BlockSpec((ffc, D_MODEL), lambda i, c: (c, 0))],
        out_specs=row(D_MODEL),
        out_shape=jax.ShapeDtypeStruct((rows, D_MODEL), F32),
        scratch_shapes=[pltpu.VMEM((tr, D_MODEL), F32), pltpu.VMEM((tr, D_MODEL), BF16),
                        pltpu.VMEM((tr, D_MODEL), F32)],
        compiler_params=_cparams(("arbitrary", "arbitrary")),
        name="outmlp",
    )(x, oa, ob, gm, *mods, g1, wf, wn, wo, wu, wd)


def _prep_w_in(w_in):
    qa, ka, va, zf, qb, zkv, zg, zm = jnp.split(
        w_in, np.cumsum([512, 512, 512, 8, 512, 768, 24, 2048])[:-1].tolist(), axis=1)
    pad = lambda w: jnp.pad(w, ((0, 0), (0, 128 - w.shape[1])))
    return jnp.concatenate([qa, ka, va, qb, zkv, pad(zf), pad(zg), zm], axis=1).astype(BF16)


def _prep_cmp_weights(pe_cmp, w_cmp1, w_cmp2):
    eye = jnp.eye(NSA_G, dtype=F32)
    w1 = w_cmp1.reshape(2, 2, CMP_STRIDE, HEAD_DIM, CMP_HIDDEN)
    w1big = jnp.einsum('khrdj,gq->khrgdqj', w1, eye).reshape(
        2, 2, CMP_STRIDE * NSA_G * HEAD_DIM, NSA_G * CMP_HIDDEN)
    w2big = jnp.einsum('kjd,gq->kgjqd', w_cmp2, eye).reshape(2, NSA_G * CMP_HIDDEN, NSA_G * HEAD_DIM)
    pe = pe_cmp.reshape(2, 2, CMP_STRIDE, 1, HEAD_DIM)
    pe = jnp.broadcast_to(pe, (2, 2, CMP_STRIDE, NSA_G, HEAD_DIM)).reshape(2, 2, 1, CMP_STRIDE * NSA_G * HEAD_DIM)
    return pe, w1big.astype(BF16), w2big.astype(BF16)


def _block_diag_mean(n):
    r = np.arange(n)
    return jnp.asarray((r[:, None] // HEAD_DIM == r[None, :] // HEAD_DIM).astype(np.float32) / HEAD_DIM, BF16)


def kernel(x_prompt, x_sample, cache_fox_kv, cache_fox_logf, cache_nsa_kv, state_nsa_win, page_table,
           c_prompt, c_sample, w_ada, b_ada, g_norm, w_in, b_forget, g_qk_fox, g_qk_nsa,
           pe_cmp, w_cmp1, w_cmp2, rel_bias, w_out_fox, w_out_nsa, w_out, w_up, w_down):
    n_batch, seq, _ = x_prompt.shape
    dec_batch, dec_seq, _ = x_sample.shape
    n_pages = page_table.shape[1]
    past = n_pages * PAGE
    assert n_batch == 1 and dec_seq == 8 and w_ada.shape[0] == 1
    assert seq % PREP_ROWS == 0 and past % PREP_ROWS == 0 and state_nsa_win.shape[2] == WINDOW

    n_c = 1 + dec_batch
    c_rows = -(-n_c // 8) * 8
    c_all = jnp.pad(jnp.concatenate([c_prompt, c_sample], axis=0), ((0, c_rows - n_c), (0, 0)))
    mods = _ada(c_all, w_ada[0], b_ada[0])
    mod_p = [mods[0:1, k * D_MODEL:(k + 1) * D_MODEL] for k in range(6)]
    mod_s = [jnp.repeat(mods[1:n_c, k * D_MODEL:(k + 1) * D_MODEL], dec_seq, axis=0) for k in range(6)]

    w_cat = _prep_w_in(w_in[0])
    bd256 = _block_diag_mean(256)
    tile8 = lambda g: jnp.tile(g, 8)
    gq = jnp.concatenate([tile8(g_qk_fox[0, 0]), tile8(g_qk_fox[0, 1]), tile8(g_qk_nsa[0, 0]),
                          jnp.tile(g_qk_nsa[0, 2], 2), jnp.tile(g_qk_nsa[0, 3], 2)]).reshape(1, 1792)
    bf_pad = jnp.pad(b_forget[0], (0, 120)).reshape(1, 128)
    g0 = g_norm[0, 0].reshape(1, D_MODEL)
    xp = x_prompt.reshape(seq, D_MODEL)
    xs = x_sample.reshape(dec_batch * dec_seq, D_MODEL)
    (p_foxkv, p_nsakv, p_kvwin, p_lft, p_qa, p_kat, p_va, p_qb, p_kwt, p_vwt, p_gb, p_gm) = _inproj(
        xp, mod_p[0], mod_p[1], g0, w_cat, bd256, gq, bf_pad)
    (s_foxkv, s_nsakv, s_kvwin, s_lft, s_qa, s_kat, s_va, s_qb, s_kwt, s_vwt, s_gb, s_gm) = _inproj(
        xs, mod_s[0], mod_s[1], g0, w_cat, bd256, gq, bf_pad)

    ck = _cumsum(p_lft)
    ck_rep = jnp.broadcast_to(ck[:, :, None, :], (FOX_HEADS, seq // LANE, 8, LANE))
    va1 = _with_ones(p_va.reshape(seq, FOX_HEADS, HEAD_DIM), 2).reshape(seq, FOX_HEADS * 2 * HEAD_DIM)
    oa_p = _fox_prompt(p_qa, p_kat, va1, ck_rep)

    pe, w1big, w2big = _prep_cmp_weights(pe_cmp[0], w_cmp1[0], w_cmp2[0])
    bd128 = _block_diag_mean(128)
    gkc = jnp.tile(g_qk_nsa[0, 1], 2).reshape(1, 128)
    look_blocks = PREP_ROWS // CMP_STRIDE
    last_look_p = seq // CMP_STRIDE - 1
    kct_p, vct_p, kst_p, vst_p = _nsaprep(
        [p_nsakv], [pl.BlockSpec((PREP_ROWS, 512), lambda b, t: (t, 0))],
        p_nsakv, pl.BlockSpec((CMP_STRIDE, 512), lambda b, t: (jnp.minimum((t + 1) * look_blocks, last_look_p), 0)),
        (1, seq // PREP_ROWS), 1, seq, (), pe, w1big, w2big, bd128, gkc, False)
    tables_p = _nsa_tables(rel_bias, NSA_TQ, seq // CMP_STRIDE, max(seq // SEL_BLOCK, LANE))
    kwt_pad = jnp.pad(p_kwt, ((0, 0), (WINDOW, 0)))[None]
    vwt_pad = jnp.pad(_with_ones(p_vwt.reshape(NSA_G, HEAD_DIM, seq), 1), ((0, 0), (0, 0), (WINDOW, 0)))[None]
    ob_p = _nsa_attention(p_qb[None], kct_p, vct_p, kst_p, vst_p, None, kwt_pad, vwt_pad, p_gb[None],
                          tables_p, NSA_TQ, 0)[0]

    kv_pool = jnp.transpose(cache_fox_kv[0], (0, 2, 3, 4, 1))
    lft_pool = jnp.transpose(cache_fox_logf[0], (0, 2, 1))
    eye_h = jnp.eye(FOX_HEADS, dtype=BF16)
    qa_s = s_qa.reshape(dec_batch, dec_seq, FOX_HEADS, HEAD_DIM)
    qbd = jnp.einsum('bjhd,hk->bhjkd', qa_s, eye_h).reshape(dec_batch, 64, 512)

    def new_cols(a_t):
        a = jnp.transpose(a_t.reshape(a_t.shape[0], dec_batch, dec_seq), (1, 0, 2))
        return jnp.pad(a, ((0, 0), (0, 0), (0, LANE - dec_seq)))

    knew_t = new_cols(s_kat)
    vnew_t = new_cols(jnp.transpose(s_va))
    lfnew = new_cols(s_lft)
    oa_s = _fox_decode(page_table, qbd, knew_t, vnew_t, lfnew, kv_pool, lft_pool)

    nsa_pool = jnp.transpose(cache_nsa_kv[0], (0, 2, 3, 4, 1))
    npg = NSA_DEC_PAGES
    page_shape = (1, 4, NSA_G, HEAD_DIM, PAGE)
    part_specs = [pl.BlockSpec(page_shape, lambda b, t, pt, k=k: (pt[b, t * npg + k], 0, 0, 0, 0))
                  for k in range(npg)]
    look_spec = pl.BlockSpec(
        page_shape, lambda b, t, pt: (pt[b, jnp.minimum((t + 1) * npg, n_pages - 1)], 0, 0, 0, 0))
    kct_s, vct_s, kst_s, vst_s = _nsaprep(
        [nsa_pool] * npg, part_specs, nsa_pool, look_spec,
        (dec_batch, past // PREP_ROWS), dec_batch, past, (page_table,), pe, w1big, w2big, bd128, gkc, True)
    tile_base = past // NSA_TQ
    nb_dec = past // SEL_BLOCK + NSA_TK // SEL_BLOCK
    tables_s = _nsa_tables(rel_bias, dec_seq, past // CMP_STRIDE, -(-nb_dec // LANE) * LANE, skv=past)
    new_nsa_t = jnp.transpose(s_nsakv.reshape(dec_batch, dec_seq, 512), (0, 2, 1))
    pad_tail = lambda a: jnp.pad(a, ((0, 0), (0, 0), (0, NSA_TK - dec_seq))).astype(BF16)
    kst_tail = pad_tail(new_nsa_t[:, 256:384])
    vst_tail = _with_ones(pad_tail(new_nsa_t[:, 384:512]).reshape(dec_batch, NSA_G, HEAD_DIM, NSA_TK), 2)
    win_t = jnp.transpose(state_nsa_win[0], (0, 2, 3, 4, 1)).reshape(dec_batch, 256, WINDOW)
    new_win_t = jnp.transpose(s_kvwin.reshape(dec_batch, dec_seq, 256), (0, 2, 1))
    band_t = jnp.concatenate([win_t, new_win_t], axis=2)
    band_pad = jnp.pad(band_t, ((0, 0), (0, 0), (0, WIN_BAND - band_t.shape[2]))).astype(BF16)
    ob_s = _nsa_attention(s_qb.reshape(dec_batch, dec_seq, 512), kct_s, vct_s, kst_s, vst_s,
                          (kst_tail, vst_tail), band_pad[:, 0:128],
                          _with_ones(band_pad[:, 128:256].reshape(dec_batch, NSA_G, HEAD_DIM, WIN_BAND), 2),
                          s_gb.reshape(dec_batch, dec_seq, 128), tables_s, dec_seq, tile_base)

    g1 = g_norm[0, 1].reshape(1, D_MODEL)
    wf, wn, wo = w_out_fox[0].astype(BF16), w_out_nsa[0].astype(BF16), w_out[0].astype(BF16)
    wu, wd = w_up[0].astype(BF16), w_down[0].astype(BF16)
    y_p = _out_mlp(xp, oa_p, ob_p, p_gm, mod_p[2:6], g1, wf, wn, wo, wu, wd, 512)
    y_s = _out_mlp(xs, oa_s.reshape(dec_batch * dec_seq, 512), ob_s.reshape(dec_batch * dec_seq, 512),
                   s_gm, mod_s[2:6], g1, wf, wn, wo, wu, wd, dec_batch * dec_seq)

    win_keep = min(WINDOW, seq)
    new_win_sample = jnp.transpose(band_t[:, :, dec_seq:].reshape(dec_batch, 2, NSA_G, HEAD_DIM, WINDOW),
                                   (0, 4, 1, 2, 3))[None]
    return (
        y_p.reshape(1, seq, D_MODEL),
        y_s.reshape(dec_batch, dec_seq, D_MODEL),
        p_foxkv.reshape(1, 1, seq, 2, FOX_HEADS, HEAD_DIM),
        s_foxkv.reshape(1, dec_batch, dec_seq, 2, FOX_HEADS, HEAD_DIM),
        jnp.transpose(p_lft).reshape(1, 1, seq, FOX_HEADS),
        jnp.transpose(s_lft).reshape(1, dec_batch, dec_seq, FOX_HEADS),
        p_nsakv.reshape(1, 1, seq, 4, NSA_G, HEAD_DIM),
        s_nsakv.reshape(1, dec_batch, dec_seq, 4, NSA_G, HEAD_DIM),
        p_kvwin[seq - win_keep:].reshape(1, 1, win_keep, 2, NSA_G, HEAD_DIM),
        new_win_sample,
    )
```
